```python
import math
import jax, jax.numpy as jnp
from jax import lax
import numpy as np

D_MODEL = 1024
BATCH = 8
SEQ = 2048
DEPTH = 1
DEC_BATCH = 8
DEC_SEQ = 64
PAST_LEN = 2048

CHUNK = 64
A_HEADS = 4
A_DK = 128
A_DV = 128
A_WIDTH = A_HEADS * A_DV
B_HEADS = 8
B_KV_HEADS = 4
B_HD = 64
B_WIDTH = B_HEADS * B_HD
IDX_HEADS = 8
IDX_DIM = 64
TOPK_MAX = 256
QBLOCK = 128
ROT_FRAC = 4
ROPE_THETA = 500000.0
MIX_WIDTH = A_WIDTH + B_WIDTH
D_FF = 4 * D_MODEL
EPS = 1e-6
IN_SIZES = (A_HEADS * A_DK, A_HEADS * A_DK, A_WIDTH, A_WIDTH,
            B_HEADS * B_HD, B_KV_HEADS * B_HD, B_KV_HEADS * B_HD,
            IDX_HEADS * IDX_DIM, IDX_DIM, IDX_HEADS)
IN_WIDTH = sum(IN_SIZES)

kernel_name = "hymba_hgrn2_dsa_streaming_step"


def rmsnorm(x, g):
    xf = x.astype(jnp.float32)
    y = xf * lax.rsqrt(jnp.mean(jnp.square(xf), axis=-1, keepdims=True) + EPS)
    return (y * g.astype(jnp.float32)).astype(x.dtype)


def split_cols(z):
    pts, acc = [], 0
    for s in IN_SIZES[:-1]:
        acc += s
        pts.append(acc)
    return jnp.split(z, pts, axis=-1)


def partial_rope(x, pos):
    rot = x.shape[-1] // ROT_FRAC
    half = rot // 2
    inv = jnp.power(ROPE_THETA, -jnp.arange(half, dtype=jnp.float32) * (2.0 / rot))
    ang = pos.astype(jnp.float32)[:, None] * inv[None, :]
    cos = jnp.cos(ang)[None, :, None, :]
    sin = jnp.sin(ang)[None, :, None, :]
    xr = x[..., :rot].astype(jnp.float32)
    x1, x2 = xr[..., :half], xr[..., half:]
    out = jnp.concatenate([x1 * cos - x2 * sin, x2 * cos + x1 * sin], axis=-1).astype(x.dtype)
    return jnp.concatenate([out, x[..., rot:]], axis=-1)


def hgrn2_mix(q_pre, f_pre, v, gate_pre, lb, s0, g_norm):
    bsz, T = q_pre.shape[:2]
    C = min(CHUNK, T)
    N = T // C
    q = jax.nn.silu(q_pre.astype(jnp.float32))
    f = lb + (1.0 - lb) * jax.nn.sigmoid(f_pre.astype(jnp.float32))
    logf = jnp.log(f)
    k = 1.0 - f
    vf = v.astype(jnp.float32)

    def to_chunks(a):
        return jnp.moveaxis(a.reshape(bsz, N, C, *a.shape[2:]), 1, 0)

    mask = jnp.tril(jnp.ones((C, C), dtype=bool))[None, :, :, None, None]

    def step(S, inp):
        qc, kc, vc, lc = inp
        b = jnp.cumsum(lc, axis=1)
        diff = b[:, :, None] - b[:, None, :]
        decay = jnp.exp(jnp.where(mask, diff, -jnp.inf))
        attn = jnp.einsum('bthk,bshk,btshk->bhts', qc, kc, decay)
        o = (jnp.einsum('bhts,bshv->bthv', attn, vc)
             + jnp.einsum('bthk,bhkv->bthv', qc * jnp.exp(b), S))
        bl = b[:, -1]
        kd = kc * jnp.exp(bl[:, None] - b)
        S_new = jnp.exp(bl)[..., None] * S + jnp.einsum('bshk,bshv->bhkv', kd, vc)
        return S_new, o

    S, o = lax.scan(step, s0.astype(jnp.float32),
                    (to_chunks(q), to_chunks(k), to_chunks(vf), to_chunks(logf)))
    o = jnp.moveaxis(o, 0, 1).reshape(bsz, T, A_HEADS, A_DV)
    o = rmsnorm(o, g_norm.reshape(A_HEADS, A_DV)) * jax.nn.silu(gate_pre.astype(jnp.float32))
    return o.reshape(bsz, T, A_WIDTH), S


def dsa_attend(q, k_all, v_all, qi, ki_all, wi, q_pos):
    bsz, Tq = q.shape[:2]
    Tk = k_all.shape[1]
    topk = min(TOPK_MAX, Tk // 4)
    qb = min(QBLOCK, Tq)
    nb = Tq // qb
    G = B_HEADS // B_KV_HEADS
    scale = B_HD ** -0.5
    key_chunk = jnp.arange(Tk) // CHUNK
    ki32 = ki_all.astype(jnp.float32)

    def blocks(a):
        return jnp.moveaxis(a.reshape(bsz, nb, qb, *a.shape[2:]), 1, 0)

    def block(args):
        qc, qic, wic, qp = args
        qchunk = qp // CHUNK
        s_idx = jnp.einsum('bqhd,bkd->bqhk', qic.astype(jnp.float32), ki32)
        score = jnp.einsum('bqh,bqhk->bqk', wic.astype(jnp.float32), jax.nn.relu(s_idx))
        visible = key_chunk[None, :] <= qchunk[:, None]
        score = jnp.where(visible[None], score, -jnp.inf)
        _, sel = lax.top_k(score, topk)
        valid = (sel // CHUNK) <= qchunk[None, :, None]
        kg = jax.vmap(lambda kk, ss: kk[ss])(k_all, sel)
        vg = jax.vmap(lambda vv, ss: vv[ss])(v_all, sel)
        qg = qc.reshape(bsz, qb, B_KV_HEADS, G, B_HD).astype(jnp.float32)
        logits = jnp.einsum('bqngd,bqjnd->bqngj', qg, kg.astype(jnp.float32)) * scale
        logits = jnp.where(valid[:, :, None, None, :], logits, -jnp.inf)
        p = jax.nn.softmax(logits, axis=-1)
        o = jnp.einsum('bqngj,bqjnd->bqngd', p, vg.astype(jnp.float32))
        return o.reshape(bsz, qb, B_WIDTH)

    o = lax.map(block, (blocks(q), blocks(qi), blocks(wi), q_pos.reshape(nb, qb)))
    return jnp.moveaxis(o, 0, 1).reshape(bsz, Tq, B_WIDTH)


def trunk_layer(x, c, pos, s0, k_past, v_past, ki_past,
                w_mod, b_mod, norm1, w_in, lb, g_norm_a, w_out, norm2, w_ff1, w_ff2):
    bsz, T = x.shape[:2]
    mod = (jax.nn.silu(c) @ w_mod + b_mod)[:, None, :]
    sh1, sc1, g1, sh2, sc2, g2 = jnp.split(mod, 6, axis=-1)
    h = rmsnorm(x, norm1) * (1.0 + sc1) + sh1
    z = h @ w_in
    qa, fa, ia, ga, qB, kB, vB, qI, kI, wI = split_cols(z)
    o_a, S = hgrn2_mix(qa.reshape(bsz, T, A_HEADS, A_DK), fa.reshape(bsz, T, A_HEADS, A_DK),
                       ia.reshape(bsz, T, A_HEADS, A_DV), ga.reshape(bsz, T, A_HEADS, A_DV),
                       lb, s0, g_norm_a)
    qB = partial_rope(qB.reshape(bsz, T, B_HEADS, B_HD), pos)
    kB = partial_rope(kB.reshape(bsz, T, B_KV_HEADS, B_HD), pos)
    vB = vB.reshape(bsz, T, B_KV_HEADS, B_HD)
    qI = partial_rope(qI.reshape(bsz, T, IDX_HEADS, IDX_DIM), pos)
    kI = partial_rope(kI[:, :, None, :], pos)[:, :, 0, :]
    wI = wI * ((IDX_HEADS * IDX_DIM) ** -0.5)
    if k_past is None:
        k_all, v_all, ki_all = kB, vB, kI
    else:
        k_all = jnp.concatenate([k_past.astype(kB.dtype), kB], axis=1)
        v_all = jnp.concatenate([v_past.astype(vB.dtype), vB], axis=1)
        ki_all = jnp.concatenate([ki_past.astype(kI.dtype), kI], axis=1)
    o_b = dsa_attend(qB, k_all, v_all, qI, ki_all, wI, pos)
    mix = jnp.concatenate([o_a.astype(x.dtype), o_b.astype(x.dtype)], axis=-1) @ w_out
    x = x + g1 * mix
    h2 = rmsnorm(x, norm2) * (1.0 + sc2) + sh2
    x = x + g2 * (jnp.square(jax.nn.relu(h2 @ w_ff1)) @ w_ff2)
    return x, kB, vB, kI, S


def setup_inputs(seed: int = 0) -> dict:
    key = jax.random.key(seed)
    ks = jax.random.split(key, 20)
    f32 = jnp.float32

    def nrm(k, shape, s):
        return jax.random.normal(k, shape, f32) * s

    return {
        "x_prompt": nrm(ks[0], (BATCH, SEQ, D_MODEL), 1.0),
        "x_sample": nrm(ks[1], (DEC_BATCH, DEC_SEQ, D_MODEL), 1.0),
        "cache_k": nrm(ks[2], (DEPTH, DEC_BATCH, PAST_LEN, B_KV_HEADS, B_HD), 1.0),
        "cache_v": nrm(ks[3], (DEPTH, DEC_BATCH, PAST_LEN, B_KV_HEADS, B_HD), 1.0),
        "cache_k_idx": nrm(ks[4], (DEPTH, DEC_BATCH, PAST_LEN, IDX_DIM), 1.0),
        "state_hgrn": nrm(ks[5], (DEPTH, DEC_BATCH, A_HEADS, A_DK, A_DV), 0.5),
        "c_prompt": nrm(ks[6], (BATCH, D_MODEL), 1.0),
        "c_sample": nrm(ks[7], (DEC_BATCH, D_MODEL), 1.0),
        "w_mod": nrm(ks[8], (DEPTH, D_MODEL, 6 * D_MODEL), 0.2 * D_MODEL ** -0.5),
        "b_mod": nrm(ks[9], (DEPTH, 6 * D_MODEL), 0.02),
        "norm1": 1.0 + nrm(ks[10], (DEPTH, D_MODEL), 0.02),
        "w_in": nrm(ks[11], (DEPTH, D_MODEL, IN_WIDTH), D_MODEL ** -0.5),
        "lb_logits": nrm(ks[12], (DEPTH + 1, A_HEADS * A_DK), 0.5),
        "g_norm_a": 1.0 + nrm(ks[13], (DEPTH, A_WIDTH), 0.02),
        "w_out": nrm(ks[14], (DEPTH, MIX_WIDTH, D_MODEL), MIX_WIDTH ** -0.5),
        "norm2": 1.0 + nrm(ks[15], (DEPTH, D_MODEL), 0.02),
        "w_ff1": nrm(ks[16], (DEPTH, D_MODEL, D_FF), D_MODEL ** -0.5),
        "w_ff2": nrm(ks[17], (DEPTH, D_FF, D_MODEL), D_FF ** -0.5),
        "norm_f": 1.0 + nrm(ks[18], (D_MODEL,), 0.02),
    }


def reference(x_prompt, x_sample, cache_k, cache_v, cache_k_idx, state_hgrn, c_prompt, c_sample,
              w_mod, b_mod, norm1, w_in, lb_logits, g_norm_a, w_out, norm2, w_ff1, w_ff2, norm_f):
    lb_all = jnp.cumsum(jax.nn.softmax(lb_logits.astype(jnp.float32), axis=0), axis=0)
    past = cache_k.shape[2]
    pos_p = jnp.arange(x_prompt.shape[1])
    pos_s = past + jnp.arange(x_sample.shape[1])
    hp, hs = x_prompt, x_sample
    kp_l, vp_l, kip_l, sp_l = [], [], [], []
    ks_l, vs_l, kis_l, ss_l = [], [], [], []
    for l in range(DEPTH):
        lw = (w_mod[l], b_mod[l], norm1[l], w_in[l], lb_all[l].reshape(A_HEADS, A_DK),
              g_norm_a[l], w_out[l], norm2[l], w_ff1[l], w_ff2[l])
        s0 = jnp.zeros((hp.shape[0], A_HEADS, A_DK, A_DV), jnp.float32)
        hp, kp, vp, kip, sp = trunk_layer(hp, c_prompt, pos_p, s0, None, None, None, *lw)
        hs, k_s, v_s, ki_s, s_s = trunk_layer(hs, c_sample, pos_s, state_hgrn[l],
                                              cache_k[l], cache_v[l], cache_k_idx[l], *lw)
        kp_l.append(kp); vp_l.append(vp); kip_l.append(kip); sp_l.append(sp)
        ks_l.append(k_s); vs_l.append(v_s); kis_l.append(ki_s); ss_l.append(s_s)
    y_prompt = rmsnorm(hp, norm_f)
    y_sample = rmsnorm(hs, norm_f)
    return (y_prompt, y_sample,
            jnp.stack(kp_l), jnp.stack(vp_l), jnp.stack(kip_l), jnp.stack(sp_l),
            jnp.stack(ks_l), jnp.stack(vs_l), jnp.stack(kis_l), jnp.stack(ss_l))
```

```python
import functools

import numpy as np
import jax
import jax.numpy as jnp
from jax import lax
from jax.experimental import pallas as pl
from jax.experimental.pallas import tpu as pltpu

D_MODEL = 1024
CHUNK = 64
A_HEADS = 4
A_DK = 128
A_DV = 128
A_WIDTH = A_HEADS * A_DV
B_HEADS = 8
B_KV_HEADS = 4
B_HD = 64
B_WIDTH = B_HEADS * B_HD
KV_WIDTH = B_KV_HEADS * B_HD
IDX_HEADS = 8
IDX_DIM = 64
IDX_WIDTH = IDX_HEADS * IDX_DIM
TOPK_MAX = 256
QBLOCK = 128
ROT_FRAC = 4
ROPE_THETA = 500000.0
D_FF = 4 * D_MODEL
EPS = 1e-6
IN_WIDTH = 4 * A_WIDTH + B_WIDTH + 2 * KV_WIDTH + IDX_WIDTH + IDX_DIM + IDX_HEADS

LANES = 128
IN_WIDTH_PAD = -(-IN_WIDTH // LANES) * LANES
VMEM_LIMIT_BYTES = 56 * 1024 * 1024

F32 = jnp.float32
BF16 = jnp.bfloat16
INT_MIN = np.int32(-2 ** 31)
NEG_BIG = -1e30

OFF_HG = 0
OFF_QB = 4 * A_WIDTH
OFF_KB = OFF_QB + B_WIDTH
OFF_VB = OFF_KB + KV_WIDTH
OFF_QI = OFF_VB + KV_WIDTH
OFF_KI = OFF_QI + IDX_WIDTH
OFF_WI = OFF_KI + IDX_DIM


def _dot(a, b):
    return jnp.dot(a, b, preferred_element_type=F32)


def _dot_nt(a, b):
    return lax.dot_general(a, b, (((1,), (1,)), ((), ())), preferred_element_type=F32)


def _silu(x):
    return x * jax.nn.sigmoid(x)


def _rms(x):
    return x * lax.rsqrt(jnp.mean(jnp.square(x), axis=-1, keepdims=True) + EPS)


def _mod_kernel(c_ref, w_ref, b_ref, o_ref):
    a = _silu(c_ref[...])
    o_ref[...] = jnp.dot(a, w_ref[...], preferred_element_type=F32,
                         precision=lax.Precision.HIGHEST) + b_ref[...]


def _mod_call(c, w_mod, b_mod):
    rows, d = c.shape
    n = w_mod.shape[1]
    tn = 1024
    return pl.pallas_call(
        _mod_kernel,
        grid=(n // tn,),
        in_specs=[pl.BlockSpec((rows, d), lambda j: (0, 0)),
                  pl.BlockSpec((d, tn), lambda j: (0, j)),
                  pl.BlockSpec((1, tn), lambda j: (0, j))],
        out_specs=pl.BlockSpec((rows, tn), lambda j: (0, j)),
        out_shape=jax.ShapeDtypeStruct((rows, n), F32),
        compiler_params=pltpu.CompilerParams(vmem_limit_bytes=VMEM_LIMIT_BYTES),
        name="mod",
    )(c, w_mod, b_mod.reshape(1, n))


def _rope(x, cos, sin_lo, sin_hi):
    half = B_HD // ROT_FRAC // 2
    return (x * cos + pltpu.roll(x, half, 1) * sin_hi
            + pltpu.roll(x, LANES - half, 1) * sin_lo)


def _inproj_kernel(x_ref, mod_ref, n1_ref, w_ref, cos_ref, slo_ref, shi_ref,
                   hg_ref, qb_ref, k_ref, v_ref, qi_ref, ki_ref, wi_ref, *, bb, tt):
    rows = bb * tt
    x = x_ref[...]
    mod = mod_ref[...]
    sh1 = mod[:, :, 0:D_MODEL]
    sc1 = mod[:, :, D_MODEL:2 * D_MODEL]
    h = (_rms(x) * n1_ref[...]) * (1.0 + sc1) + sh1
    h = h.reshape(rows, D_MODEL).astype(BF16)
    z = _dot(h, w_ref[...])
    cos, slo, shi = cos_ref[...], slo_ref[...], shi_ref[...]

    def rope_cols(off, width):
        return [_rope(z[:, off + j:off + j + LANES], cos, slo, shi)
                for j in range(0, width, LANES)]

    hg_ref[...] = z[:, OFF_HG:OFF_QB].reshape(bb, tt, 4 * A_WIDTH)
    scale = B_HD ** -0.5
    qb = jnp.concatenate(rope_cols(OFF_QB, B_WIDTH), axis=1) * scale
    qb_ref[...] = qb.astype(BF16).reshape(bb, tt, B_WIDTH)
    kb = jnp.concatenate(rope_cols(OFF_KB, KV_WIDTH), axis=1)
    k_ref[...] = kb.reshape(bb, tt, KV_WIDTH)
    v_ref[...] = z[:, OFF_VB:OFF_QI].reshape(bb, tt, KV_WIDTH)
    qi = jnp.concatenate(rope_cols(OFF_QI, IDX_WIDTH), axis=1)
    qi_ref[...] = qi.astype(BF16).reshape(bb, tt, IDX_WIDTH)
    last = _rope(z[:, OFF_KI:OFF_KI + LANES], cos, slo, shi)
    ki_ref[...] = last[:, 0:IDX_DIM].reshape(bb, tt, IDX_DIM)
    wi = z[:, OFF_WI:OFF_WI + IDX_HEADS] * (IDX_WIDTH ** -0.5)
    wi_ref[...] = wi.reshape(bb, tt, IDX_HEADS)


def _rope_tables(pos, reps):
    rot = B_HD // ROT_FRAC
    half = rot // 2
    inv = jnp.power(ROPE_THETA, -jnp.arange(half, dtype=F32) * (2.0 / rot))
    ang = pos.astype(F32)[:, None] * inv[None, :]
    cos, sin = jnp.cos(ang), jnp.sin(ang)
    t = pos.shape[0]
    ones = jnp.ones((t, B_HD - rot), F32)
    zeros = jnp.zeros((t, B_HD - rot), F32)
    zh = jnp.zeros((t, half), F32)
    cos_h = jnp.concatenate([cos, cos, ones], axis=1)
    slo_h = jnp.concatenate([-sin, zh, zeros], axis=1)
    shi_h = jnp.concatenate([zh, sin, zeros], axis=1)
    per = LANES // B_HD
    return tuple(jnp.tile(a, (reps, per)) for a in (cos_h, slo_h, shi_h))


def _inproj_call(x, mod, norm1, w_in_bf, pos, bb, tt):
    b, t, d = x.shape
    cos, slo, shi = _rope_tables(pos, bb)
    rows = bb * tt
    if bb == 1:
        tab_spec = pl.BlockSpec((tt, LANES), lambda i, j: (j, 0))
    else:
        tab_spec = pl.BlockSpec((rows, LANES), lambda i, j: (0, 0))

    def act_spec(w):
        return pl.BlockSpec((bb, tt, w), lambda i, j: (i, j, 0))

    def out(w, dt):
        return jax.ShapeDtypeStruct((b, t, w), dt)

    return pl.pallas_call(
        functools.partial(_inproj_kernel, bb=bb, tt=tt),
        grid=(b // bb, t // tt),
        in_specs=[act_spec(d),
                  pl.BlockSpec((bb, 1, 6 * d), lambda i, j: (i, 0, 0)),
                  pl.BlockSpec((1, 1, d), lambda i, j: (0, 0, 0)),
                  pl.BlockSpec((d, IN_WIDTH_PAD), lambda i, j: (0, 0)),
                  tab_spec, tab_spec, tab_spec],
        out_specs=[act_spec(4 * A_WIDTH), act_spec(B_WIDTH), act_spec(KV_WIDTH),
                   act_spec(KV_WIDTH), act_spec(IDX_WIDTH), act_spec(IDX_DIM),
                   act_spec(IDX_HEADS)],
        out_shape=[out(4 * A_WIDTH, F32), out(B_WIDTH, BF16), out(KV_WIDTH, F32),
                   out(KV_WIDTH, F32), out(IDX_WIDTH, BF16), out(IDX_DIM, F32),
                   out(IDX_HEADS, F32)],
        compiler_params=pltpu.CompilerParams(
            dimension_semantics=("parallel", "parallel"),
            vmem_limit_bytes=VMEM_LIMIT_BYTES),
        name="inproj",
    )(x, mod.reshape(b, 1, 6 * d), norm1.reshape(1, 1, d), w_in_bf, cos, slo, shi)


def _hgrn_tables(cc):
    nlev = int(np.log2(cc))
    t = np.arange(cc)[:, None]
    u = np.arange(cc)[None, :]
    mats = []
    for l in range(nlev):
        m = cc >> (l + 1)
        ref = (t // (2 * m)) * (2 * m) + m - 1
        qside = ((t // m) % 2) == 1
        mats.append(np.where(qside, (u > ref) & (u <= t), (u > t) & (u <= ref)))
    mats.append(u <= t)
    mats.append(u > t)
    w = np.concatenate(mats, axis=0).astype(np.float32)
    lvl = np.full((cc, cc), -1, np.int32)
    for l in range(nlev):
        m = cc >> (l + 1)
        same_parent = (t // (2 * m)) == (u // (2 * m))
        lvl[same_parent & ((t // m) % 2 == 1) & ((u // m) % 2 == 0)] = l
    lvl[np.arange(cc), np.arange(cc)] = nlev
    return jnp.asarray(w, BF16), jnp.asarray(lvl), nlev


def _hgrn_kernel(q_ref, f_ref, i_ref, g_ref, lb_ref, gn_ref, s0_ref, w_ref, lvl_ref,
                 o_ref, s_out_ref, st_scr, *, cc, nlev):
    ci = pl.program_id(1)

    @pl.when(ci == 0)
    def _():
        for h in range(A_HEADS):
            st_scr[h] = s0_ref[0, h].T

    row = lax.broadcasted_iota(jnp.int32, (cc, A_DK), 0)
    lvl = lvl_ref[...]
    w = w_ref[...]
    for h in range(A_HEADS):
        sl = slice(h * A_DK, (h + 1) * A_DK)
        lb = lb_ref[:, sl]
        q = _silu(q_ref[0, :, sl])
        f = lb + (1.0 - lb) * jax.nn.sigmoid(f_ref[0, :, sl])
        nl = -jnp.log(f)
        kk = 1.0 - f
        v = i_ref[0, :, sl]
        nl_hi = nl.astype(BF16)
        nl_lo = (nl - nl_hi.astype(F32)).astype(BF16)
        dd = _dot(w, nl_hi) + _dot(w, nl_lo)
        attn = jnp.zeros((cc, cc), F32)
        for l in range(nlev):
            m = cc >> (l + 1)
            e = jnp.exp(-dd[l * cc:(l + 1) * cc])
            qside = ((row // m) % 2) == 1
            xl = (jnp.where(qside, q, kk) * e).astype(BF16)
            attn = attn + jnp.where(lvl == l, _dot_nt(xl, xl), 0.0)
        attn = attn + jnp.where(lvl == nlev, _dot_nt(q.astype(BF16), kk.astype(BF16)), 0.0)
        bcum = dd[nlev * cc:(nlev + 1) * cc]
        brev = dd[(nlev + 1) * cc:(nlev + 2) * cc]
        qg = (q * jnp.exp(-bcum)).astype(BF16)
        kg = (kk * jnp.exp(-brev)).astype(BF16)
        st = st_scr[h]
        o = _dot(attn.astype(BF16), v.astype(BF16)) + _dot_nt(qg, st.astype(BF16))
        dec = jnp.exp(-bcum[cc - 1:cc, :])
        st_scr[h] = st * dec + _dot(v.T.astype(BF16), kg)
        y = _rms(o) * gn_ref[:, sl]
        o_ref[0, :, sl] = (y * _silu(g_ref[0, :, sl])).astype(BF16)

    @pl.when(ci == pl.num_programs(1) - 1)
    def _():
        for h in range(A_HEADS):
            s_out_ref[0, h] = st_scr[h].T


def _hgrn_call(hg, lb, g_norm, s0, cc):
    b, t, _ = hg.shape
    w, lvl, nlev = _hgrn_tables(cc)

    def part(p):
        return pl.BlockSpec((1, cc, A_WIDTH), lambda i, j, p=p: (i, j, p))

    return pl.pallas_call(
        functools.partial(_hgrn_kernel, cc=cc, nlev=nlev),
        grid=(b, t // cc),
        in_specs=[part(0), part(1), part(2), part(3),
                  pl.BlockSpec((1, A_WIDTH), lambda i, j: (0, 0)),
                  pl.BlockSpec((1, A_WIDTH), lambda i, j: (0, 0)),
                  pl.BlockSpec((1, A_HEADS, A_DK, A_DV), lambda i, j: (i, 0, 0, 0)),
                  pl.BlockSpec(w.shape, lambda i, j: (0, 0)),
                  pl.BlockSpec(lvl.shape, lambda i, j: (0, 0))],
        out_specs=[pl.BlockSpec((1, cc, A_WIDTH), lambda i, j: (i, j, 0)),
                   pl.BlockSpec((1, A_HEADS, A_DK, A_DV), lambda i, j: (i, 0, 0, 0))],
        out_shape=[jax.ShapeDtypeStruct((b, t, A_WIDTH), BF16),
                   jax.ShapeDtypeStruct((b, A_HEADS, A_DK, A_DV), F32)],
        scratch_shapes=[pltpu.VMEM((A_HEADS, A_DV, A_DK), F32)],
        compiler_params=pltpu.CompilerParams(
            dimension_semantics=("parallel", "arbitrary"),
            vmem_limit_bytes=VMEM_LIMIT_BYTES),
        name="hgrn",
    )(hg, hg, hg, hg, lb.reshape(1, A_WIDTH), g_norm.reshape(1, A_WIDTH), s0, w, lvl)


def _dsa_kernel(qi_ref, wi_ref, qb_ref, ki_ref, k_ref, v_ref, tri_ref, o_ref, key_scr,
                *, qb, tkp, tk, topk, pos0):
    qi = qi_ref[0]
    ki = ki_ref[0]
    wi = wi_ref[0]
    score = jnp.zeros((qb, tkp), F32)
    for h in range(IDX_HEADS):
        s = _dot_nt(qi[:, h * IDX_DIM:(h + 1) * IDX_DIM], ki)
        score = score + wi[:, h:h + 1] * jnp.maximum(s, 0.0)

    kidx = lax.broadcasted_iota(jnp.int32, (qb, tkp), 1)
    qpos = pos0 + pl.program_id(1) * qb + lax.broadcasted_iota(jnp.int32, (qb, tkp), 0)
    visible = ((kidx // CHUNK) <= (qpos // CHUNK)) & (kidx < tk)

    score = jnp.where(score == 0.0, 0.0, score)
    bits = pltpu.bitcast(score, jnp.int32)
    key = bits ^ ((bits >> 31) & np.int32(0x7FFFFFFF))
    key = jnp.where(visible, key, INT_MIN)
    key_scr[...] = key

    def bisect(i, t_u):
        cand_u = t_u | (jnp.int32(1) << (31 - i))
        cnt = jnp.sum((key_scr[...] >= (cand_u ^ INT_MIN)).astype(jnp.int32),
                      axis=1, keepdims=True)
        return jnp.where(cnt >= topk, cand_u, t_u)

    t_u = lax.fori_loop(0, 32, bisect, jnp.zeros((qb, 1), jnp.int32))
    thr = t_u ^ INT_MIN

    key = key_scr[...]
    above = key > thr
    tie = key == thr
    need = topk - jnp.sum(above.astype(jnp.int32), axis=1, keepdims=True)
    tie_f = jnp.where(tie, 1.0, 0.0)
    tri = tri_ref[...]
    offs = jnp.zeros((qb, 1), F32)
    ranks = []
    for j in range(0, tkp, LANES):
        tj = tie_f[:, j:j + LANES]
        ranks.append(_dot(tj.astype(BF16), tri) + offs)
        offs = offs + jnp.sum(tj, axis=1, keepdims=True)
    rank = jnp.concatenate(ranks, axis=1)
    sel = visible & (above | (tie & (rank < need.astype(F32))))

    kb = k_ref[0]
    vb = v_ref[0]
    qq = qb_ref[0]
    group = B_HEADS // B_KV_HEADS
    for n in range(B_KV_HEADS):
        kn = kb[:, n * B_HD:(n + 1) * B_HD]
        vn = vb[:, n * B_HD:(n + 1) * B_HD]
        for g in range(group):
            hq = n * group + g
            lg = _dot_nt(qq[:, hq * B_HD:(hq + 1) * B_HD], kn)
            lg = jnp.where(sel, lg, NEG_BIG)
            p = jnp.exp(lg - jnp.max(lg, axis=1, keepdims=True))
            den = jnp.sum(p, axis=1, keepdims=True)
            pv = _dot(p.astype(BF16), vn)
            o_ref[0, :, hq * B_HD:(hq + 1) * B_HD] = (pv / den).astype(BF16)


def _dsa_call(qi, wi, qbs, ki_all, k_all, v_all, tk, pos0):
    b, t, _ = qi.shape
    tkp = ki_all.shape[1]
    qb = min(QBLOCK, t)
    topk = min(TOPK_MAX, tk // 4)
    tri = jnp.asarray(np.triu(np.ones((LANES, LANES), np.float32), 1), BF16)

    def q_spec(w):
        return pl.BlockSpec((1, qb, w), lambda i, j: (i, j, 0))

    def kv_spec(w):
        return pl.BlockSpec((1, tkp, w), lambda i, j: (i, 0, 0))

    return pl.pallas_call(
        functools.partial(_dsa_kernel, qb=qb, tkp=tkp, tk=tk, topk=topk, pos0=pos0),
        grid=(b, t // qb),
        in_specs=[q_spec(IDX_WIDTH), q_spec(IDX_HEADS), q_spec(B_WIDTH),
                  kv_spec(IDX_DIM), kv_spec(KV_WIDTH), kv_spec(KV_WIDTH),
                  pl.BlockSpec((LANES, LANES), lambda i, j: (0, 0))],
        out_specs=q_spec(B_WIDTH),
        out_shape=jax.ShapeDtypeStruct((b, t, B_WIDTH), BF16),
        scratch_shapes=[pltpu.VMEM((qb, tkp), jnp.int32)],
        compiler_params=pltpu.CompilerParams(
            dimension_semantics=("parallel", "parallel"),
            vmem_limit_bytes=VMEM_LIMIT_BYTES),
        name="dsa",
    )(qi, wi, qbs, ki_all, k_all, v_all, tri)


def _out_kernel(x_ref, oa_ref, ob_ref, mod_ref, n2_ref, nf_ref, wo_ref, w1_ref, w2_ref,
                y_ref, *, bb, tt):
    rows = bb * tt
    x = x_ref[...]
    mod = mod_ref[...]
    g1 = mod[:, :, 2 * D_MODEL:3 * D_MODEL]
    sh2 = mod[:, :, 3 * D_MODEL:4 * D_MODEL]
    sc2 = mod[:, :, 4 * D_MODEL:5 * D_MODEL]
    g2 = mod[:, :, 5 * D_MODEL:6 * D_MODEL]
    oa = oa_ref[...].reshape(rows, A_WIDTH)
    ob = ob_ref[...].reshape(rows, B_WIDTH)
    mix = _dot(oa, wo_ref[0:A_WIDTH, :]) + _dot(ob, wo_ref[A_WIDTH:A_WIDTH + B_WIDTH, :])
    x = x + g1 * mix.reshape(bb, tt, D_MODEL)
    h2 = (_rms(x) * n2_ref[...]) * (1.0 + sc2) + sh2
    u = _dot(h2.reshape(rows, D_MODEL).astype(BF16), w1_ref[...])
    r = jnp.square(jnp.maximum(u, 0.0)).astype(BF16)
    x = x + g2 * _dot(r, w2_ref[...]).reshape(bb, tt, D_MODEL)
    y_ref[...] = _rms(x) * nf_ref[...]


def _out_call(x, oa, ob, mod, norm2, norm_f, wo_bf, w1_bf, w2_bf, bb, tt):
    b, t, d = x.shape

    def act_spec(w):
        return pl.BlockSpec((bb, tt, w), lambda i, j: (i, j, 0))

    def const_spec(shape):
        zeros = (0,) * len(shape)
        return pl.BlockSpec(shape, lambda i, j: zeros, pipeline_mode=pl.Buffered(1))

    return pl.pallas_call(
        functools.partial(_out_kernel, bb=bb, tt=tt),
        grid=(b // bb, t // tt),
        in_specs=[act_spec(d), act_spec(A_WIDTH), act_spec(B_WIDTH),
                  pl.BlockSpec((bb, 1, 6 * d), lambda i, j: (i, 0, 0)),
                  const_spec((1, 1, d)), const_spec((1, 1, d)),
                  const_spec(wo_bf.shape), const_spec(w1_bf.shape), const_spec(w2_bf.shape)],
        out_specs=act_spec(d),
        out_shape=jax.ShapeDtypeStruct((b, t, d), F32),
        compiler_params=pltpu.CompilerParams(
            dimension_semantics=("parallel", "parallel"),
            vmem_limit_bytes=VMEM_LIMIT_BYTES),
        name="out",
    )(x, oa, ob, mod.reshape(b, 1, 6 * d), norm2.reshape(1, 1, d), norm_f.reshape(1, 1, d),
      wo_bf, w1_bf, w2_bf)


def _pad_keys(a, tkp):
    return jnp.pad(a, ((0, 0), (0, tkp - a.shape[1]), (0, 0))).astype(BF16)


def _layer(x, mod, pos0, s0, k_past, v_past, ki_past, weights, bb, tt, cc):
    norm1, w_in_bf, lb, g_norm, wo_bf, norm2, w1_bf, w2_bf, norm_f = weights
    b, t, _ = x.shape
    pos = pos0 + jnp.arange(t)
    hg, qbs, k_new, v_new, qi, ki_new, wi = _inproj_call(x, mod, norm1, w_in_bf, pos, bb, tt)
    oa, s_new = _hgrn_call(hg, lb, g_norm, s0, cc)
    if k_past is None:
        k_all, v_all, ki_all = k_new, v_new, ki_new
    else:
        k_all = jnp.concatenate([k_past.reshape(b, -1, KV_WIDTH), k_new], axis=1)
        v_all = jnp.concatenate([v_past.reshape(b, -1, KV_WIDTH), v_new], axis=1)
        ki_all = jnp.concatenate([ki_past, ki_new], axis=1)
    tk = k_all.shape[1]
    tkp = -(-tk // LANES) * LANES
    ob = _dsa_call(qi, wi, qbs, _pad_keys(ki_all, tkp), _pad_keys(k_all, tkp),
                   _pad_keys(v_all, tkp), tk, pos0)
    y = _out_call(x, oa, ob, mod, norm2, norm_f, wo_bf, w1_bf, w2_bf, bb, tt)
    return (y, k_new.reshape(b, t, B_KV_HEADS, B_HD), v_new.reshape(b, t, B_KV_HEADS, B_HD),
            ki_new, s_new)


def kernel(x_prompt, x_sample, cache_k, cache_v, cache_k_idx, state_hgrn, c_prompt, c_sample,
           w_mod, b_mod, norm1, w_in, lb_logits, g_norm_a, w_out, norm2, w_ff1, w_ff2, norm_f):
    depth = w_in.shape[0]
    assert depth == 1, "kernel is written for the single-layer configuration"
    lb_all = jnp.cumsum(jax.nn.softmax(lb_logits.astype(F32), axis=0), axis=0)
    bp, tp, _ = x_prompt.shape
    bs, ts, _ = x_sample.shape
    past = cache_k.shape[2]
    l = 0
    mod = _mod_call(jnp.concatenate([c_prompt, c_sample], axis=0), w_mod[l], b_mod[l])
    w_in_bf = jnp.pad(w_in[l], ((0, 0), (0, IN_WIDTH_PAD - IN_WIDTH))).astype(BF16)
    weights = (norm1[l], w_in_bf, lb_all[l], g_norm_a[l], w_out[l].astype(BF16), norm2[l],
               w_ff1[l].astype(BF16), w_ff2[l].astype(BF16), norm_f)
    s0 = jnp.zeros((bp, A_HEADS, A_DK, A_DV), F32)
    yp, kp, vp, kip, sp = _layer(x_prompt, mod[:bp], 0, s0, None, None, None, weights,
                                 bb=1, tt=min(512, tp), cc=min(128, tp))
    ys, ks, vs, kis, ss = _layer(x_sample, mod[bp:], past, state_hgrn[l], cache_k[l],
                                 cache_v[l], cache_k_idx[l], weights,
                                 bb=bs, tt=ts, cc=min(128, ts))
    return (yp, ys, kp[None], vp[None], kip[None], sp[None],
            ks[None], vs[None], kis[None], ss[None])
```

```python
import functools

import numpy as np
import jax
import jax.numpy as jnp
from jax import lax
from jax.experimental import pallas as pl
from jax.experimental.pallas import tpu as pltpu

D_MODEL = 1024
CHUNK = 64
A_HEADS = 4
A_DK = 128
A_DV = 128
A_WIDTH = A_HEADS * A_DV
B_HEADS = 8
B_KV_HEADS = 4
B_HD = 64
B_WIDTH = B_HEADS * B_HD
KV_WIDTH = B_KV_HEADS * B_HD
IDX_HEADS = 8
IDX_DIM = 64
IDX_WIDTH = IDX_HEADS * IDX_DIM
TOPK_MAX = 256
QBLOCK = 128
ROT_FRAC = 4
ROPE_THETA = 500000.0
D_FF = 4 * D_MODEL
EPS = 1e-6
IN_WIDTH = 4 * A_WIDTH + B_WIDTH + 2 * KV_WIDTH + IDX_WIDTH + IDX_DIM + IDX_HEADS

LANES = 128
SUBLANES = 8
KEY_TILE = 256
DSA_QUERIES = 256
IN_WIDTH_PAD = -(-IN_WIDTH // LANES) * LANES
VMEM_LIMIT_BYTES = 56 * 1024 * 1024

F32 = jnp.float32
BF16 = jnp.bfloat16
INT_MIN = np.int32(-2 ** 31)
NEG_BIG = -1e30
LOG2_E = 1.4426950408889634

OFF_HG = 0
OFF_QB = 4 * A_WIDTH
OFF_KB = OFF_QB + B_WIDTH
OFF_VB = OFF_KB + KV_WIDTH
OFF_QI = OFF_VB + KV_WIDTH
OFF_KI = OFF_QI + IDX_WIDTH
OFF_WI = OFF_KI + IDX_DIM


def _dot(a, b):
    return jnp.dot(a, b, preferred_element_type=F32)


def _dot_nt(a, b):
    return lax.dot_general(a, b, (((1,), (1,)), ((), ())), preferred_element_type=F32)


def _silu(x):
    return x * jax.nn.sigmoid(x)


def _rms(x):
    return x * lax.rsqrt(jnp.mean(jnp.square(x), axis=-1, keepdims=True) + EPS)


def _mod_kernel(c_ref, w_ref, b_ref, o_ref):
    a = _silu(c_ref[...])
    o_ref[...] = jnp.dot(a, w_ref[...], preferred_element_type=F32,
                         precision=lax.Precision.HIGHEST) + b_ref[...]


def _mod_call(c, w_mod, b_mod):
    rows, d = c.shape
    n = w_mod.shape[1]
    tn = 1024
    return pl.pallas_call(
        _mod_kernel,
        grid=(n // tn,),
        in_specs=[pl.BlockSpec((rows, d), lambda j: (0, 0)),
                  pl.BlockSpec((d, tn), lambda j: (0, j)),
                  pl.BlockSpec((1, tn), lambda j: (0, j))],
        out_specs=pl.BlockSpec((rows, tn), lambda j: (0, j)),
        out_shape=jax.ShapeDtypeStruct((rows, n), F32),
        compiler_params=pltpu.CompilerParams(vmem_limit_bytes=VMEM_LIMIT_BYTES),
        name="mod",
    )(c, w_mod, b_mod.reshape(1, n))


def _rope(x, cos, sin_lo, sin_hi):
    half = B_HD // ROT_FRAC // 2
    return (x * cos + pltpu.roll(x, half, 1) * sin_hi
            + pltpu.roll(x, LANES - half, 1) * sin_lo)


def _inproj_kernel(x_ref, mod_ref, n1_ref, w_ref, cos_ref, slo_ref, shi_ref,
                   hg_ref, qb_ref, k_ref, v_ref, qi_ref, ki_ref, wi_ref, *, bb, tt):
    rows = bb * tt
    x = x_ref[...]
    mod = mod_ref[...]
    sh1 = mod[:, :, 0:D_MODEL]
    sc1 = mod[:, :, D_MODEL:2 * D_MODEL]
    h = (_rms(x) * n1_ref[...]) * (1.0 + sc1) + sh1
    h = h.reshape(rows, D_MODEL).astype(BF16)
    z = _dot(h, w_ref[...])
    cos, slo, shi = cos_ref[...], slo_ref[...], shi_ref[...]

    def rope_cols(off, width):
        return [_rope(z[:, off + j:off + j + LANES], cos, slo, shi)
                for j in range(0, width, LANES)]

    hg_ref[...] = z[:, OFF_HG:OFF_QB].reshape(bb, tt, 4 * A_WIDTH)
    scale = B_HD ** -0.5 * LOG2_E
    qb = jnp.concatenate(rope_cols(OFF_QB, B_WIDTH), axis=1) * scale
    qb_ref[...] = qb.astype(BF16).reshape(bb, tt, B_WIDTH)
    kb = jnp.concatenate(rope_cols(OFF_KB, KV_WIDTH), axis=1)
    k_ref[...] = kb.reshape(bb, tt, KV_WIDTH)
    v_ref[...] = z[:, OFF_VB:OFF_QI].reshape(bb, tt, KV_WIDTH)
    qi = jnp.concatenate(rope_cols(OFF_QI, IDX_WIDTH), axis=1)
    qi_ref[...] = qi.astype(BF16).reshape(bb, tt, IDX_WIDTH)
    last = _rope(z[:, OFF_KI:OFF_KI + LANES], cos, slo, shi)
    ki_ref[...] = last[:, 0:IDX_DIM].reshape(bb, tt, IDX_DIM)
    wi = z[:, OFF_WI:OFF_WI + IDX_HEADS] * (IDX_WIDTH ** -0.5)
    wi_ref[...] = wi.reshape(bb, tt, IDX_HEADS)


def _rope_tables(pos, reps):
    rot = B_HD // ROT_FRAC
    half = rot // 2
    inv = jnp.power(ROPE_THETA, -jnp.arange(half, dtype=F32) * (2.0 / rot))
    ang = pos.astype(F32)[:, None] * inv[None, :]
    cos, sin = jnp.cos(ang), jnp.sin(ang)
    t = pos.shape[0]
    ones = jnp.ones((t, B_HD - rot), F32)
    zeros = jnp.zeros((t, B_HD - rot), F32)
    zh = jnp.zeros((t, half), F32)
    cos_h = jnp.concatenate([cos, cos, ones], axis=1)
    slo_h = jnp.concatenate([-sin, zh, zeros], axis=1)
    shi_h = jnp.concatenate([zh, sin, zeros], axis=1)
    per = LANES // B_HD
    return tuple(jnp.tile(a, (reps, per)) for a in (cos_h, slo_h, shi_h))


def _inproj_call(x, mod, norm1, w_in_bf, pos, bb, tt):
    b, t, d = x.shape
    cos, slo, shi = _rope_tables(pos, bb)
    rows = bb * tt
    if bb == 1:
        tab_spec = pl.BlockSpec((tt, LANES), lambda i, j: (j, 0))
    else:
        tab_spec = pl.BlockSpec((rows, LANES), lambda i, j: (0, 0))

    def act_spec(w):
        return pl.BlockSpec((bb, tt, w), lambda i, j: (i, j, 0))

    def out(w, dt):
        return jax.ShapeDtypeStruct((b, t, w), dt)

    return pl.pallas_call(
        functools.partial(_inproj_kernel, bb=bb, tt=tt),
        grid=(b // bb, t // tt),
        in_specs=[act_spec(d),
                  pl.BlockSpec((bb, 1, 6 * d), lambda i, j: (i, 0, 0)),
                  pl.BlockSpec((1, 1, d), lambda i, j: (0, 0, 0)),
                  pl.BlockSpec((d, IN_WIDTH_PAD), lambda i, j: (0, 0)),
                  tab_spec, tab_spec, tab_spec],
        out_specs=[act_spec(4 * A_WIDTH), act_spec(B_WIDTH), act_spec(KV_WIDTH),
                   act_spec(KV_WIDTH), act_spec(IDX_WIDTH), act_spec(IDX_DIM),
                   act_spec(IDX_HEADS)],
        out_shape=[out(4 * A_WIDTH, F32), out(B_WIDTH, BF16), out(KV_WIDTH, F32),
                   out(KV_WIDTH, F32), out(IDX_WIDTH, BF16), out(IDX_DIM, F32),
                   out(IDX_HEADS, F32)],
        compiler_params=pltpu.CompilerParams(
            dimension_semantics=("parallel", "parallel"),
            vmem_limit_bytes=VMEM_LIMIT_BYTES),
        name="inproj",
    )(x, mod.reshape(b, 1, 6 * d), norm1.reshape(1, 1, d), w_in_bf, cos, slo, shi)


def _hgrn_tables(cc):
    nlev = int(np.log2(cc))
    t = np.arange(cc)[:, None]
    u = np.arange(cc)[None, :]
    mats = []
    for l in range(nlev):
        m = cc >> (l + 1)
        ref = (t // (2 * m)) * (2 * m) + m - 1
        qside = ((t // m) % 2) == 1
        mats.append(np.where(qside, (u > ref) & (u <= t), (u > t) & (u <= ref)))
    mats.append(u <= t)
    mats.append(u > t)
    w = np.concatenate(mats, axis=0).astype(np.float32)
    lvl = np.full((cc, cc), -1, np.int32)
    for l in range(nlev):
        m = cc >> (l + 1)
        same_parent = (t // (2 * m)) == (u // (2 * m))
        lvl[same_parent & ((t // m) % 2 == 1) & ((u // m) % 2 == 0)] = l
    lvl[np.arange(cc), np.arange(cc)] = nlev
    return jnp.asarray(w, BF16), jnp.asarray(lvl), nlev


def _hgrn_kernel(q_ref, f_ref, i_ref, g_ref, lb_ref, gn_ref, s0_ref, w_ref, lvl_ref,
                 o_ref, s_out_ref, st_scr, *, cc, nlev):
    ci = pl.program_id(1)

    @pl.when(ci == 0)
    def _():
        for h in range(A_HEADS):
            st_scr[h] = s0_ref[0, h].T

    row = lax.broadcasted_iota(jnp.int32, (cc, A_DK), 0)
    lvl = lvl_ref[...]
    w = w_ref[...]
    for h in range(A_HEADS):
        sl = slice(h * A_DK, (h + 1) * A_DK)
        lb = lb_ref[:, sl]
        q = _silu(q_ref[0, :, sl])
        f = lb + (1.0 - lb) * jax.nn.sigmoid(f_ref[0, :, sl])
        nl = -jnp.log(f)
        kk = 1.0 - f
        v = i_ref[0, :, sl]
        nl_hi = nl.astype(BF16)
        nl_lo = (nl - nl_hi.astype(F32)).astype(BF16)
        dd = _dot(w, nl_hi) + _dot(w, nl_lo)
        attn = jnp.zeros((cc, cc), F32)
        for l in range(nlev):
            m = cc >> (l + 1)
            e = jnp.exp(-dd[l * cc:(l + 1) * cc])
            qside = ((row // m) % 2) == 1
            xl = (jnp.where(qside, q, kk) * e).astype(BF16)
            attn = attn + jnp.where(lvl == l, _dot_nt(xl, xl), 0.0)
        attn = attn + jnp.where(lvl == nlev, _dot_nt(q.astype(BF16), kk.astype(BF16)), 0.0)
        bcum = dd[nlev * cc:(nlev + 1) * cc]
        brev = dd[(nlev + 1) * cc:(nlev + 2) * cc]
        qg = (q * jnp.exp(-bcum)).astype(BF16)
        kg = (kk * jnp.exp(-brev)).astype(BF16)
        st = st_scr[h]
        o = _dot(attn.astype(BF16), v.astype(BF16)) + _dot_nt(qg, st.astype(BF16))
        dec = jnp.exp(-bcum[cc - 1:cc, :])
        st_scr[h] = st * dec + _dot(v.T.astype(BF16), kg)
        y = _rms(o) * gn_ref[:, sl]
        o_ref[0, :, sl] = (y * _silu(g_ref[0, :, sl])).astype(BF16)

    @pl.when(ci == pl.num_programs(1) - 1)
    def _():
        for h in range(A_HEADS):
            s_out_ref[0, h] = st_scr[h].T


def _hgrn_call(hg, lb, g_norm, s0, cc):
    b, t, _ = hg.shape
    w, lvl, nlev = _hgrn_tables(cc)

    def part(p):
        return pl.BlockSpec((1, cc, A_WIDTH), lambda i, j, p=p: (i, j, p))

    return pl.pallas_call(
        functools.partial(_hgrn_kernel, cc=cc, nlev=nlev),
        grid=(b, t // cc),
        in_specs=[part(0), part(1), part(2), part(3),
                  pl.BlockSpec((1, A_WIDTH), lambda i, j: (0, 0)),
                  pl.BlockSpec((1, A_WIDTH), lambda i, j: (0, 0)),
                  pl.BlockSpec((1, A_HEADS, A_DK, A_DV), lambda i, j: (i, 0, 0, 0)),
                  pl.BlockSpec(w.shape, lambda i, j: (0, 0)),
                  pl.BlockSpec(lvl.shape, lambda i, j: (0, 0))],
        out_specs=[pl.BlockSpec((1, cc, A_WIDTH), lambda i, j: (i, j, 0)),
                   pl.BlockSpec((1, A_HEADS, A_DK, A_DV), lambda i, j: (i, 0, 0, 0))],
        out_shape=[jax.ShapeDtypeStruct((b, t, A_WIDTH), BF16),
                   jax.ShapeDtypeStruct((b, A_HEADS, A_DK, A_DV), F32)],
        scratch_shapes=[pltpu.VMEM((A_HEADS, A_DV, A_DK), F32)],
        compiler_params=pltpu.CompilerParams(
            dimension_semantics=("parallel", "arbitrary"),
            vmem_limit_bytes=VMEM_LIMIT_BYTES),
        name="hgrn",
    )(hg, hg, hg, hg, lb.reshape(1, A_WIDTH), g_norm.reshape(1, A_WIDTH), s0, w, lvl)


def _fold8(x, op):
    parts = [x[r:r + SUBLANES] for r in range(0, x.shape[0], SUBLANES)]
    while len(parts) > 1:
        parts = [op(parts[i], parts[i + 1]) for i in range(0, len(parts) - 1, 2)] + (
            [parts[-1]] if len(parts) % 2 else [])
    return parts[0]


def _stack_heads(x, a, b, width):
    return jnp.concatenate([x[:, a * width:(a + 1) * width], x[:, b * width:(b + 1) * width]],
                           axis=0)


def _dsa_kernel(qi_ref, wit_ref, qb_ref, ki_ref, k_ref, vt_ref, tri_ref, o_ref,
                score_scr, logit_scr, acc_scr, *, qb, qreal, tk, topk, pos0):
    blk = pl.program_id(1)
    last_pos = pos0 + (blk + 1) * qreal - 1
    extent = jnp.minimum((last_pos // CHUNK + 1) * CHUNK, tk)
    ntile = (extent + KEY_TILE - 1) // KEY_TILE
    qpos = pos0 + blk * qreal + lax.broadcasted_iota(jnp.int32, (1, qb), 1)
    key_end = jnp.minimum((qpos // CHUNK + 1) * CHUNK, tk)
    neg_inf = jnp.float32(-jnp.inf)

    qi = qi_ref[0]
    wit = wit_ref[0]
    qi_pairs = [_stack_heads(qi, 2 * p, 2 * p + 1, IDX_DIM) for p in range(IDX_HEADS // 2)]

    def tile_start(j):
        return pl.multiple_of(j * KEY_TILE, KEY_TILE)

    def score_tile(j, carry):
        k0 = tile_start(j)
        ki_t = ki_ref[0, pl.ds(k0, KEY_TILE), :]
        acc = jnp.zeros((KEY_TILE, qb), F32)
        for p in range(IDX_HEADS // 2):
            s2 = jnp.maximum(_dot_nt(ki_t, qi_pairs[p]), 0.0)
            acc = acc + wit[2 * p:2 * p + 1, :] * s2[:, :qb]
            acc = acc + wit[2 * p + 1:2 * p + 2, :] * s2[:, qb:]
        kidx = k0 + lax.broadcasted_iota(jnp.int32, (KEY_TILE, qb), 0)
        score_scr[pl.ds(k0, KEY_TILE), :] = jnp.where(kidx < key_end, acc, neg_inf)
        return carry

    lax.fori_loop(0, ntile, score_tile, 0)

    def count(pred_fn):
        def body(j, acc):
            s = score_scr[pl.ds(tile_start(j), KEY_TILE), :]
            return acc + _fold8(jnp.where(pred_fn(s), 1.0, 0.0), jnp.add)
        acc = lax.fori_loop(0, ntile, body, jnp.zeros((SUBLANES, qb), F32))
        return jnp.sum(acc, axis=0, keepdims=True)

    def decode(t_u):
        key = t_u ^ INT_MIN
        return pltpu.bitcast(key ^ ((key >> 31) & np.int32(0x7FFFFFFF)), F32)

    def bisect(i, t_u):
        cand_u = t_u | (jnp.int32(1) << (31 - i))
        cand = decode(cand_u)
        cnt = count(lambda s: s >= cand)
        return jnp.where(cnt >= topk, cand_u, t_u)

    thr = decode(lax.fori_loop(0, 32, bisect, jnp.zeros((1, qb), jnp.int32)))
    thr = jnp.where(thr != thr, neg_inf, thr)
    need = topk - count(lambda s: s > thr)
    need = jnp.where(thr == neg_inf, 0.0, need)

    qq = qb_ref[0]
    group = B_HEADS // B_KV_HEADS
    q_pairs = [_stack_heads(qq, n * group, n * group + 1, B_HD) for n in range(B_KV_HEADS)]
    tri = tri_ref[...]

    def logit_tile(j, carry):
        offs, mx = carry
        k0 = tile_start(j)
        s = score_scr[pl.ds(k0, KEY_TILE), :]
        tie = jnp.where(s == thr, 1.0, 0.0)
        rank = _dot(tri, tie.astype(BF16)) + offs
        picked = jnp.where(s > thr, 1.0, jnp.where(rank < need, tie, 0.0))
        bias = jnp.where(picked > 0.0, 0.0, NEG_BIG)
        bias2 = jnp.concatenate([bias, bias], axis=1)
        new_mx = []
        for n in range(B_KV_HEADS):
            lg = _dot_nt(k_ref[0, n, pl.ds(k0, KEY_TILE), :], q_pairs[n]) + bias2
            logit_scr[n, pl.ds(k0, KEY_TILE), :] = lg
            new_mx.append(jnp.maximum(mx[n], _fold8(lg, jnp.maximum)))
        offs = offs + jnp.sum(_fold8(tie, jnp.add), axis=0, keepdims=True)
        return offs, tuple(new_mx)

    mx0 = tuple(jnp.full((SUBLANES, 2 * qb), NEG_BIG, F32) for _ in range(B_KV_HEADS))
    _, mx = lax.fori_loop(0, ntile, logit_tile, (jnp.zeros((1, qb), F32), mx0))
    mx = [jnp.max(m, axis=0, keepdims=True) for m in mx]

    acc_scr[...] = jnp.zeros_like(acc_scr)

    def pv_tile(j, den):
        k0 = tile_start(j)
        new_den = []
        for n in range(B_KV_HEADS):
            p = jnp.exp2(logit_scr[n, pl.ds(k0, KEY_TILE), :] - mx[n])
            new_den.append(den[n] + _fold8(p, jnp.add))
            vt = vt_ref[0, n * B_HD:(n + 1) * B_HD, pl.ds(k0, KEY_TILE)]
            acc_scr[n] += _dot(vt, p.astype(BF16))
        return tuple(new_den)

    den0 = tuple(jnp.zeros((SUBLANES, 2 * qb), F32) for _ in range(B_KV_HEADS))
    den = lax.fori_loop(0, ntile, pv_tile, den0)
    rows = []
    for n in range(B_KV_HEADS):
        o2 = acc_scr[n] / jnp.sum(den[n], axis=0, keepdims=True)
        rows += [o2[:, :qb], o2[:, qb:]]
    o_ref[0] = jnp.concatenate(rows, axis=0).T.astype(BF16)


def _dsa_call(qi, wi, qbs, ki_all, k_all, v_all, tk, pos0):
    b, t, _ = qi.shape
    qb = DSA_QUERIES if t % DSA_QUERIES == 0 else LANES
    qreal = min(qb, t)
    tpad = -(-t // qb) * qb
    tkp = -(-tk // KEY_TILE) * KEY_TILE
    topk = min(TOPK_MAX, tk // 4)
    assert qreal == qb or t == qreal, "a partial query block must be the only one"
    assert topk <= KEY_TILE

    def pad_q(a):
        return jnp.pad(a, ((0, 0), (0, tpad - t), (0, 0)))

    def pad_k(a):
        return jnp.pad(a, ((0, 0), (0, tkp - tk), (0, 0))).astype(BF16)

    wit = jnp.swapaxes(pad_q(wi), 1, 2)
    k_hm = jnp.swapaxes(pad_k(k_all).reshape(b, tkp, B_KV_HEADS, B_HD), 1, 2)
    vt = jnp.swapaxes(pad_k(v_all), 1, 2)
    tri = jnp.asarray(np.tril(np.ones((KEY_TILE, KEY_TILE), np.float32), -1), BF16)

    def q_spec(w):
        return pl.BlockSpec((1, qb, w), lambda i, j: (i, j, 0))

    out = pl.pallas_call(
        functools.partial(_dsa_kernel, qb=qb, qreal=qreal, tk=tk, topk=topk, pos0=pos0),
        grid=(b, tpad // qb),
        in_specs=[q_spec(IDX_WIDTH),
                  pl.BlockSpec((1, IDX_HEADS, qb), lambda i, j: (i, 0, j)),
                  q_spec(B_WIDTH),
                  pl.BlockSpec((1, tkp, IDX_DIM), lambda i, j: (i, 0, 0)),
                  pl.BlockSpec((1, B_KV_HEADS, tkp, B_HD), lambda i, j: (i, 0, 0, 0)),
                  pl.BlockSpec((1, KV_WIDTH, tkp), lambda i, j: (i, 0, 0)),
                  pl.BlockSpec((KEY_TILE, KEY_TILE), lambda i, j: (0, 0))],
        out_specs=q_spec(B_WIDTH),
        out_shape=jax.ShapeDtypeStruct((b, tpad, B_WIDTH), BF16),
        scratch_shapes=[pltpu.VMEM((tkp, qb), F32),
                        pltpu.VMEM((B_KV_HEADS, tkp, 2 * qb), F32),
                        pltpu.VMEM((B_KV_HEADS, B_HD, 2 * qb), F32)],
        compiler_params=pltpu.CompilerParams(
            dimension_semantics=("parallel", "parallel"),
            vmem_limit_bytes=VMEM_LIMIT_BYTES),
        name="dsa",
    )(pad_q(qi), wit, pad_q(qbs), pad_k(ki_all), k_hm, vt, tri)
    return out[:, :t]


def _out_kernel(x_ref, oa_ref, ob_ref, mod_ref, n2_ref, nf_ref, wo_ref, w1_ref, w2_ref,
                y_ref, *, bb, tt):
    rows = bb * tt
    x = x_ref[...]
    mod = mod_ref[...]
    g1 = mod[:, :, 2 * D_MODEL:3 * D_MODEL]
    sh2 = mod[:, :, 3 * D_MODEL:4 * D_MODEL]
    sc2 = mod[:, :, 4 * D_MODEL:5 * D_MODEL]
    g2 = mod[:, :, 5 * D_MODEL:6 * D_MODEL]
    oa = oa_ref[...].reshape(rows, A_WIDTH)
    ob = ob_ref[...].reshape(rows, B_WIDTH)
    mix = _dot(oa, wo_ref[0:A_WIDTH, :]) + _dot(ob, wo_ref[A_WIDTH:A_WIDTH + B_WIDTH, :])
    x = x + g1 * mix.reshape(bb, tt, D_MODEL)
    h2 = (_rms(x) * n2_ref[...]) * (1.0 + sc2) + sh2
    u = _dot(h2.reshape(rows, D_MODEL).astype(BF16), w1_ref[...])
    r = jnp.square(jnp.maximum(u, 0.0)).astype(BF16)
    x = x + g2 * _dot(r, w2_ref[...]).reshape(bb, tt, D_MODEL)
    y_ref[...] = _rms(x) * nf_ref[...]


def _out_call(x, oa, ob, mod, norm2, norm_f, wo_bf, w1_bf, w2_bf, bb, tt):
    b, t, d = x.shape

    def act_spec(w):
        return pl.BlockSpec((bb, tt, w), lambda i, j: (i, j, 0))

    def const_spec(shape):
        zeros = (0,) * len(shape)
        return pl.BlockSpec(shape, lambda i, j: zeros, pipeline_mode=pl.Buffered(1))

    return pl.pallas_call(
        functools.partial(_out_kernel, bb=bb, tt=tt),
        grid=(b // bb, t // tt),
        in_specs=[act_spec(d), act_spec(A_WIDTH), act_spec(B_WIDTH),
                  pl.BlockSpec((bb, 1, 6 * d), lambda i, j: (i, 0, 0)),
                  const_spec((1, 1, d)), const_spec((1, 1, d)),
                  const_spec(wo_bf.shape), const_spec(w1_bf.shape), const_spec(w2_bf.shape)],
        out_specs=act_spec(d),
        out_shape=jax.ShapeDtypeStruct((b, t, d), F32),
        compiler_params=pltpu.CompilerParams(
            dimension_semantics=("parallel", "parallel"),
            vmem_limit_bytes=VMEM_LIMIT_BYTES),
        name="out",
    )(x, oa, ob, mod.reshape(b, 1, 6 * d), norm2.reshape(1, 1, d), norm_f.reshape(1, 1, d),
      wo_bf, w1_bf, w2_bf)


def _layer(x, mod, pos0, s0, k_past, v_past, ki_past, weights, bb, tt, cc):
    norm1, w_in_bf, lb, g_norm, wo_bf, norm2, w1_bf, w2_bf, norm_f = weights
    b, t, _ = x.shape
    pos = pos0 + jnp.arange(t)
    hg, qbs, k_new, v_new, qi, ki_new, wi = _inproj_call(x, mod, norm1, w_in_bf, pos, bb, tt)
    oa, s_new = _hgrn_call(hg, lb, g_norm, s0, cc)
    if k_past is None:
        k_all, v_all, ki_all = k_new, v_new, ki_new
    else:
        k_all = jnp.concatenate([k_past.reshape(b, -1, KV_WIDTH), k_new], axis=1)
        v_all = jnp.concatenate([v_past.reshape(b, -1, KV_WIDTH), v_new], axis=1)
        ki_all = jnp.concatenate([ki_past, ki_new], axis=1)
    ob = _dsa_call(qi, wi, qbs, ki_all, k_all, v_all, k_all.shape[1], pos0)
    y = _out_call(x, oa, ob, mod, norm2, norm_f, wo_bf, w1_bf, w2_bf, bb, tt)
    return (y, k_new.reshape(b, t, B_KV_HEADS, B_HD), v_new.reshape(b, t, B_KV_HEADS, B_HD),
            ki_new, s_new)


def kernel(x_prompt, x_sample, cache_k, cache_v, cache_k_idx, state_hgrn, c_prompt, c_sample,
           w_mod, b_mod, norm1, w_in, lb_logits, g_norm_a, w_out, norm2, w_ff1, w_ff2, norm_f):
    depth = w_in.shape[0]
    assert depth == 1, "kernel is written for the single-layer configuration"
    lb_all = jnp.cumsum(jax.nn.softmax(lb_logits.astype(F32), axis=0), axis=0)
    bp, tp, _ = x_prompt.shape
    bs, ts, _ = x_sample.shape
    past = cache_k.shape[2]
    l = 0
    mod = _mod_call(jnp.concatenate([c_prompt, c_sample], axis=0), w_mod[l], b_mod[l])
    w_in_bf = jnp.pad(w_in[l], ((0, 0), (0, IN_WIDTH_PAD - IN_WIDTH))).astype(BF16)
    weights = (norm1[l], w_in_bf, lb_all[l], g_norm_a[l], w_out[l].astype(BF16), norm2[l],
               w_ff1[l].astype(BF16), w_ff2[l].astype(BF16), norm_f)
    s0 = jnp.zeros((bp, A_HEADS, A_DK, A_DV), F32)
    yp, kp, vp, kip, sp = _layer(x_prompt, mod[:bp], 0, s0, None, None, None, weights,
                                 bb=1, tt=min(512, tp), cc=min(128, tp))
    ys, ks, vs, kis, ss = _layer(x_sample, mod[bp:], past, state_hgrn[l], cache_k[l],
                                 cache_v[l], cache_k_idx[l], weights,
                                 bb=bs, tt=ts, cc=min(128, ts))
    return (yp, ys, kp[None], vp[None], kip[None], sp[None],
            ks[None], vs[None], kis[None], ss[None])
```

```python
import functools

import numpy as np
import jax
import jax.numpy as jnp
from jax import lax
from jax.experimental import pallas as pl
from jax.experimental.pallas import tpu as pltpu

D_MODEL = 1024
CHUNK = 64
A_HEADS = 4
A_DK = 128
A_DV = 128
A_WIDTH = A_HEADS * A_DV
B_HEADS = 8
B_KV_HEADS = 4
B_HD = 64
B_WIDTH = B_HEADS * B_HD
KV_WIDTH = B_KV_HEADS * B_HD
IDX_HEADS = 8
IDX_DIM = 64
IDX_WIDTH = IDX_HEADS * IDX_DIM
TOPK_MAX = 256
QBLOCK = 128
ROT_FRAC = 4
ROPE_THETA = 500000.0
D_FF = 4 * D_MODEL
EPS = 1e-6
IN_WIDTH = 4 * A_WIDTH + B_WIDTH + 2 * KV_WIDTH + IDX_WIDTH + IDX_DIM + IDX_HEADS

LANES = 128
SUBLANES = 8
KEY_TILE = 256
DSA_QUERIES = 256
IN_WIDTH_PAD = -(-IN_WIDTH // LANES) * LANES
VMEM_LIMIT_BYTES = 56 * 1024 * 1024

F32 = jnp.float32
BF16 = jnp.bfloat16
INT_MIN = np.int32(-2 ** 31)
NEG_BIG = -1e30
LOG2_E = 1.4426950408889634

OFF_HG = 0
OFF_QB = 4 * A_WIDTH
OFF_KB = OFF_QB + B_WIDTH
OFF_VB = OFF_KB + KV_WIDTH
OFF_QI = OFF_VB + KV_WIDTH
OFF_KI = OFF_QI + IDX_WIDTH
OFF_WI = OFF_KI + IDX_DIM


def _dot(a, b):
    return jnp.dot(a, b, preferred_element_type=F32)


def _dot_nt(a, b):
    return lax.dot_general(a, b, (((1,), (1,)), ((), ())), preferred_element_type=F32)


def _silu(x):
    return x * jax.nn.sigmoid(x)


def _rms(x):
    return x * lax.rsqrt(jnp.mean(jnp.square(x), axis=-1, keepdims=True) + EPS)


def _mod_kernel(c_ref, w_ref, b_ref, o_ref):
    a = _silu(c_ref[...])
    o_ref[...] = jnp.dot(a, w_ref[...], preferred_element_type=F32,
                         precision=lax.Precision.HIGHEST) + b_ref[...]


def _mod_call(c, w_mod, b_mod):
    rows, d = c.shape
    n = w_mod.shape[1]
    tn = 1024
    return pl.pallas_call(
        _mod_kernel,
        grid=(n // tn,),
        in_specs=[pl.BlockSpec((rows, d), lambda j: (0, 0)),
                  pl.BlockSpec((d, tn), lambda j: (0, j)),
                  pl.BlockSpec((1, tn), lambda j: (0, j))],
        out_specs=pl.BlockSpec((rows, tn), lambda j: (0, j)),
        out_shape=jax.ShapeDtypeStruct((rows, n), F32),
        compiler_params=pltpu.CompilerParams(vmem_limit_bytes=VMEM_LIMIT_BYTES),
        name="mod",
    )(c, w_mod, b_mod.reshape(1, n))


def _rope(x, cos, sin_lo, sin_hi):
    half = B_HD // ROT_FRAC // 2
    return (x * cos + pltpu.roll(x, half, 1) * sin_hi
            + pltpu.roll(x, LANES - half, 1) * sin_lo)


def _inproj_kernel(x_ref, mod_ref, n1_ref, w_ref, cos_ref, slo_ref, shi_ref,
                   hg_ref, qb_ref, k_ref, v_ref, qi_ref, ki_ref, wi_ref, *, bb, tt):
    rows = bb * tt
    x = x_ref[...]
    mod = mod_ref[...]
    sh1 = mod[:, :, 0:D_MODEL]
    sc1 = mod[:, :, D_MODEL:2 * D_MODEL]
    h = (_rms(x) * n1_ref[...]) * (1.0 + sc1) + sh1
    h = h.reshape(rows, D_MODEL).astype(BF16)
    z = _dot(h, w_ref[...])
    cos, slo, shi = cos_ref[...], slo_ref[...], shi_ref[...]

    def rope_cols(off, width):
        return [_rope(z[:, off + j:off + j + LANES], cos, slo, shi)
                for j in range(0, width, LANES)]

    hg_ref[...] = z[:, OFF_HG:OFF_QB].reshape(bb, tt, 4 * A_WIDTH)
    scale = B_HD ** -0.5 * LOG2_E
    qb = jnp.concatenate(rope_cols(OFF_QB, B_WIDTH), axis=1) * scale
    qb_ref[...] = qb.astype(BF16).reshape(bb, tt, B_WIDTH)
    kb = jnp.concatenate(rope_cols(OFF_KB, KV_WIDTH), axis=1)
    k_ref[...] = kb.reshape(bb, tt, KV_WIDTH)
    v_ref[...] = z[:, OFF_VB:OFF_QI].reshape(bb, tt, KV_WIDTH)
    qi = jnp.concatenate(rope_cols(OFF_QI, IDX_WIDTH), axis=1)
    qi_ref[...] = qi.astype(BF16).reshape(bb, tt, IDX_WIDTH)
    last = _rope(z[:, OFF_KI:OFF_KI + LANES], cos, slo, shi)
    ki_ref[...] = last[:, 0:IDX_DIM].reshape(bb, tt, IDX_DIM)
    wi = z[:, OFF_WI:OFF_WI + IDX_HEADS] * (IDX_WIDTH ** -0.5)
    wi_ref[...] = wi.reshape(bb, tt, IDX_HEADS)


def _rope_tables(pos, reps):
    rot = B_HD // ROT_FRAC
    half = rot // 2
    inv = jnp.power(ROPE_THETA, -jnp.arange(half, dtype=F32) * (2.0 / rot))
    ang = pos.astype(F32)[:, None] * inv[None, :]
    cos, sin = jnp.cos(ang), jnp.sin(ang)
    t = pos.shape[0]
    ones = jnp.ones((t, B_HD - rot), F32)
    zeros = jnp.zeros((t, B_HD - rot), F32)
    zh = jnp.zeros((t, half), F32)
    cos_h = jnp.concatenate([cos, cos, ones], axis=1)
    slo_h = jnp.concatenate([-sin, zh, zeros], axis=1)
    shi_h = jnp.concatenate([zh, sin, zeros], axis=1)
    per = LANES // B_HD
    return tuple(jnp.tile(a, (reps, per)) for a in (cos_h, slo_h, shi_h))


def _inproj_call(x, mod, norm1, w_in_bf, pos, bb, tt):
    b, t, d = x.shape
    cos, slo, shi = _rope_tables(pos, bb)
    rows = bb * tt
    if bb == 1:
        tab_spec = pl.BlockSpec((tt, LANES), lambda i, j: (j, 0))
    else:
        tab_spec = pl.BlockSpec((rows, LANES), lambda i, j: (0, 0))

    def act_spec(w):
        return pl.BlockSpec((bb, tt, w), lambda i, j: (i, j, 0))

    def out(w, dt):
        return jax.ShapeDtypeStruct((b, t, w), dt)

    return pl.pallas_call(
        functools.partial(_inproj_kernel, bb=bb, tt=tt),
        grid=(b // bb, t // tt),
        in_specs=[act_spec(d),
                  pl.BlockSpec((bb, 1, 6 * d), lambda i, j: (i, 0, 0)),
                  pl.BlockSpec((1, 1, d), lambda i, j: (0, 0, 0)),
                  pl.BlockSpec((d, IN_WIDTH_PAD), lambda i, j: (0, 0)),
                  tab_spec, tab_spec, tab_spec],
        out_specs=[act_spec(4 * A_WIDTH), act_spec(B_WIDTH), act_spec(KV_WIDTH),
                   act_spec(KV_WIDTH), act_spec(IDX_WIDTH), act_spec(IDX_DIM),
                   act_spec(IDX_HEADS)],
        out_shape=[out(4 * A_WIDTH, F32), out(B_WIDTH, BF16), out(KV_WIDTH, F32),
                   out(KV_WIDTH, F32), out(IDX_WIDTH, BF16), out(IDX_DIM, F32),
                   out(IDX_HEADS, F32)],
        compiler_params=pltpu.CompilerParams(
            dimension_semantics=("parallel", "parallel"),
            vmem_limit_bytes=VMEM_LIMIT_BYTES),
        name="inproj",
    )(x, mod.reshape(b, 1, 6 * d), norm1.reshape(1, 1, d), w_in_bf, cos, slo, shi)


def _hgrn_tables(cc):
    nlev = int(np.log2(cc))
    t = np.arange(cc)[:, None]
    u = np.arange(cc)[None, :]
    mats = []
    for l in range(nlev):
        m = cc >> (l + 1)
        ref = (t // (2 * m)) * (2 * m) + m - 1
        qside = ((t // m) % 2) == 1
        mats.append(np.where(qside, (u > ref) & (u <= t), (u > t) & (u <= ref)))
    mats.append(u <= t)
    mats.append(u > t)
    w = np.concatenate(mats, axis=0).astype(np.float32)
    w = np.concatenate([w, w], axis=1)
    lvl = np.full((cc, cc), -1, np.int32)
    for l in range(nlev):
        m = cc >> (l + 1)
        same_parent = (t // (2 * m)) == (u // (2 * m))
        lvl[same_parent & ((t // m) % 2 == 1) & ((u // m) % 2 == 0)] = l
    lvl[np.arange(cc), np.arange(cc)] = nlev
    return jnp.asarray(w, BF16), jnp.asarray(lvl), nlev


def _hgrn_kernel(q_ref, f_ref, i_ref, g_ref, lb_ref, gn_ref, s0_ref, w_ref, lvl_ref,
                 o_ref, s_out_ref, st_scr, *, cc, nlev):
    ci = pl.program_id(1)

    @pl.when(ci == 0)
    def _():
        for h in range(A_HEADS):
            st_scr[h] = s0_ref[0, h].T

    row = lax.broadcasted_iota(jnp.int32, (cc, A_DK), 0)
    qsides = [((row // (cc >> (l + 1))) % 2) == 1 for l in range(nlev)]
    lvl = lvl_ref[...]
    lvl_masks = [lvl == l for l in range(nlev + 1)]
    w = w_ref[...]
    lb = lb_ref[...]
    f_all = lb + (1.0 - lb) * jax.nn.sigmoid(f_ref[0])
    nl = -jnp.log2(f_all)
    nl_hi = nl.astype(BF16)
    nl_lo = (nl - nl_hi.astype(F32)).astype(BF16)
    dd_all = _dot(w, jnp.concatenate([nl_hi, nl_lo], axis=0))
    for h in range(A_HEADS):
        sl = slice(h * A_DK, (h + 1) * A_DK)
        q = _silu(q_ref[0, :, sl])
        kk = 1.0 - f_all[:, sl]
        v = i_ref[0, :, sl]
        dd = dd_all[:, sl]
        attn = jnp.zeros((cc, cc), F32)
        for l in range(nlev):
            e = jnp.exp2(-dd[l * cc:(l + 1) * cc])
            xl = (jnp.where(qsides[l], q, kk) * e).astype(BF16)
            attn = attn + jnp.where(lvl_masks[l], _dot_nt(xl, xl), 0.0)
        attn = attn + jnp.where(lvl_masks[nlev], _dot_nt(q.astype(BF16), kk.astype(BF16)), 0.0)
        bcum = dd[nlev * cc:(nlev + 1) * cc]
        brev = dd[(nlev + 1) * cc:(nlev + 2) * cc]
        qg = (q * jnp.exp2(-bcum)).astype(BF16)
        kg = (kk * jnp.exp2(-brev)).astype(BF16)
        st = st_scr[h]
        o = _dot(attn.astype(BF16), v.astype(BF16)) + _dot_nt(qg, st.astype(BF16))
        dec = jnp.exp2(-bcum[cc - 1:cc, :])
        st_scr[h] = st * dec + _dot(v.T.astype(BF16), kg)
        y = _rms(o) * gn_ref[:, sl]
        o_ref[0, :, sl] = (y * _silu(g_ref[0, :, sl])).astype(BF16)

    @pl.when(ci == pl.num_programs(1) - 1)
    def _():
        for h in range(A_HEADS):
            s_out_ref[0, h] = st_scr[h].T


def _hgrn_call(hg, lb, g_norm, s0, cc):
    b, t, _ = hg.shape
    w, lvl, nlev = _hgrn_tables(cc)

    def part(p):
        return pl.BlockSpec((1, cc, A_WIDTH), lambda i, j, p=p: (i, j, p))

    return pl.pallas_call(
        functools.partial(_hgrn_kernel, cc=cc, nlev=nlev),
        grid=(b, t // cc),
        in_specs=[part(0), part(1), part(2), part(3),
                  pl.BlockSpec((1, A_WIDTH), lambda i, j: (0, 0)),
                  pl.BlockSpec((1, A_WIDTH), lambda i, j: (0, 0)),
                  pl.BlockSpec((1, A_HEADS, A_DK, A_DV), lambda i, j: (i, 0, 0, 0)),
                  pl.BlockSpec(w.shape, lambda i, j: (0, 0)),
                  pl.BlockSpec(lvl.shape, lambda i, j: (0, 0))],
        out_specs=[pl.BlockSpec((1, cc, A_WIDTH), lambda i, j: (i, j, 0)),
                   pl.BlockSpec((1, A_HEADS, A_DK, A_DV), lambda i, j: (i, 0, 0, 0))],
        out_shape=[jax.ShapeDtypeStruct((b, t, A_WIDTH), BF16),
                   jax.ShapeDtypeStruct((b, A_HEADS, A_DK, A_DV), F32)],
        scratch_shapes=[pltpu.VMEM((A_HEADS, A_DV, A_DK), F32)],
        compiler_params=pltpu.CompilerParams(
            dimension_semantics=("parallel", "arbitrary"),
            vmem_limit_bytes=VMEM_LIMIT_BYTES),
        name="hgrn",
    )(hg, hg, hg, hg, lb.reshape(1, A_WIDTH), g_norm.reshape(1, A_WIDTH), s0, w, lvl)


def _fold8(x, op):
    parts = [x[r:r + SUBLANES] for r in range(0, x.shape[0], SUBLANES)]
    while len(parts) > 1:
        parts = [op(parts[i], parts[i + 1]) for i in range(0, len(parts) - 1, 2)] + (
            [parts[-1]] if len(parts) % 2 else [])
    return parts[0]


def _stack_heads(x, a, b, width):
    return jnp.concatenate([x[:, a * width:(a + 1) * width], x[:, b * width:(b + 1) * width]],
                           axis=0)


def _dsa_kernel(qi_ref, wit_ref, qb_ref, ki_ref, k_ref, vt_ref, tri_ref, o_ref,
                score_scr, logit_scr, acc_scr, *, qb, qreal, tk, topk, pos0):
    blk = pl.program_id(1)
    last_pos = pos0 + (blk + 1) * qreal - 1
    extent = jnp.minimum((last_pos // CHUNK + 1) * CHUNK, tk)
    ntile = (extent + KEY_TILE - 1) // KEY_TILE
    qpos = pos0 + blk * qreal + lax.broadcasted_iota(jnp.int32, (1, qb), 1)
    key_end = jnp.minimum((qpos // CHUNK + 1) * CHUNK, tk)
    neg_inf = jnp.float32(-jnp.inf)

    qi = qi_ref[0]
    wit = wit_ref[0]
    qi_pairs = [_stack_heads(qi, 2 * p, 2 * p + 1, IDX_DIM) for p in range(IDX_HEADS // 2)]

    def tile_start(j):
        return pl.multiple_of(j * KEY_TILE, KEY_TILE)

    def tile_loop(body, init):
        carry = lax.fori_loop(0, ntile // 2, lambda i, c: body(2 * i + 1, body(2 * i, c)), init)
        return lax.cond(ntile % 2 == 1, lambda c: body(ntile - 1, c), lambda c: c, carry)

    def score_tile(j, carry):
        k0 = tile_start(j)
        ki_t = ki_ref[0, pl.ds(k0, KEY_TILE), :]
        acc = jnp.zeros((KEY_TILE, qb), F32)
        for p in range(IDX_HEADS // 2):
            s2 = jnp.maximum(_dot_nt(ki_t, qi_pairs[p]), 0.0)
            acc = acc + wit[2 * p:2 * p + 1, :] * s2[:, :qb]
            acc = acc + wit[2 * p + 1:2 * p + 2, :] * s2[:, qb:]
        kidx = k0 + lax.broadcasted_iota(jnp.int32, (KEY_TILE, qb), 0)
        score_scr[pl.ds(k0, KEY_TILE), :] = jnp.where(kidx < key_end, acc, neg_inf)
        return carry

    tile_loop(score_tile, 0)

    def count(pred_fn):
        def body(j, acc):
            s = score_scr[pl.ds(tile_start(j), KEY_TILE), :]
            return acc + _fold8(jnp.where(pred_fn(s), 1.0, 0.0), jnp.add)
        acc = tile_loop(body, jnp.zeros((SUBLANES, qb), F32))
        return jnp.sum(acc, axis=0, keepdims=True)

    def decode(t_u):
        key = t_u ^ INT_MIN
        return pltpu.bitcast(key ^ ((key >> 31) & np.int32(0x7FFFFFFF)), F32)

    def bisect(i, t_u):
        cand_u = t_u | (jnp.int32(1) << (31 - i))
        cand = decode(cand_u)
        cnt = count(lambda s: s >= cand)
        return jnp.where(cnt >= topk, cand_u, t_u)

    thr = decode(lax.fori_loop(0, 32, bisect, jnp.zeros((1, qb), jnp.int32)))
    thr = jnp.where(thr != thr, neg_inf, thr)
    need = topk - count(lambda s: s > thr)
    need = jnp.where(thr == neg_inf, 0.0, need)

    qq = qb_ref[0]
    group = B_HEADS // B_KV_HEADS
    q_pairs = [_stack_heads(qq, n * group, n * group + 1, B_HD) for n in range(B_KV_HEADS)]
    tri = tri_ref[...]

    def logit_tile(j, carry):
        offs, mx = carry
        k0 = tile_start(j)
        s = score_scr[pl.ds(k0, KEY_TILE), :]
        tie = jnp.where(s == thr, 1.0, 0.0)
        rank = _dot(tri, tie.astype(BF16)) + offs
        picked = jnp.where(s > thr, 1.0, jnp.where(rank < need, tie, 0.0))
        bias = jnp.where(picked > 0.0, 0.0, NEG_BIG)
        bias2 = jnp.concatenate([bias, bias], axis=1)
        new_mx = []
        for n in range(B_KV_HEADS):
            lg = _dot_nt(k_ref[0, n, pl.ds(k0, KEY_TILE), :], q_pairs[n]) + bias2
            logit_scr[n, pl.ds(k0, KEY_TILE), :] = lg
            new_mx.append(jnp.maximum(mx[n], _fold8(lg, jnp.maximum)))
        offs = offs + jnp.sum(_fold8(tie, jnp.add), axis=0, keepdims=True)
        return offs, tuple(new_mx)

    mx0 = tuple(jnp.full((SUBLANES, 2 * qb), NEG_BIG, F32) for _ in range(B_KV_HEADS))
    _, mx = tile_loop(logit_tile, (jnp.zeros((1, qb), F32), mx0))
    mx = [jnp.max(m, axis=0, keepdims=True) for m in mx]

    acc_scr[...] = jnp.zeros_like(acc_scr)

    def pv_tile(j, den):
        k0 = tile_start(j)
        new_den = []
        for n in range(B_KV_HEADS):
            p = jnp.exp2(logit_scr[n, pl.ds(k0, KEY_TILE), :] - mx[n])
            new_den.append(den[n] + _fold8(p, jnp.add))
            vt = vt_ref[0, n * B_HD:(n + 1) * B_HD, pl.ds(k0, KEY_TILE)]
            acc_scr[n] += _dot(vt, p.astype(BF16))
        return tuple(new_den)

    den0 = tuple(jnp.zeros((SUBLANES, 2 * qb), F32) for _ in range(B_KV_HEADS))
    den = tile_loop(pv_tile, den0)
    rows = []
    for n in range(B_KV_HEADS):
        o2 = acc_scr[n] / jnp.sum(den[n], axis=0, keepdims=True)
        rows += [o2[:, :qb], o2[:, qb:]]
    o_ref[0] = jnp.concatenate(rows, axis=0).T.astype(BF16)


def _dsa_call(qi, wi, qbs, ki_all, k_all, v_all, tk, pos0):
    b, t, _ = qi.shape
    qb = DSA_QUERIES if t % DSA_QUERIES == 0 else LANES
    qreal = min(qb, t)
    tpad = -(-t // qb) * qb
    tkp = -(-tk // KEY_TILE) * KEY_TILE
    topk = min(TOPK_MAX, tk // 4)
    assert qreal == qb or t == qreal, "a partial query block must be the only one"
    assert topk <= KEY_TILE

    def pad_q(a):
        return jnp.pad(a, ((0, 0), (0, tpad - t), (0, 0)))

    def pad_k(a):
        return jnp.pad(a, ((0, 0), (0, tkp - tk), (0, 0))).astype(BF16)

    wit = jnp.swapaxes(pad_q(wi), 1, 2)
    k_hm = jnp.swapaxes(pad_k(k_all).reshape(b, tkp, B_KV_HEADS, B_HD), 1, 2)
    vt = jnp.swapaxes(pad_k(v_all), 1, 2)
    tri = jnp.asarray(np.tril(np.ones((KEY_TILE, KEY_TILE), np.float32), -1), BF16)

    def q_spec(w):
        return pl.BlockSpec((1, qb, w), lambda i, j: (i, j, 0))

    out = pl.pallas_call(
        functools.partial(_dsa_kernel, qb=qb, qreal=qreal, tk=tk, topk=topk, pos0=pos0),
        grid=(b, tpad // qb),
        in_specs=[q_spec(IDX_WIDTH),
                  pl.BlockSpec((1, IDX_HEADS, qb), lambda i, j: (i, 0, j)),
                  q_spec(B_WIDTH),
                  pl.BlockSpec((1, tkp, IDX_DIM), lambda i, j: (i, 0, 0)),
                  pl.BlockSpec((1, B_KV_HEADS, tkp, B_HD), lambda i, j: (i, 0, 0, 0)),
                  pl.BlockSpec((1, KV_WIDTH, tkp), lambda i, j: (i, 0, 0)),
                  pl.BlockSpec((KEY_TILE, KEY_TILE), lambda i, j: (0, 0))],
        out_specs=q_spec(B_WIDTH),
        out_shape=jax.ShapeDtypeStruct((b, tpad, B_WIDTH), BF16),
        scratch_shapes=[pltpu.VMEM((tkp, qb), F32),
                        pltpu.VMEM((B_KV_HEADS, tkp, 2 * qb), F32),
                        pltpu.VMEM((B_KV_HEADS, B_HD, 2 * qb), F32)],
        compiler_params=pltpu.CompilerParams(
            dimension_semantics=("parallel", "parallel"),
            vmem_limit_bytes=VMEM_LIMIT_BYTES),
        name="dsa",
    )(pad_q(qi), wit, pad_q(qbs), pad_k(ki_all), k_hm, vt, tri)
    return out[:, :t]


def _out_kernel(x_ref, oa_ref, ob_ref, mod_ref, n2_ref, nf_ref, wo_ref, w1_ref, w2_ref,
                y_ref, *, bb, tt):
    rows = bb * tt
    x = x_ref[...]
    mod = mod_ref[...]
    g1 = mod[:, :, 2 * D_MODEL:3 * D_MODEL]
    sh2 = mod[:, :, 3 * D_MODEL:4 * D_MODEL]
    sc2 = mod[:, :, 4 * D_MODEL:5 * D_MODEL]
    g2 = mod[:, :, 5 * D_MODEL:6 * D_MODEL]
    oa = oa_ref[...].reshape(rows, A_WIDTH)
    ob = ob_ref[...].reshape(rows, B_WIDTH)
    mix = _dot(oa, wo_ref[0:A_WIDTH, :]) + _dot(ob, wo_ref[A_WIDTH:A_WIDTH + B_WIDTH, :])
    x = x + g1 * mix.reshape(bb, tt, D_MODEL)
    h2 = (_rms(x) * n2_ref[...]) * (1.0 + sc2) + sh2
    u = _dot(h2.reshape(rows, D_MODEL).astype(BF16), w1_ref[...])
    r = jnp.square(jnp.maximum(u, 0.0)).astype(BF16)
    x = x + g2 * _dot(r, w2_ref[...]).reshape(bb, tt, D_MODEL)
    y_ref[...] = _rms(x) * nf_ref[...]


def _out_call(x, oa, ob, mod, norm2, norm_f, wo_bf, w1_bf, w2_bf, bb, tt):
    b, t, d = x.shape

    def act_spec(w):
        return pl.BlockSpec((bb, tt, w), lambda i, j: (i, j, 0))

    def const_spec(shape):
        zeros = (0,) * len(shape)
        return pl.BlockSpec(shape, lambda i, j: zeros, pipeline_mode=pl.Buffered(1))

    return pl.pallas_call(
        functools.partial(_out_kernel, bb=bb, tt=tt),
        grid=(b // bb, t // tt),
        in_specs=[act_spec(d), act_spec(A_WIDTH), act_spec(B_WIDTH),
                  pl.BlockSpec((bb, 1, 6 * d), lambda i, j: (i, 0, 0)),
                  const_spec((1, 1, d)), const_spec((1, 1, d)),
                  const_spec(wo_bf.shape), const_spec(w1_bf.shape), const_spec(w2_bf.shape)],
        out_specs=act_spec(d),
        out_shape=jax.ShapeDtypeStruct((b, t, d), F32),
        compiler_params=pltpu.CompilerParams(
            dimension_semantics=("parallel", "parallel"),
            vmem_limit_bytes=VMEM_LIMIT_BYTES),
        name="out",
    )(x, oa, ob, mod.reshape(b, 1, 6 * d), norm2.reshape(1, 1, d), norm_f.reshape(1, 1, d),
      wo_bf, w1_bf, w2_bf)


def _layer(x, mod, pos0, s0, k_past, v_past, ki_past, weights, bb, tt, cc):
    norm1, w_in_bf, lb, g_norm, wo_bf, norm2, w1_bf, w2_bf, norm_f = weights
    b, t, _ = x.shape
    pos = pos0 + jnp.arange(t)
    hg, qbs, k_new, v_new, qi, ki_new, wi = _inproj_call(x, mod, norm1, w_in_bf, pos, bb, tt)
    oa, s_new = _hgrn_call(hg, lb, g_norm, s0, cc)
    if k_past is None:
        k_all, v_all, ki_all = k_new, v_new, ki_new
    else:
        k_all = jnp.concatenate([k_past.reshape(b, -1, KV_WIDTH), k_new], axis=1)
        v_all = jnp.concatenate([v_past.reshape(b, -1, KV_WIDTH), v_new], axis=1)
        ki_all = jnp.concatenate([ki_past, ki_new], axis=1)
    ob = _dsa_call(qi, wi, qbs, ki_all, k_all, v_all, k_all.shape[1], pos0)
    y = _out_call(x, oa, ob, mod, norm2, norm_f, wo_bf, w1_bf, w2_bf, bb, tt)
    return (y, k_new.reshape(b, t, B_KV_HEADS, B_HD), v_new.reshape(b, t, B_KV_HEADS, B_HD),
            ki_new, s_new)


def kernel(x_prompt, x_sample, cache_k, cache_v, cache_k_idx, state_hgrn, c_prompt, c_sample,
           w_mod, b_mod, norm1, w_in, lb_logits, g_norm_a, w_out, norm2, w_ff1, w_ff2, norm_f):
    depth = w_in.shape[0]
    assert depth == 1, "kernel is written for the single-layer configuration"
    lb_all = jnp.cumsum(jax.nn.softmax(lb_logits.astype(F32), axis=0), axis=0)
    bp, tp, _ = x_prompt.shape
    bs, ts, _ = x_sample.shape
    past = cache_k.shape[2]
    l = 0
    mod = _mod_call(jnp.concatenate([c_prompt, c_sample], axis=0), w_mod[l], b_mod[l])
    w_in_bf = jnp.pad(w_in[l], ((0, 0), (0, IN_WIDTH_PAD - IN_WIDTH))).astype(BF16)
    weights = (norm1[l], w_in_bf, lb_all[l], g_norm_a[l], w_out[l].astype(BF16), norm2[l],
               w_ff1[l].astype(BF16), w_ff2[l].astype(BF16), norm_f)
    s0 = jnp.zeros((bp, A_HEADS, A_DK, A_DV), F32)
    yp, kp, vp, kip, sp = _layer(x_prompt, mod[:bp], 0, s0, None, None, None, weights,
                                 bb=1, tt=min(512, tp), cc=min(128, tp))
    ys, ks, vs, kis, ss = _layer(x_sample, mod[bp:], past, state_hgrn[l], cache_k[l],
                                 cache_v[l], cache_k_idx[l], weights,
                                 bb=bs, tt=ts, cc=min(128, ts))
    return (yp, ys, kp[None], vp[None], kip[None], sp[None],
            ks[None], vs[None], kis[None], ss[None])
```

```python
import functools

import numpy as np
import jax
import jax.numpy as jnp
from jax import lax
from jax.experimental import pallas as pl
from jax.experimental.pallas import tpu as pltpu

D_MODEL = 1024
CHUNK = 64
A_HEADS = 4
A_DK = 128
A_DV = 128
A_WIDTH = A_HEADS * A_DV
B_HEADS = 8
B_KV_HEADS = 4
B_HD = 64
B_WIDTH = B_HEADS * B_HD
KV_WIDTH = B_KV_HEADS * B_HD
IDX_HEADS = 8
IDX_DIM = 64
IDX_WIDTH = IDX_HEADS * IDX_DIM
TOPK_MAX = 256
QBLOCK = 128
ROT_FRAC = 4
ROPE_THETA = 500000.0
D_FF = 4 * D_MODEL
EPS = 1e-6
IN_WIDTH = 4 * A_WIDTH + B_WIDTH + 2 * KV_WIDTH + IDX_WIDTH + IDX_DIM + IDX_HEADS

LANES = 128
SUBLANES = 8
KEY_TILE = 256
DSA_QUERIES = 256
HGRN_CHUNKS_PER_STEP = 4
IN_WIDTH_PAD = -(-IN_WIDTH // LANES) * LANES
VMEM_LIMIT_BYTES = 56 * 1024 * 1024

F32 = jnp.float32
BF16 = jnp.bfloat16
INT_MIN = np.int32(-2 ** 31)
NEG_BIG = -1e30
LOG2_E = 1.4426950408889634

OFF_HG = 0
OFF_QB = 4 * A_WIDTH
OFF_KB = OFF_QB + B_WIDTH
OFF_VB = OFF_KB + KV_WIDTH
OFF_QI = OFF_VB + KV_WIDTH
OFF_KI = OFF_QI + IDX_WIDTH
OFF_WI = OFF_KI + IDX_DIM


def _dot(a, b):
    return jnp.dot(a, b, preferred_element_type=F32)


def _dot_nt(a, b):
    return lax.dot_general(a, b, (((1,), (1,)), ((), ())), preferred_element_type=F32)


def _silu(x):
    return x * jax.nn.sigmoid(x)


def _rms(x):
    return x * lax.rsqrt(jnp.mean(jnp.square(x), axis=-1, keepdims=True) + EPS)


def _mod_kernel(c_ref, w_ref, b_ref, o_ref):
    a = _silu(c_ref[...])
    o_ref[...] = jnp.dot(a, w_ref[...], preferred_element_type=F32,
                         precision=lax.Precision.HIGHEST) + b_ref[...]


def _mod_call(c, w_mod, b_mod):
    rows, d = c.shape
    n = w_mod.shape[1]
    tn = 1024
    return pl.pallas_call(
        _mod_kernel,
        grid=(n // tn,),
        in_specs=[pl.BlockSpec((rows, d), lambda j: (0, 0)),
                  pl.BlockSpec((d, tn), lambda j: (0, j)),
                  pl.BlockSpec((1, tn), lambda j: (0, j))],
        out_specs=pl.BlockSpec((rows, tn), lambda j: (0, j)),
        out_shape=jax.ShapeDtypeStruct((rows, n), F32),
        compiler_params=pltpu.CompilerParams(vmem_limit_bytes=VMEM_LIMIT_BYTES),
        name="mod",
    )(c, w_mod, b_mod.reshape(1, n))


def _rope(x, cos, sin_lo, sin_hi):
    half = B_HD // ROT_FRAC // 2
    return (x * cos + pltpu.roll(x, half, 1) * sin_hi
            + pltpu.roll(x, LANES - half, 1) * sin_lo)


def _inproj_kernel(x_ref, mod_ref, n1_ref, w_ref, cos_ref, slo_ref, shi_ref,
                   hg_ref, qb_ref, k_ref, v_ref, qi_ref, ki_ref, wi_ref, *, bb, tt):
    rows = bb * tt
    x = x_ref[...]
    mod = mod_ref[...]
    sh1 = mod[:, :, 0:D_MODEL]
    sc1 = mod[:, :, D_MODEL:2 * D_MODEL]
    h = (_rms(x) * n1_ref[...]) * (1.0 + sc1) + sh1
    h = h.reshape(rows, D_MODEL).astype(BF16)
    z = _dot(h, w_ref[...])
    cos, slo, shi = cos_ref[...], slo_ref[...], shi_ref[...]

    def rope_cols(off, width):
        return [_rope(z[:, off + j:off + j + LANES], cos, slo, shi)
                for j in range(0, width, LANES)]

    hg_ref[...] = z[:, OFF_HG:OFF_QB].reshape(bb, tt, 4 * A_WIDTH)
    scale = B_HD ** -0.5 * LOG2_E
    qb = jnp.concatenate(rope_cols(OFF_QB, B_WIDTH), axis=1) * scale
    qb_ref[...] = qb.astype(BF16).reshape(bb, tt, B_WIDTH)
    kb = jnp.concatenate(rope_cols(OFF_KB, KV_WIDTH), axis=1)
    k_ref[...] = kb.reshape(bb, tt, KV_WIDTH)
    v_ref[...] = z[:, OFF_VB:OFF_QI].reshape(bb, tt, KV_WIDTH)
    qi = jnp.concatenate(rope_cols(OFF_QI, IDX_WIDTH), axis=1)
    qi_ref[...] = qi.astype(BF16).reshape(bb, tt, IDX_WIDTH)
    last = _rope(z[:, OFF_KI:OFF_KI + LANES], cos, slo, shi)
    ki_ref[...] = last[:, 0:IDX_DIM].reshape(bb, tt, IDX_DIM)
    wi = z[:, OFF_WI:OFF_WI + IDX_HEADS] * (IDX_WIDTH ** -0.5)
    wi_ref[...] = wi.reshape(bb, tt, IDX_HEADS)


def _rope_tables(pos, reps):
    rot = B_HD // ROT_FRAC
    half = rot // 2
    inv = jnp.power(ROPE_THETA, -jnp.arange(half, dtype=F32) * (2.0 / rot))
    ang = pos.astype(F32)[:, None] * inv[None, :]
    cos, sin = jnp.cos(ang), jnp.sin(ang)
    t = pos.shape[0]
    ones = jnp.ones((t, B_HD - rot), F32)
    zeros = jnp.zeros((t, B_HD - rot), F32)
    zh = jnp.zeros((t, half), F32)
    cos_h = jnp.concatenate([cos, cos, ones], axis=1)
    slo_h = jnp.concatenate([-sin, zh, zeros], axis=1)
    shi_h = jnp.concatenate([zh, sin, zeros], axis=1)
    per = LANES // B_HD
    return tuple(jnp.tile(a, (reps, per)) for a in (cos_h, slo_h, shi_h))


def _inproj_call(x, mod, norm1, w_in_bf, pos, bb, tt):
    b, t, d = x.shape
    cos, slo, shi = _rope_tables(pos, bb)
    rows = bb * tt
    if bb == 1:
        tab_spec = pl.BlockSpec((tt, LANES), lambda i, j: (j, 0))
    else:
        tab_spec = pl.BlockSpec((rows, LANES), lambda i, j: (0, 0))

    def act_spec(w):
        return pl.BlockSpec((bb, tt, w), lambda i, j: (i, j, 0))

    def out(w, dt):
        return jax.ShapeDtypeStruct((b, t, w), dt)

    return pl.pallas_call(
        functools.partial(_inproj_kernel, bb=bb, tt=tt),
        grid=(b // bb, t // tt),
        in_specs=[act_spec(d),
                  pl.BlockSpec((bb, 1, 6 * d), lambda i, j: (i, 0, 0)),
                  pl.BlockSpec((1, 1, d), lambda i, j: (0, 0, 0)),
                  pl.BlockSpec((d, IN_WIDTH_PAD), lambda i, j: (0, 0)),
                  tab_spec, tab_spec, tab_spec],
        out_specs=[act_spec(4 * A_WIDTH), act_spec(B_WIDTH), act_spec(KV_WIDTH),
                   act_spec(KV_WIDTH), act_spec(IDX_WIDTH), act_spec(IDX_DIM),
                   act_spec(IDX_HEADS)],
        out_shape=[out(4 * A_WIDTH, F32), out(B_WIDTH, BF16), out(KV_WIDTH, F32),
                   out(KV_WIDTH, F32), out(IDX_WIDTH, BF16), out(IDX_DIM, F32),
                   out(IDX_HEADS, F32)],
        compiler_params=pltpu.CompilerParams(
            dimension_semantics=("parallel", "parallel"),
            vmem_limit_bytes=VMEM_LIMIT_BYTES),
        name="inproj",
    )(x, mod.reshape(b, 1, 6 * d), norm1.reshape(1, 1, d), w_in_bf, cos, slo, shi)


def _hgrn_tables(cc):
    nlev = int(np.log2(cc))
    t = np.arange(cc)[:, None]
    u = np.arange(cc)[None, :]
    mats = []
    for l in range(nlev):
        m = cc >> (l + 1)
        ref = (t // (2 * m)) * (2 * m) + m - 1
        qside = ((t // m) % 2) == 1
        mats.append(np.where(qside, (u > ref) & (u <= t), (u > t) & (u <= ref)))
    mats.append(u <= t)
    mats.append(u > t)
    w = np.concatenate(mats, axis=0).astype(np.float32)
    w = -np.concatenate([w, w], axis=1)
    lvl = np.full((cc, cc), -1, np.int32)
    for l in range(nlev):
        m = cc >> (l + 1)
        same_parent = (t // (2 * m)) == (u // (2 * m))
        lvl[same_parent & ((t // m) % 2 == 1) & ((u // m) % 2 == 0)] = l
    lvl[np.arange(cc), np.arange(cc)] = nlev
    return jnp.asarray(w, BF16), jnp.asarray(lvl), nlev


def _hgrn_kernel(q_ref, f_ref, i_ref, g_ref, lb_ref, gn_ref, s0_ref, w_ref, lvl_ref,
                 o_ref, s_out_ref, st_scr, *, cc, nc, nlev):
    ci = pl.program_id(1)

    @pl.when(ci == 0)
    def _():
        for h in range(A_HEADS):
            st_scr[h] = s0_ref[0, h].T

    row = lax.broadcasted_iota(jnp.int32, (cc, A_DK), 0)
    qsides = [((row // (cc >> (l + 1))) % 2) == 1 for l in range(nlev)]
    lvl = lvl_ref[...]
    lvl_masks = [lvl == l for l in range(nlev + 1)]
    w = w_ref[...]
    lb = lb_ref[...]
    f_all = lb + (1.0 - lb) * jax.nn.sigmoid(f_ref[0])
    nl = -jnp.log2(f_all)
    nl_hi = nl.astype(BF16)
    nl_lo = (nl - nl_hi.astype(F32)).astype(BF16)
    for c in range(nc):
        rs = slice(c * cc, (c + 1) * cc)
        dd_all = _dot(w, jnp.concatenate([nl_hi[rs], nl_lo[rs]], axis=0))
        for h in range(A_HEADS):
            sl = slice(h * A_DK, (h + 1) * A_DK)
            q = _silu(q_ref[0, rs, sl])
            kk = 1.0 - f_all[rs, sl]
            v = i_ref[0, rs, sl]
            dd = dd_all[:, sl]
            attn = jnp.zeros((cc, cc), F32)
            for l in range(nlev):
                e = jnp.exp2(dd[l * cc:(l + 1) * cc])
                xl = (jnp.where(qsides[l], q, kk) * e).astype(BF16)
                attn = attn + jnp.where(lvl_masks[l], _dot_nt(xl, xl), 0.0)
            attn = attn + jnp.where(lvl_masks[nlev], _dot_nt(q.astype(BF16), kk.astype(BF16)), 0.0)
            bcum = dd[nlev * cc:(nlev + 1) * cc]
            brev = dd[(nlev + 1) * cc:(nlev + 2) * cc]
            qg = (q * jnp.exp2(bcum)).astype(BF16)
            kg = (kk * jnp.exp2(brev)).astype(BF16)
            st = st_scr[h]
            o = _dot(attn.astype(BF16), v.astype(BF16)) + _dot_nt(qg, st.astype(BF16))
            dec = jnp.exp2(bcum[cc - 1:cc, :])
            st_scr[h] = st * dec + _dot(v.T.astype(BF16), kg)
            y = _rms(o) * gn_ref[:, sl]
            o_ref[0, rs, sl] = (y * _silu(g_ref[0, rs, sl])).astype(BF16)

    @pl.when(ci == pl.num_programs(1) - 1)
    def _():
        for h in range(A_HEADS):
            s_out_ref[0, h] = st_scr[h].T


def _hgrn_call(hg, lb, g_norm, s0, cc):
    b, t, _ = hg.shape
    w, lvl, nlev = _hgrn_tables(cc)
    nc = HGRN_CHUNKS_PER_STEP if t % (HGRN_CHUNKS_PER_STEP * cc) == 0 else 1
    rows = nc * cc

    def part(p):
        return pl.BlockSpec((1, rows, A_WIDTH), lambda i, j, p=p: (i, j, p))

    return pl.pallas_call(
        functools.partial(_hgrn_kernel, cc=cc, nc=nc, nlev=nlev),
        grid=(b, t // rows),
        in_specs=[part(0), part(1), part(2), part(3),
                  pl.BlockSpec((1, A_WIDTH), lambda i, j: (0, 0)),
                  pl.BlockSpec((1, A_WIDTH), lambda i, j: (0, 0)),
                  pl.BlockSpec((1, A_HEADS, A_DK, A_DV), lambda i, j: (i, 0, 0, 0)),
                  pl.BlockSpec(w.shape, lambda i, j: (0, 0)),
                  pl.BlockSpec(lvl.shape, lambda i, j: (0, 0))],
        out_specs=[pl.BlockSpec((1, rows, A_WIDTH), lambda i, j: (i, j, 0)),
                   pl.BlockSpec((1, A_HEADS, A_DK, A_DV), lambda i, j: (i, 0, 0, 0))],
        out_shape=[jax.ShapeDtypeStruct((b, t, A_WIDTH), BF16),
                   jax.ShapeDtypeStruct((b, A_HEADS, A_DK, A_DV), F32)],
        scratch_shapes=[pltpu.VMEM((A_HEADS, A_DV, A_DK), F32)],
        compiler_params=pltpu.CompilerParams(
            dimension_semantics=("parallel", "arbitrary"),
            vmem_limit_bytes=VMEM_LIMIT_BYTES),
        name="hgrn",
    )(hg, hg, hg, hg, lb.reshape(1, A_WIDTH), g_norm.reshape(1, A_WIDTH), s0, w, lvl)


def _fold8(x, op):
    parts = [x[r:r + SUBLANES] for r in range(0, x.shape[0], SUBLANES)]
    while len(parts) > 1:
        parts = [op(parts[i], parts[i + 1]) for i in range(0, len(parts) - 1, 2)] + (
            [parts[-1]] if len(parts) % 2 else [])
    return parts[0]


def _stack_heads(x, a, b, width):
    return jnp.concatenate([x[:, a * width:(a + 1) * width], x[:, b * width:(b + 1) * width]],
                           axis=0)


def _dsa_kernel(qi_ref, wit_ref, qb_ref, ki_ref, k_ref, vt_ref, tri_ref, o_ref,
                score_scr, logit_scr, acc_scr, *, qb, qreal, tk, topk, pos0):
    blk = pl.program_id(1)
    last_pos = pos0 + (blk + 1) * qreal - 1
    extent = jnp.minimum((last_pos // CHUNK + 1) * CHUNK, tk)
    ntile = (extent + KEY_TILE - 1) // KEY_TILE
    qpos = pos0 + blk * qreal + lax.broadcasted_iota(jnp.int32, (1, qb), 1)
    key_end = jnp.minimum((qpos // CHUNK + 1) * CHUNK, tk)
    neg_inf = jnp.float32(-jnp.inf)

    qi = qi_ref[0]
    wit = wit_ref[0]
    qi_pairs = [_stack_heads(qi, 2 * p, 2 * p + 1, IDX_DIM) for p in range(IDX_HEADS // 2)]

    def tile_start(j):
        return pl.multiple_of(j * KEY_TILE, KEY_TILE)

    def tile_loop(body, init):
        carry = lax.fori_loop(0, ntile // 2, lambda i, c: body(2 * i + 1, body(2 * i, c)), init)
        return lax.cond(ntile % 2 == 1, lambda c: body(ntile - 1, c), lambda c: c, carry)

    def score_tile(j, carry):
        k0 = tile_start(j)
        ki_t = ki_ref[0, pl.ds(k0, KEY_TILE), :]
        acc = jnp.zeros((KEY_TILE, qb), F32)
        for p in range(IDX_HEADS // 2):
            s2 = jnp.maximum(_dot_nt(ki_t, qi_pairs[p]), 0.0)
            acc = acc + wit[2 * p:2 * p + 1, :] * s2[:, :qb]
            acc = acc + wit[2 * p + 1:2 * p + 2, :] * s2[:, qb:]
        kidx = k0 + lax.broadcasted_iota(jnp.int32, (KEY_TILE, qb), 0)
        score_scr[pl.ds(k0, KEY_TILE), :] = jnp.where(kidx < key_end, acc, neg_inf)
        return carry

    tile_loop(score_tile, 0)

    def count(pred_fn):
        def body(j, acc):
            s = score_scr[pl.ds(tile_start(j), KEY_TILE), :]
            return acc + _fold8(jnp.where(pred_fn(s), 1.0, 0.0), jnp.add)
        acc = tile_loop(body, jnp.zeros((SUBLANES, qb), F32))
        return jnp.sum(acc, axis=0, keepdims=True)

    def decode(t_u):
        key = t_u ^ INT_MIN
        return pltpu.bitcast(key ^ ((key >> 31) & np.int32(0x7FFFFFFF)), F32)

    def bisect(i, t_u):
        cand_u = t_u | (jnp.int32(1) << (31 - i))
        cand = decode(cand_u)
        cnt = count(lambda s: s >= cand)
        return jnp.where(cnt >= topk, cand_u, t_u)

    thr = decode(lax.fori_loop(0, 32, bisect, jnp.zeros((1, qb), jnp.int32)))
    thr = jnp.where(thr != thr, neg_inf, thr)
    need = topk - count(lambda s: s > thr)
    need = jnp.where(thr == neg_inf, 0.0, need)

    qq = qb_ref[0]
    group = B_HEADS // B_KV_HEADS
    q_pairs = [_stack_heads(qq, n * group, n * group + 1, B_HD) for n in range(B_KV_HEADS)]
    tri = tri_ref[...]

    def logit_tile(j, carry):
        offs, mx = carry
        k0 = tile_start(j)
        s = score_scr[pl.ds(k0, KEY_TILE), :]
        tie = jnp.where(s == thr, 1.0, 0.0)
        rank = _dot(tri, tie.astype(BF16)) + offs
        picked = jnp.where(s > thr, 1.0, jnp.where(rank < need, tie, 0.0))
        bias = jnp.where(picked > 0.0, 0.0, NEG_BIG)
        bias2 = jnp.concatenate([bias, bias], axis=1)
        new_mx = []
        for n in range(B_KV_HEADS):
            lg = _dot_nt(k_ref[0, n, pl.ds(k0, KEY_TILE), :], q_pairs[n]) + bias2
            logit_scr[n, pl.ds(k0, KEY_TILE), :] = lg
            new_mx.append(jnp.maximum(mx[n], _fold8(lg, jnp.maximum)))
        offs = offs + jnp.sum(_fold8(tie, jnp.add), axis=0, keepdims=True)
        return offs, tuple(new_mx)

    mx0 = tuple(jnp.full((SUBLANES, 2 * qb), NEG_BIG, F32) for _ in range(B_KV_HEADS))
    _, mx = tile_loop(logit_tile, (jnp.zeros((1, qb), F32), mx0))
    mx = [jnp.max(m, axis=0, keepdims=True) for m in mx]

    acc_scr[...] = jnp.zeros_like(acc_scr)

    def pv_tile(j, den):
        k0 = tile_start(j)
        new_den = []
        for n in range(B_KV_HEADS):
            p = jnp.exp2(logit_scr[n, pl.ds(k0, KEY_TILE), :] - mx[n])
            new_den.append(den[n] + _fold8(p, jnp.add))
            vt = vt_ref[0, n * B_HD:(n + 1) * B_HD, pl.ds(k0, KEY_TILE)]
            acc_scr[n] += _dot(vt, p.astype(BF16))
        return tuple(new_den)

    den0 = tuple(jnp.zeros((SUBLANES, 2 * qb), F32) for _ in range(B_KV_HEADS))
    den = tile_loop(pv_tile, den0)
    rows = []
    for n in range(B_KV_HEADS):
        o2 = acc_scr[n] / jnp.sum(den[n], axis=0, keepdims=True)
        rows += [o2[:, :qb], o2[:, qb:]]
    o_ref[0] = jnp.concatenate(rows, axis=0).T.astype(BF16)


def _kvprep_kernel(*refs, past_tiles, new_rows):
    n_in = 6 if past_tiles else 3
    khm_ref, vt_ref, kib_ref = refs[n_in:]

    def emit(k, v, ki):
        kb = k.astype(BF16)
        for n in range(B_KV_HEADS):
            khm_ref[0, n] = kb[:, n * B_HD:(n + 1) * B_HD]
        vt_ref[0] = v.T.astype(BF16)
        kib_ref[0] = ki.astype(BF16)

    def padded(ref):
        a = ref[0]
        if new_rows == KEY_TILE:
            return a
        return jnp.concatenate([a, jnp.zeros((KEY_TILE - new_rows, a.shape[1]), a.dtype)], axis=0)

    new_refs = refs[n_in - 3:n_in]
    if past_tiles:
        j = pl.program_id(1)

        @pl.when(j < past_tiles)
        def _():
            emit(refs[0][0], refs[1][0], refs[2][0])

        @pl.when(j >= past_tiles)
        def _():
            emit(*(padded(r) for r in new_refs))
    else:
        emit(*(padded(r) for r in new_refs))


def _kvprep_call(past, new):
    b, t, _ = new[0].shape
    p = 0 if past is None else past[0].shape[1]
    assert p % KEY_TILE == 0 and (t % KEY_TILE == 0 or t < KEY_TILE)
    past_tiles = p // KEY_TILE
    new_rows = min(t, KEY_TILE)
    new_tiles = -(-t // KEY_TILE)
    tkp = (past_tiles + new_tiles) * KEY_TILE

    def past_spec(w):
        return pl.BlockSpec((1, KEY_TILE, w), lambda i, j: (i, jnp.minimum(j, past_tiles - 1), 0))

    def new_spec(w):
        return pl.BlockSpec((1, new_rows, w), lambda i, j: (i, jnp.maximum(j - past_tiles, 0), 0))

    widths = (KV_WIDTH, KV_WIDTH, IDX_DIM)
    in_specs = [new_spec(w) for w in widths]
    args = list(new)
    if past_tiles:
        in_specs = [past_spec(w) for w in widths] + in_specs
        args = list(past) + args
    return pl.pallas_call(
        functools.partial(_kvprep_kernel, past_tiles=past_tiles, new_rows=new_rows),
        grid=(b, past_tiles + new_tiles),
        in_specs=in_specs,
        out_specs=[pl.BlockSpec((1, B_KV_HEADS, KEY_TILE, B_HD), lambda i, j: (i, 0, j, 0)),
                   pl.BlockSpec((1, KV_WIDTH, KEY_TILE), lambda i, j: (i, 0, j)),
                   pl.BlockSpec((1, KEY_TILE, IDX_DIM), lambda i, j: (i, j, 0))],
        out_shape=[jax.ShapeDtypeStruct((b, B_KV_HEADS, tkp, B_HD), BF16),
                   jax.ShapeDtypeStruct((b, KV_WIDTH, tkp), BF16),
                   jax.ShapeDtypeStruct((b, tkp, IDX_DIM), BF16)],
        compiler_params=pltpu.CompilerParams(
            dimension_semantics=("parallel", "parallel"),
            vmem_limit_bytes=VMEM_LIMIT_BYTES),
        name="kvprep",
    )(*args)


def _dsa_call(qi, wi, qbs, k_hm, vt, ki_bf, tk, pos0):
    b, t, _ = qi.shape
    qb = DSA_QUERIES if t % DSA_QUERIES == 0 else LANES
    qreal = min(qb, t)
    tpad = -(-t // qb) * qb
    tkp = ki_bf.shape[1]
    topk = min(TOPK_MAX, tk // 4)
    assert qreal == qb or t == qreal, "a partial query block must be the only one"
    assert topk <= KEY_TILE

    def pad_q(a):
        return jnp.pad(a, ((0, 0), (0, tpad - t), (0, 0)))

    wit = jnp.swapaxes(pad_q(wi), 1, 2)
    tri = jnp.asarray(np.tril(np.ones((KEY_TILE, KEY_TILE), np.float32), -1), BF16)

    def q_spec(w):
        return pl.BlockSpec((1, qb, w), lambda i, j: (i, j, 0))

    out = pl.pallas_call(
        functools.partial(_dsa_kernel, qb=qb, qreal=qreal, tk=tk, topk=topk, pos0=pos0),
        grid=(b, tpad // qb),
        in_specs=[q_spec(IDX_WIDTH),
                  pl.BlockSpec((1, IDX_HEADS, qb), lambda i, j: (i, 0, j)),
                  q_spec(B_WIDTH),
                  pl.BlockSpec((1, tkp, IDX_DIM), lambda i, j: (i, 0, 0)),
                  pl.BlockSpec((1, B_KV_HEADS, tkp, B_HD), lambda i, j: (i, 0, 0, 0)),
                  pl.BlockSpec((1, KV_WIDTH, tkp), lambda i, j: (i, 0, 0)),
                  pl.BlockSpec((KEY_TILE, KEY_TILE), lambda i, j: (0, 0))],
        out_specs=q_spec(B_WIDTH),
        out_shape=jax.ShapeDtypeStruct((b, tpad, B_WIDTH), BF16),
        scratch_shapes=[pltpu.VMEM((tkp, qb), F32),
                        pltpu.VMEM((B_KV_HEADS, tkp, 2 * qb), F32),
                        pltpu.VMEM((B_KV_HEADS, B_HD, 2 * qb), F32)],
        compiler_params=pltpu.CompilerParams(
            dimension_semantics=("parallel", "parallel"),
            vmem_limit_bytes=VMEM_LIMIT_BYTES),
        name="dsa",
    )(pad_q(qi), wit, pad_q(qbs), ki_bf, k_hm, vt, tri)
    return out[:, :t]


def _out_kernel(x_ref, oa_ref, ob_ref, mod_ref, n2_ref, nf_ref, wo_ref, w1_ref, w2_ref,
                y_ref, *, bb, tt):
    rows = bb * tt
    x = x_ref[...]
    mod = mod_ref[...]
    g1 = mod[:, :, 2 * D_MODEL:3 * D_MODEL]
    sh2 = mod[:, :, 3 * D_MODEL:4 * D_MODEL]
    sc2 = mod[:, :, 4 * D_MODEL:5 * D_MODEL]
    g2 = mod[:, :, 5 * D_MODEL:6 * D_MODEL]
    oa = oa_ref[...].reshape(rows, A_WIDTH)
    ob = ob_ref[...].reshape(rows, B_WIDTH)
    mix = _dot(oa, wo_ref[0:A_WIDTH, :]) + _dot(ob, wo_ref[A_WIDTH:A_WIDTH + B_WIDTH, :])
    x = x + g1 * mix.reshape(bb, tt, D_MODEL)
    h2 = (_rms(x) * n2_ref[...]) * (1.0 + sc2) + sh2
    u = _dot(h2.reshape(rows, D_MODEL).astype(BF16), w1_ref[...])
    r = jnp.square(jnp.maximum(u, 0.0)).astype(BF16)
    x = x + g2 * _dot(r, w2_ref[...]).reshape(bb, tt, D_MODEL)
    y_ref[...] = _rms(x) * nf_ref[...]


def _out_call(x, oa, ob, mod, norm2, norm_f, wo_bf, w1_bf, w2_bf, bb, tt):
    b, t, d = x.shape

    def act_spec(w):
        return pl.BlockSpec((bb, tt, w), lambda i, j: (i, j, 0))

    def const_spec(shape):
        zeros = (0,) * len(shape)
        return pl.BlockSpec(shape, lambda i, j: zeros, pipeline_mode=pl.Buffered(1))

    return pl.pallas_call(
        functools.partial(_out_kernel, bb=bb, tt=tt),
        grid=(b // bb, t // tt),
        in_specs=[act_spec(d), act_spec(A_WIDTH), act_spec(B_WIDTH),
                  pl.BlockSpec((bb, 1, 6 * d), lambda i, j: (i, 0, 0)),
                  const_spec((1, 1, d)), const_spec((1, 1, d)),
                  const_spec(wo_bf.shape), const_spec(w1_bf.shape), const_spec(w2_bf.shape)],
        out_specs=act_spec(d),
        out_shape=jax.ShapeDtypeStruct((b, t, d), F32),
        compiler_params=pltpu.CompilerParams(
            dimension_semantics=("parallel", "parallel"),
            vmem_limit_bytes=VMEM_LIMIT_BYTES),
        name="out",
    )(x, oa, ob, mod.reshape(b, 1, 6 * d), norm2.reshape(1, 1, d), norm_f.reshape(1, 1, d),
      wo_bf, w1_bf, w2_bf)


def _layer(x, mod, pos0, s0, k_past, v_past, ki_past, weights, bb, tt, cc):
    norm1, w_in_bf, lb, g_norm, wo_bf, norm2, w1_bf, w2_bf, norm_f = weights
    b, t, _ = x.shape
    pos = pos0 + jnp.arange(t)
    hg, qbs, k_new, v_new, qi, ki_new, wi = _inproj_call(x, mod, norm1, w_in_bf, pos, bb, tt)
    oa, s_new = _hgrn_call(hg, lb, g_norm, s0, cc)
    past, n_past = None, 0
    if k_past is not None:
        n_past = k_past.shape[1]
        past = (k_past.reshape(b, n_past, KV_WIDTH), v_past.reshape(b, n_past, KV_WIDTH), ki_past)
    k_hm, vt, ki_bf = _kvprep_call(past, (k_new, v_new, ki_new))
    ob = _dsa_call(qi, wi, qbs, k_hm, vt, ki_bf, n_past + t, pos0)
    y = _out_call(x, oa, ob, mod, norm2, norm_f, wo_bf, w1_bf, w2_bf, bb, tt)
    return (y, k_new.reshape(b, t, B_KV_HEADS, B_HD), v_new.reshape(b, t, B_KV_HEADS, B_HD),
            ki_new, s_new)


def kernel(x_prompt, x_sample, cache_k, cache_v, cache_k_idx, state_hgrn, c_prompt, c_sample,
           w_mod, b_mod, norm1, w_in, lb_logits, g_norm_a, w_out, norm2, w_ff1, w_ff2, norm_f):
    depth = w_in.shape[0]
    assert depth == 1, "kernel is written for the single-layer configuration"
    lb_all = jnp.cumsum(jax.nn.softmax(lb_logits.astype(F32), axis=0), axis=0)
    bp, tp, _ = x_prompt.shape
    bs, ts, _ = x_sample.shape
    past = cache_k.shape[2]
    l = 0
    mod = _mod_call(jnp.concatenate([c_prompt, c_sample], axis=0), w_mod[l], b_mod[l])
    w_in_bf = jnp.pad(w_in[l], ((0, 0), (0, IN_WIDTH_PAD - IN_WIDTH))).astype(BF16)
    weights = (norm1[l], w_in_bf, lb_all[l], g_norm_a[l], w_out[l].astype(BF16), norm2[l],
               w_ff1[l].astype(BF16), w_ff2[l].astype(BF16), norm_f)
    s0 = jnp.zeros((bp, A_HEADS, A_DK, A_DV), F32)
    yp, kp, vp, kip, sp = _layer(x_prompt, mod[:bp], 0, s0, None, None, None, weights,
                                 bb=1, tt=min(512, tp), cc=min(128, tp))
    ys, ks, vs, kis, ss = _layer(x_sample, mod[bp:], past, state_hgrn[l], cache_k[l],
                                 cache_v[l], cache_k_idx[l], weights,
                                 bb=bs, tt=ts, cc=min(128, ts))
    return (yp, ys, kp[None], vp[None], kip[None], sp[None],
            ks[None], vs[None], kis[None], ss[None])
```

```python
import functools

import numpy as np
import jax
import jax.numpy as jnp
from jax import lax
from jax.experimental import pallas as pl
from jax.experimental.pallas import tpu as pltpu

D_MODEL = 1024
CHUNK = 64
A_HEADS = 4
A_DK = 128
A_DV = 128
A_WIDTH = A_HEADS * A_DV
B_HEADS = 8
B_KV_HEADS = 4
B_HD = 64
B_WIDTH = B_HEADS * B_HD
KV_WIDTH = B_KV_HEADS * B_HD
IDX_HEADS = 8
IDX_DIM = 64
IDX_WIDTH = IDX_HEADS * IDX_DIM
TOPK_MAX = 256
QBLOCK = 128
ROT_FRAC = 4
ROPE_THETA = 500000.0
D_FF = 4 * D_MODEL
EPS = 1e-6
IN_WIDTH = 4 * A_WIDTH + B_WIDTH + 2 * KV_WIDTH + IDX_WIDTH + IDX_DIM + IDX_HEADS

LANES = 128
SUBLANES = 8
KEY_TILE = 256
DSA_QUERIES = 256
HGRN_CHUNKS_PER_STEP = 4
IN_WIDTH_PAD = -(-IN_WIDTH // LANES) * LANES
VMEM_LIMIT_BYTES = 56 * 1024 * 1024

F32 = jnp.float32
BF16 = jnp.bfloat16
INT_MIN = np.int32(-2 ** 31)
NEG_BIG = -1e30
LOG2_E = 1.4426950408889634

OFF_HG = 0
OFF_QB = 4 * A_WIDTH
OFF_KB = OFF_QB + B_WIDTH
OFF_VB = OFF_KB + KV_WIDTH
OFF_QI = OFF_VB + KV_WIDTH
OFF_KI = OFF_QI + IDX_WIDTH
OFF_WI = OFF_KI + IDX_DIM


def _dot(a, b):
    return jnp.dot(a, b, preferred_element_type=F32)


def _dot_nt(a, b):
    return lax.dot_general(a, b, (((1,), (1,)), ((), ())), preferred_element_type=F32)


def _silu(x):
    return x * jax.nn.sigmoid(x)


def _rms(x):
    return x * lax.rsqrt(jnp.mean(jnp.square(x), axis=-1, keepdims=True) + EPS)


def _mod_kernel(c_ref, w_ref, b_ref, o_ref):
    a = _silu(c_ref[...])
    o_ref[...] = jnp.dot(a, w_ref[...], preferred_element_type=F32,
                         precision=lax.Precision.HIGHEST) + b_ref[...]


def _mod_call(c, w_mod, b_mod):
    rows, d = c.shape
    n = w_mod.shape[1]
    tn = 1024
    return pl.pallas_call(
        _mod_kernel,
        grid=(n // tn,),
        in_specs=[pl.BlockSpec((rows, d), lambda j: (0, 0)),
                  pl.BlockSpec((d, tn), lambda j: (0, j)),
                  pl.BlockSpec((1, tn), lambda j: (0, j))],
        out_specs=pl.BlockSpec((rows, tn), lambda j: (0, j)),
        out_shape=jax.ShapeDtypeStruct((rows, n), F32),
        compiler_params=pltpu.CompilerParams(vmem_limit_bytes=VMEM_LIMIT_BYTES),
        name="mod",
    )(c, w_mod, b_mod.reshape(1, n))


def _rope(x, cos, sin_lo, sin_hi):
    half = B_HD // ROT_FRAC // 2
    return (x * cos + pltpu.roll(x, half, 1) * sin_hi
            + pltpu.roll(x, LANES - half, 1) * sin_lo)


def _inproj_kernel(x_ref, mod_ref, n1_ref, w_ref, cos_ref, slo_ref, shi_ref,
                   hg_ref, qb_ref, k_ref, v_ref, qi_ref, ki_ref, wi_ref, *dsa_refs, bb, tt):
    rows = bb * tt
    x = x_ref[...]
    mod = mod_ref[...]
    sh1 = mod[:, :, 0:D_MODEL]
    sc1 = mod[:, :, D_MODEL:2 * D_MODEL]
    h = (_rms(x) * n1_ref[...]) * (1.0 + sc1) + sh1
    h = h.reshape(rows, D_MODEL).astype(BF16)
    z = _dot(h, w_ref[...])
    cos, slo, shi = cos_ref[...], slo_ref[...], shi_ref[...]

    def rope_cols(off, width):
        return [_rope(z[:, off + j:off + j + LANES], cos, slo, shi)
                for j in range(0, width, LANES)]

    hg_ref[...] = z[:, OFF_HG:OFF_QB].reshape(bb, tt, 4 * A_WIDTH)
    scale = B_HD ** -0.5 * LOG2_E
    qb = jnp.concatenate(rope_cols(OFF_QB, B_WIDTH), axis=1) * scale
    qb_ref[...] = qb.astype(BF16).reshape(bb, tt, B_WIDTH)
    kb = jnp.concatenate(rope_cols(OFF_KB, KV_WIDTH), axis=1)
    k_ref[...] = kb.reshape(bb, tt, KV_WIDTH)
    v_ref[...] = z[:, OFF_VB:OFF_QI].reshape(bb, tt, KV_WIDTH)
    qi = jnp.concatenate(rope_cols(OFF_QI, IDX_WIDTH), axis=1)
    qi_ref[...] = qi.astype(BF16).reshape(bb, tt, IDX_WIDTH)
    last = _rope(z[:, OFF_KI:OFF_KI + LANES], cos, slo, shi)
    ki_ref[...] = last[:, 0:IDX_DIM].reshape(bb, tt, IDX_DIM)
    wi = z[:, OFF_WI:OFF_WI + IDX_HEADS] * (IDX_WIDTH ** -0.5)
    wi_ref[...] = wi.reshape(bb, tt, IDX_HEADS)
    if dsa_refs:
        kbf_ref, vt_ref, kibf_ref = dsa_refs
        kbf_ref[0] = kb.astype(BF16)
        vt_ref[0] = z[:, OFF_VB:OFF_QI].T.astype(BF16)
        kibf_ref[0] = last[:, 0:IDX_DIM].astype(BF16)


def _rope_tables(pos, reps):
    rot = B_HD // ROT_FRAC
    half = rot // 2
    inv = jnp.power(ROPE_THETA, -jnp.arange(half, dtype=F32) * (2.0 / rot))
    ang = pos.astype(F32)[:, None] * inv[None, :]
    cos, sin = jnp.cos(ang), jnp.sin(ang)
    t = pos.shape[0]
    ones = jnp.ones((t, B_HD - rot), F32)
    zeros = jnp.zeros((t, B_HD - rot), F32)
    zh = jnp.zeros((t, half), F32)
    cos_h = jnp.concatenate([cos, cos, ones], axis=1)
    slo_h = jnp.concatenate([-sin, zh, zeros], axis=1)
    shi_h = jnp.concatenate([zh, sin, zeros], axis=1)
    per = LANES // B_HD
    return tuple(jnp.tile(a, (reps, per)) for a in (cos_h, slo_h, shi_h))


def _inproj_call(x, mod, norm1, w_in_bf, pos, bb, tt, dsa_layouts):
    b, t, d = x.shape
    cos, slo, shi = _rope_tables(pos, bb)
    rows = bb * tt
    if bb == 1:
        tab_spec = pl.BlockSpec((tt, LANES), lambda i, j: (j, 0))
    else:
        tab_spec = pl.BlockSpec((rows, LANES), lambda i, j: (0, 0))

    def act_spec(w):
        return pl.BlockSpec((bb, tt, w), lambda i, j: (i, j, 0))

    def out(w, dt):
        return jax.ShapeDtypeStruct((b, t, w), dt)

    out_specs = [act_spec(4 * A_WIDTH), act_spec(B_WIDTH), act_spec(KV_WIDTH),
                 act_spec(KV_WIDTH), act_spec(IDX_WIDTH), act_spec(IDX_DIM),
                 act_spec(IDX_HEADS)]
    out_shape = [out(4 * A_WIDTH, F32), out(B_WIDTH, BF16), out(KV_WIDTH, F32),
                 out(KV_WIDTH, F32), out(IDX_WIDTH, BF16), out(IDX_DIM, F32),
                 out(IDX_HEADS, F32)]
    if dsa_layouts:
        assert bb == 1
        out_specs += [act_spec(KV_WIDTH), pl.BlockSpec((1, KV_WIDTH, tt), lambda i, j: (i, 0, j)),
                      act_spec(IDX_DIM)]
        out_shape += [out(KV_WIDTH, BF16), jax.ShapeDtypeStruct((b, KV_WIDTH, t), BF16),
                      out(IDX_DIM, BF16)]

    return pl.pallas_call(
        functools.partial(_inproj_kernel, bb=bb, tt=tt),
        grid=(b // bb, t // tt),
        in_specs=[act_spec(d),
                  pl.BlockSpec((bb, 1, 6 * d), lambda i, j: (i, 0, 0)),
                  pl.BlockSpec((1, 1, d), lambda i, j: (0, 0, 0)),
                  pl.BlockSpec((d, IN_WIDTH_PAD), lambda i, j: (0, 0)),
                  tab_spec, tab_spec, tab_spec],
        out_specs=out_specs,
        out_shape=out_shape,
        compiler_params=pltpu.CompilerParams(
            dimension_semantics=("parallel", "parallel"),
            vmem_limit_bytes=VMEM_LIMIT_BYTES),
        name="inproj",
    )(x, mod.reshape(b, 1, 6 * d), norm1.reshape(1, 1, d), w_in_bf, cos, slo, shi)


def _hgrn_tables(cc):
    nlev = int(np.log2(cc))
    t = np.arange(cc)[:, None]
    u = np.arange(cc)[None, :]
    mats = []
    for l in range(nlev):
        m = cc >> (l + 1)
        ref = (t // (2 * m)) * (2 * m) + m - 1
        qside = ((t // m) % 2) == 1
        mats.append(np.where(qside, (u > ref) & (u <= t), (u > t) & (u <= ref)))
    mats.append(u <= t)
    mats.append(u > t)
    w = np.concatenate(mats, axis=0).astype(np.float32)
    w = -np.concatenate([w, w], axis=1)
    lvl = np.full((cc, cc), -1, np.int32)
    for l in range(nlev):
        m = cc >> (l + 1)
        same_parent = (t // (2 * m)) == (u // (2 * m))
        lvl[same_parent & ((t // m) % 2 == 1) & ((u // m) % 2 == 0)] = l
    lvl[np.arange(cc), np.arange(cc)] = nlev
    return jnp.asarray(w, BF16), jnp.asarray(lvl), nlev


def _hgrn_kernel(q_ref, f_ref, i_ref, g_ref, lb_ref, gn_ref, s0_ref, w_ref, lvl_ref,
                 o_ref, s_out_ref, st_scr, *, cc, nc, nlev):
    ci = pl.program_id(1)

    @pl.when(ci == 0)
    def _():
        for h in range(A_HEADS):
            st_scr[h] = s0_ref[0, h].T

    row = lax.broadcasted_iota(jnp.int32, (cc, A_DK), 0)
    qsides = [((row // (cc >> (l + 1))) % 2) == 1 for l in range(nlev)]
    lvl = lvl_ref[...]
    lvl_masks = [lvl == l for l in range(nlev + 1)]
    w = w_ref[...]
    lb = lb_ref[...]
    f_all = lb + (1.0 - lb) * jax.nn.sigmoid(f_ref[0])
    nl = -jnp.log2(f_all)
    nl_hi = nl.astype(BF16)
    nl_lo = (nl - nl_hi.astype(F32)).astype(BF16)
    for c in range(nc):
        rs = slice(c * cc, (c + 1) * cc)
        dd_all = _dot(w, jnp.concatenate([nl_hi[rs], nl_lo[rs]], axis=0))
        for h in range(A_HEADS):
            sl = slice(h * A_DK, (h + 1) * A_DK)
            q = _silu(q_ref[0, rs, sl])
            kk = 1.0 - f_all[rs, sl]
            v = i_ref[0, rs, sl]
            dd = dd_all[:, sl]
            attn = jnp.zeros((cc, cc), F32)
            for l in range(nlev):
                e = jnp.exp2(dd[l * cc:(l + 1) * cc])
                xl = (jnp.where(qsides[l], q, kk) * e).astype(BF16)
                attn = attn + jnp.where(lvl_masks[l], _dot_nt(xl, xl), 0.0)
            attn = attn + jnp.where(lvl_masks[nlev], _dot_nt(q.astype(BF16), kk.astype(BF16)), 0.0)
            bcum = dd[nlev * cc:(nlev + 1) * cc]
            brev = dd[(nlev + 1) * cc:(nlev + 2) * cc]
            qg = (q * jnp.exp2(bcum)).astype(BF16)
            kg = (kk * jnp.exp2(brev)).astype(BF16)
            st = st_scr[h]
            o = _dot(attn.astype(BF16), v.astype(BF16)) + _dot_nt(qg, st.astype(BF16))
            dec = jnp.exp2(bcum[cc - 1:cc, :])
            st_scr[h] = st * dec + _dot(v.T.astype(BF16), kg)
            y = _rms(o) * gn_ref[:, sl]
            o_ref[0, rs, sl] = (y * _silu(g_ref[0, rs, sl])).astype(BF16)

    @pl.when(ci == pl.num_programs(1) - 1)
    def _():
        for h in range(A_HEADS):
            s_out_ref[0, h] = st_scr[h].T


def _hgrn_call(hg, lb, g_norm, s0, cc):
    b, t, _ = hg.shape
    w, lvl, nlev = _hgrn_tables(cc)
    nc = HGRN_CHUNKS_PER_STEP if t % (HGRN_CHUNKS_PER_STEP * cc) == 0 else 1
    rows = nc * cc

    def part(p):
        return pl.BlockSpec((1, rows, A_WIDTH), lambda i, j, p=p: (i, j, p))

    return pl.pallas_call(
        functools.partial(_hgrn_kernel, cc=cc, nc=nc, nlev=nlev),
        grid=(b, t // rows),
        in_specs=[part(0), part(1), part(2), part(3),
                  pl.BlockSpec((1, A_WIDTH), lambda i, j: (0, 0)),
                  pl.BlockSpec((1, A_WIDTH), lambda i, j: (0, 0)),
                  pl.BlockSpec((1, A_HEADS, A_DK, A_DV), lambda i, j: (i, 0, 0, 0)),
                  pl.BlockSpec(w.shape, lambda i, j: (0, 0)),
                  pl.BlockSpec(lvl.shape, lambda i, j: (0, 0))],
        out_specs=[pl.BlockSpec((1, rows, A_WIDTH), lambda i, j: (i, j, 0)),
                   pl.BlockSpec((1, A_HEADS, A_DK, A_DV), lambda i, j: (i, 0, 0, 0))],
        out_shape=[jax.ShapeDtypeStruct((b, t, A_WIDTH), BF16),
                   jax.ShapeDtypeStruct((b, A_HEADS, A_DK, A_DV), F32)],
        scratch_shapes=[pltpu.VMEM((A_HEADS, A_DV, A_DK), F32)],
        compiler_params=pltpu.CompilerParams(
            dimension_semantics=("parallel", "arbitrary"),
            vmem_limit_bytes=VMEM_LIMIT_BYTES),
        name="hgrn",
    )(hg, hg, hg, hg, lb.reshape(1, A_WIDTH), g_norm.reshape(1, A_WIDTH), s0, w, lvl)


def _fold8(x, op):
    parts = [x[r:r + SUBLANES] for r in range(0, x.shape[0], SUBLANES)]
    while len(parts) > 1:
        parts = [op(parts[i], parts[i + 1]) for i in range(0, len(parts) - 1, 2)] + (
            [parts[-1]] if len(parts) % 2 else [])
    return parts[0]


def _grouped_rhs(xs, heads, width, slot, slots):
    groups = len(xs)
    zero = jnp.zeros((xs[0].shape[0], width), xs[0].dtype)
    rows = []
    for h in heads:
        for g, x in enumerate(xs):
            parts = [zero] * (groups * slots)
            parts[g * slots + slot] = x[:, h * width:(h + 1) * width]
            rows.append(jnp.concatenate(parts, axis=1) if len(parts) > 1 else parts[0])
    return jnp.concatenate(rows, axis=0)


def _dsa_kernel(qi_ref, wit_ref, qb_ref, ki_ref, k_ref, vt_ref, tri_ref, o_ref,
                score_scr, logit_scr, acc_scr, *, groups, gq, qreal, tk, topk, pos0):
    qb = groups * gq
    blk = pl.program_id(1)
    last_pos = pos0 + (blk + 1) * qreal - 1
    extent = jnp.minimum((last_pos // CHUNK + 1) * CHUNK, tk)
    ntile = (extent + KEY_TILE - 1) // KEY_TILE
    lane = lax.broadcasted_iota(jnp.int32, (1, qb), 1)
    qpos = pos0 + blk * qreal + lane % gq
    key_end = jnp.minimum((qpos // CHUNK + 1) * CHUNK, tk)
    neg_inf = jnp.float32(-jnp.inf)

    def lane_cat(parts):
        return jnp.concatenate(parts, axis=1) if len(parts) > 1 else parts[0]

    qis = [qi_ref[g] for g in range(groups)]
    wit = lane_cat([wit_ref[g] for g in range(groups)])
    qi_pairs = [_grouped_rhs(qis, (2 * p, 2 * p + 1), IDX_DIM, 0, 1)
                for p in range(IDX_HEADS // 2)]

    def tile_start(j):
        return pl.multiple_of(j * KEY_TILE, KEY_TILE)

    def tile_loop(body, init):
        carry = lax.fori_loop(0, ntile // 2, lambda i, c: body(2 * i + 1, body(2 * i, c)), init)
        return lax.cond(ntile % 2 == 1, lambda c: body(ntile - 1, c), lambda c: c, carry)

    def score_tile(j, carry):
        k0 = tile_start(j)
        ki_t = lane_cat([ki_ref[g, pl.ds(k0, KEY_TILE), :] for g in range(groups)])
        acc = jnp.zeros((KEY_TILE, qb), F32)
        for p in range(IDX_HEADS // 2):
            s2 = jnp.maximum(_dot_nt(ki_t, qi_pairs[p]), 0.0)
            acc = acc + wit[2 * p:2 * p + 1, :] * s2[:, :qb]
            acc = acc + wit[2 * p + 1:2 * p + 2, :] * s2[:, qb:]
        kidx = k0 + lax.broadcasted_iota(jnp.int32, (KEY_TILE, qb), 0)
        score_scr[pl.ds(k0, KEY_TILE), :] = jnp.where(kidx < key_end, acc, neg_inf)
        return carry

    tile_loop(score_tile, 0)

    def count(pred_fn):
        def body(j, acc):
            s = score_scr[pl.ds(tile_start(j), KEY_TILE), :]
            return acc + _fold8(jnp.where(pred_fn(s), 1.0, 0.0), jnp.add)
        acc = tile_loop(body, jnp.zeros((SUBLANES, qb), F32))
        return jnp.sum(acc, axis=0, keepdims=True)

    def decode(t_u):
        key = t_u ^ INT_MIN
        return pltpu.bitcast(key ^ ((key >> 31) & np.int32(0x7FFFFFFF)), F32)

    def bisect(i, t_u):
        cand_u = t_u | (jnp.int32(1) << (31 - i))
        cand = decode(cand_u)
        cnt = count(lambda s: s >= cand)
        return jnp.where(cnt >= topk, cand_u, t_u)

    thr = decode(lax.fori_loop(0, 32, bisect, jnp.zeros((1, qb), jnp.int32)))
    thr = jnp.where(thr != thr, neg_inf, thr)
    need = topk - count(lambda s: s > thr)
    need = jnp.where(thr == neg_inf, 0.0, need)

    qqs = [qb_ref[g] for g in range(groups)]
    per_q = B_HEADS // B_KV_HEADS
    per_block = LANES // B_HD
    q_pairs = [_grouped_rhs(qqs, (n * per_q, n * per_q + 1), B_HD, n % per_block, per_block)
               for n in range(B_KV_HEADS)]
    tri = tri_ref[...]

    def logit_tile(j, carry):
        offs, mx = carry
        k0 = tile_start(j)
        s = score_scr[pl.ds(k0, KEY_TILE), :]
        tie = jnp.where(s == thr, 1.0, 0.0)
        rank = _dot(tri, tie.astype(BF16)) + offs
        picked = jnp.where(s > thr, 1.0, jnp.where(rank < need, tie, 0.0))
        bias = jnp.where(picked > 0.0, 0.0, NEG_BIG)
        bias2 = jnp.concatenate([bias, bias], axis=1)
        new_mx = []
        for n in range(B_KV_HEADS):
            kblk = (n // per_block) * LANES
            k_t = lane_cat([k_ref[g, pl.ds(k0, KEY_TILE), kblk:kblk + LANES]
                            for g in range(groups)])
            lg = _dot_nt(k_t, q_pairs[n]) + bias2
            logit_scr[n, pl.ds(k0, KEY_TILE), :] = lg
            new_mx.append(jnp.maximum(mx[n], _fold8(lg, jnp.maximum)))
        offs = offs + jnp.sum(_fold8(tie, jnp.add), axis=0, keepdims=True)
        return offs, tuple(new_mx)

    mx0 = tuple(jnp.full((SUBLANES, 2 * qb), NEG_BIG, F32) for _ in range(B_KV_HEADS))
    _, mx = tile_loop(logit_tile, (jnp.zeros((1, qb), F32), mx0))
    mx = [jnp.max(m, axis=0, keepdims=True) for m in mx]

    acc_scr[...] = jnp.zeros_like(acc_scr)

    def pv_tile(j, den):
        k0 = tile_start(j)
        new_den = []
        for n in range(B_KV_HEADS):
            p = jnp.exp2(logit_scr[n, pl.ds(k0, KEY_TILE), :] - mx[n])
            new_den.append(den[n] + _fold8(p, jnp.add))
            vts = [vt_ref[g, n * B_HD:(n + 1) * B_HD, pl.ds(k0, KEY_TILE)] for g in range(groups)]
            vt = jnp.concatenate(vts, axis=0) if groups > 1 else vts[0]
            acc_scr[n] += _dot(vt, p.astype(BF16))
        return tuple(new_den)

    den0 = tuple(jnp.zeros((SUBLANES, 2 * qb), F32) for _ in range(B_KV_HEADS))
    den = tile_loop(pv_tile, den0)
    lane_group = (lax.broadcasted_iota(jnp.int32, (1, 2 * qb), 1) % qb) // gq
    rows = []
    for n in range(B_KV_HEADS):
        acc = acc_scr[n]
        o2 = acc[0:B_HD]
        for g in range(1, groups):
            o2 = jnp.where(lane_group == g, acc[g * B_HD:(g + 1) * B_HD], o2)
        o2 = o2 / jnp.sum(den[n], axis=0, keepdims=True)
        rows += [o2[:, :qb], o2[:, qb:]]
    o_ref[...] = jnp.concatenate(rows, axis=0).T.astype(BF16).reshape(groups, gq, B_WIDTH)


def _dsa_call(qi, wi, qbs, k_bf, vt, ki_bf, tk, pos0):
    b, t, _ = qi.shape
    if t % DSA_QUERIES == 0:
        groups, gq = 1, DSA_QUERIES
    else:
        assert LANES % t == 0 and b % (LANES // t) == 0
        groups, gq = LANES // t, t
    qb = groups * gq
    tkp = ki_bf.shape[1]
    topk = min(TOPK_MAX, tk // 4)
    assert topk <= KEY_TILE
    wit = jnp.swapaxes(wi, 1, 2)
    tri = jnp.asarray(np.tril(np.ones((KEY_TILE, KEY_TILE), np.float32), -1), BF16)

    def q_spec(w):
        return pl.BlockSpec((groups, gq, w), lambda i, j: (i, j, 0))

    def kv_spec(rows, cols):
        return pl.BlockSpec((groups, rows, cols), lambda i, j: (i, 0, 0))

    return pl.pallas_call(
        functools.partial(_dsa_kernel, groups=groups, gq=gq, qreal=gq, tk=tk, topk=topk,
                          pos0=pos0),
        grid=(b // groups, t // gq),
        in_specs=[q_spec(IDX_WIDTH),
                  pl.BlockSpec((groups, IDX_HEADS, gq), lambda i, j: (i, 0, j)),
                  q_spec(B_WIDTH),
                  kv_spec(tkp, IDX_DIM), kv_spec(tkp, KV_WIDTH), kv_spec(KV_WIDTH, tkp),
                  pl.BlockSpec((KEY_TILE, KEY_TILE), lambda i, j: (0, 0))],
        out_specs=q_spec(B_WIDTH),
        out_shape=jax.ShapeDtypeStruct((b, t, B_WIDTH), BF16),
        scratch_shapes=[pltpu.VMEM((tkp, qb), F32),
                        pltpu.VMEM((B_KV_HEADS, tkp, 2 * qb), F32),
                        pltpu.VMEM((B_KV_HEADS, groups * B_HD, 2 * qb), F32)],
        compiler_params=pltpu.CompilerParams(
            dimension_semantics=("parallel", "parallel"),
            vmem_limit_bytes=VMEM_LIMIT_BYTES),
        name="dsa",
    )(qi, wit, qbs, ki_bf, k_bf, vt, tri)


def _out_kernel(x_ref, oa_ref, ob_ref, mod_ref, n2_ref, nf_ref, wo_ref, w1_ref, w2_ref,
                y_ref, *, bb, tt):
    rows = bb * tt
    x = x_ref[...]
    mod = mod_ref[...]
    g1 = mod[:, :, 2 * D_MODEL:3 * D_MODEL]
    sh2 = mod[:, :, 3 * D_MODEL:4 * D_MODEL]
    sc2 = mod[:, :, 4 * D_MODEL:5 * D_MODEL]
    g2 = mod[:, :, 5 * D_MODEL:6 * D_MODEL]
    oa = oa_ref[...].reshape(rows, A_WIDTH)
    ob = ob_ref[...].reshape(rows, B_WIDTH)
    mix = _dot(oa, wo_ref[0:A_WIDTH, :]) + _dot(ob, wo_ref[A_WIDTH:A_WIDTH + B_WIDTH, :])
    x = x + g1 * mix.reshape(bb, tt, D_MODEL)
    h2 = (_rms(x) * n2_ref[...]) * (1.0 + sc2) + sh2
    u = _dot(h2.reshape(rows, D_MODEL).astype(BF16), w1_ref[...])
    r = jnp.square(jnp.maximum(u, 0.0)).astype(BF16)
    x = x + g2 * _dot(r, w2_ref[...]).reshape(bb, tt, D_MODEL)
    y_ref[...] = _rms(x) * nf_ref[...]


def _out_call(x, oa, ob, mod, norm2, norm_f, wo_bf, w1_bf, w2_bf, bb, tt):
    b, t, d = x.shape

    def act_spec(w):
        return pl.BlockSpec((bb, tt, w), lambda i, j: (i, j, 0))

    def const_spec(shape):
        zeros = (0,) * len(shape)
        return pl.BlockSpec(shape, lambda i, j: zeros, pipeline_mode=pl.Buffered(1))

    return pl.pallas_call(
        functools.partial(_out_kernel, bb=bb, tt=tt),
        grid=(b // bb, t // tt),
        in_specs=[act_spec(d), act_spec(A_WIDTH), act_spec(B_WIDTH),
                  pl.BlockSpec((bb, 1, 6 * d), lambda i, j: (i, 0, 0)),
                  const_spec((1, 1, d)), const_spec((1, 1, d)),
                  const_spec(wo_bf.shape), const_spec(w1_bf.shape), const_spec(w2_bf.shape)],
        out_specs=act_spec(d),
        out_shape=jax.ShapeDtypeStruct((b, t, d), F32),
        compiler_params=pltpu.CompilerParams(
            dimension_semantics=("parallel", "parallel"),
            vmem_limit_bytes=VMEM_LIMIT_BYTES),
        name="out",
    )(x, oa, ob, mod.reshape(b, 1, 6 * d), norm2.reshape(1, 1, d), norm_f.reshape(1, 1, d),
      wo_bf, w1_bf, w2_bf)


def _layer(x, mod, pos0, s0, k_past, v_past, ki_past, weights, bb, tt, cc):
    norm1, w_in_bf, lb, g_norm, wo_bf, norm2, w1_bf, w2_bf, norm_f = weights
    b, t, _ = x.shape
    pos = pos0 + jnp.arange(t)
    outs = _inproj_call(x, mod, norm1, w_in_bf, pos, bb, tt, dsa_layouts=k_past is None)
    hg, qbs, k_new, v_new, qi, ki_new, wi = outs[:7]
    oa, s_new = _hgrn_call(hg, lb, g_norm, s0, cc)
    if k_past is None:
        assert t % KEY_TILE == 0
        tk = t
        k_bf, vt, ki_bf = outs[7:]
    else:
        n_past = k_past.shape[1]
        tk = n_past + t
        tkp = -(-tk // KEY_TILE) * KEY_TILE

        def cat(past, new):
            return jnp.pad(jnp.concatenate([past, new], axis=1),
                           ((0, 0), (0, tkp - tk), (0, 0))).astype(BF16)

        k_bf = cat(k_past.reshape(b, n_past, KV_WIDTH), k_new)
        vt = jnp.swapaxes(cat(v_past.reshape(b, n_past, KV_WIDTH), v_new), 1, 2)
        ki_bf = cat(ki_past, ki_new)
    ob = _dsa_call(qi, wi, qbs, k_bf, vt, ki_bf, tk, pos0)
    y = _out_call(x, oa, ob, mod, norm2, norm_f, wo_bf, w1_bf, w2_bf, bb, tt)
    return (y, k_new.reshape(b, t, B_KV_HEADS, B_HD), v_new.reshape(b, t, B_KV_HEADS, B_HD),
            ki_new, s_new)


def kernel(x_prompt, x_sample, cache_k, cache_v, cache_k_idx, state_hgrn, c_prompt, c_sample,
           w_mod, b_mod, norm1, w_in, lb_logits, g_norm_a, w_out, norm2, w_ff1, w_ff2, norm_f):
    depth = w_in.shape[0]
    assert depth == 1, "kernel is written for the single-layer configuration"
    lb_all = jnp.cumsum(jax.nn.softmax(lb_logits.astype(F32), axis=0), axis=0)
    bp, tp, _ = x_prompt.shape
    bs, ts, _ = x_sample.shape
    past = cache_k.shape[2]
    l = 0
    mod = _mod_call(jnp.concatenate([c_prompt, c_sample], axis=0), w_mod[l], b_mod[l])
    w_in_bf = jnp.pad(w_in[l], ((0, 0), (0, IN_WIDTH_PAD - IN_WIDTH))).astype(BF16)
    weights = (norm1[l], w_in_bf, lb_all[l], g_norm_a[l], w_out[l].astype(BF16), norm2[l],
               w_ff1[l].astype(BF16), w_ff2[l].astype(BF16), norm_f)
    s0 = jnp.zeros((bp, A_HEADS, A_DK, A_DV), F32)
    yp, kp, vp, kip, sp = _layer(x_prompt, mod[:bp], 0, s0, None, None, None, weights,
                                 bb=1, tt=min(512, tp), cc=min(128, tp))
    ys, ks, vs, kis, ss = _layer(x_sample, mod[bp:], past, state_hgrn[l], cache_k[l],
                                 cache_v[l], cache_k_idx[l], weights,
                                 bb=bs, tt=ts, cc=min(128, ts))
    return (yp, ys, kp[None], vp[None], kip[None], sp[None],
            ks[None], vs[None], kis[None], ss[None])
```

```python
import functools

import numpy as np
import jax
import jax.numpy as jnp
from jax import lax
from jax.experimental import pallas as pl
from jax.experimental.pallas import tpu as pltpu

D_MODEL = 1024
CHUNK = 64
A_HEADS = 4
A_DK = 128
A_DV = 128
A_WIDTH = A_HEADS * A_DV
B_HEADS = 8
B_KV_HEADS = 4
B_HD = 64
B_WIDTH = B_HEADS * B_HD
KV_WIDTH = B_KV_HEADS * B_HD
IDX_HEADS = 8
IDX_DIM = 64
IDX_WIDTH = IDX_HEADS * IDX_DIM
TOPK_MAX = 256
QBLOCK = 128
ROT_FRAC = 4
ROPE_THETA = 500000.0
D_FF = 4 * D_MODEL
EPS = 1e-6
IN_WIDTH = 4 * A_WIDTH + B_WIDTH + 2 * KV_WIDTH + IDX_WIDTH + IDX_DIM + IDX_HEADS

LANES = 128
SUBLANES = 8
KEY_TILE = 256
DSA_QUERIES = 256
HGRN_CHUNKS_PER_STEP = 4
IN_WIDTH_PAD = -(-IN_WIDTH // LANES) * LANES
VMEM_LIMIT_BYTES = 56 * 1024 * 1024

F32 = jnp.float32
BF16 = jnp.bfloat16
INT_MIN = np.int32(-2 ** 31)
NEG_BIG = -1e30
LOG2_E = 1.4426950408889634

OFF_HG = 0
OFF_QB = 4 * A_WIDTH
OFF_KB = OFF_QB + B_WIDTH
OFF_VB = OFF_KB + KV_WIDTH
OFF_QI = OFF_VB + KV_WIDTH
OFF_KI = OFF_QI + IDX_WIDTH
OFF_WI = OFF_KI + IDX_DIM


def _dot(a, b):
    return jnp.dot(a, b, preferred_element_type=F32)


def _dot_nt(a, b):
    return lax.dot_general(a, b, (((1,), (1,)), ((), ())), preferred_element_type=F32)


def _silu(x):
    return x * jax.nn.sigmoid(x)


def _rms(x):
    return x * lax.rsqrt(jnp.mean(jnp.square(x), axis=-1, keepdims=True) + EPS)


def _mod_kernel(c_ref, w_ref, b_ref, o_ref):
    a = _silu(c_ref[...])
    w = w_ref[...]
    a_hi = a.astype(BF16)
    a_lo = (a - a_hi.astype(F32)).astype(BF16)
    w_hi = w.astype(BF16)
    w_lo = (w - w_hi.astype(F32)).astype(BF16)
    o_ref[...] = _dot(a_hi, w_hi) + _dot(a_lo, w_hi) + _dot(a_hi, w_lo) + b_ref[...]


def _mod_call(c, w_mod, b_mod):
    rows, d = c.shape
    n = w_mod.shape[1]
    tn = 1024
    return pl.pallas_call(
        _mod_kernel,
        grid=(n // tn,),
        in_specs=[pl.BlockSpec((rows, d), lambda j: (0, 0)),
                  pl.BlockSpec((d, tn), lambda j: (0, j)),
                  pl.BlockSpec((1, tn), lambda j: (0, j))],
        out_specs=pl.BlockSpec((rows, tn), lambda j: (0, j)),
        out_shape=jax.ShapeDtypeStruct((rows, n), F32),
        compiler_params=pltpu.CompilerParams(vmem_limit_bytes=VMEM_LIMIT_BYTES),
        name="mod",
    )(c, w_mod, b_mod.reshape(1, n))


def _rope(x, cos, sin_lo, sin_hi):
    half = B_HD // ROT_FRAC // 2
    return (x * cos + pltpu.roll(x, half, 1) * sin_hi
            + pltpu.roll(x, LANES - half, 1) * sin_lo)


def _inproj_kernel(x_ref, mod_ref, n1_ref, w_ref, cos_ref, slo_ref, shi_ref,
                   hg_ref, qb_ref, k_ref, v_ref, qi_ref, ki_ref, wi_ref, *dsa_refs, bb, tt):
    rows = bb * tt
    x = x_ref[...]
    mod = mod_ref[...]
    sh1 = mod[:, :, 0:D_MODEL]
    sc1 = mod[:, :, D_MODEL:2 * D_MODEL]
    h = (_rms(x) * n1_ref[...]) * (1.0 + sc1) + sh1
    h = h.reshape(rows, D_MODEL).astype(BF16)
    z = _dot(h, w_ref[...])
    cos, slo, shi = cos_ref[...], slo_ref[...], shi_ref[...]

    def rope_cols(off, width):
        return [_rope(z[:, off + j:off + j + LANES], cos, slo, shi)
                for j in range(0, width, LANES)]

    hg_ref[...] = z[:, OFF_HG:OFF_QB].reshape(bb, tt, 4 * A_WIDTH)
    scale = B_HD ** -0.5 * LOG2_E
    qb = jnp.concatenate(rope_cols(OFF_QB, B_WIDTH), axis=1) * scale
    qb_ref[...] = qb.astype(BF16).reshape(bb, tt, B_WIDTH)
    kb = jnp.concatenate(rope_cols(OFF_KB, KV_WIDTH), axis=1)
    k_ref[...] = kb.reshape(bb, tt, KV_WIDTH)
    v_ref[...] = z[:, OFF_VB:OFF_QI].reshape(bb, tt, KV_WIDTH)
    qi = jnp.concatenate(rope_cols(OFF_QI, IDX_WIDTH), axis=1)
    qi_ref[...] = qi.astype(BF16).reshape(bb, tt, IDX_WIDTH)
    last = _rope(z[:, OFF_KI:OFF_KI + LANES], cos, slo, shi)
    ki_ref[...] = last[:, 0:IDX_DIM].reshape(bb, tt, IDX_DIM)
    wi = z[:, OFF_WI:OFF_WI + IDX_HEADS] * (IDX_WIDTH ** -0.5)
    wi_ref[...] = wi.reshape(bb, tt, IDX_HEADS)
    if dsa_refs:
        kbf_ref, vt_ref, kibf_ref = dsa_refs
        kbf_ref[0] = kb.astype(BF16)
        vt_ref[0] = z[:, OFF_VB:OFF_QI].T.astype(BF16)
        kibf_ref[0] = last[:, 0:IDX_DIM].astype(BF16)


def _rope_tables(pos, reps):
    rot = B_HD // ROT_FRAC
    half = rot // 2
    inv = jnp.power(ROPE_THETA, -jnp.arange(half, dtype=F32) * (2.0 / rot))
    ang = pos.astype(F32)[:, None] * inv[None, :]
    cos, sin = jnp.cos(ang), jnp.sin(ang)
    t = pos.shape[0]
    ones = jnp.ones((t, B_HD - rot), F32)
    zeros = jnp.zeros((t, B_HD - rot), F32)
    zh = jnp.zeros((t, half), F32)
    cos_h = jnp.concatenate([cos, cos, ones], axis=1)
    slo_h = jnp.concatenate([-sin, zh, zeros], axis=1)
    shi_h = jnp.concatenate([zh, sin, zeros], axis=1)
    per = LANES // B_HD
    return tuple(jnp.tile(a, (reps, per)) for a in (cos_h, slo_h, shi_h))


def _inproj_call(x, mod, norm1, w_in_bf, pos, bb, tt, dsa_layouts):
    b, t, d = x.shape
    cos, slo, shi = _rope_tables(pos, bb)
    rows = bb * tt
    if bb == 1:
        tab_spec = pl.BlockSpec((tt, LANES), lambda i, j: (j, 0))
    else:
        tab_spec = pl.BlockSpec((rows, LANES), lambda i, j: (0, 0))

    def act_spec(w):
        return pl.BlockSpec((bb, tt, w), lambda i, j: (i, j, 0))

    def out(w, dt):
        return jax.ShapeDtypeStruct((b, t, w), dt)

    out_specs = [act_spec(4 * A_WIDTH), act_spec(B_WIDTH), act_spec(KV_WIDTH),
                 act_spec(KV_WIDTH), act_spec(IDX_WIDTH), act_spec(IDX_DIM),
                 act_spec(IDX_HEADS)]
    out_shape = [out(4 * A_WIDTH, F32), out(B_WIDTH, BF16), out(KV_WIDTH, F32),
                 out(KV_WIDTH, F32), out(IDX_WIDTH, BF16), out(IDX_DIM, F32),
                 out(IDX_HEADS, F32)]
    if dsa_layouts:
        assert bb == 1
        out_specs += [act_spec(KV_WIDTH), pl.BlockSpec((1, KV_WIDTH, tt), lambda i, j: (i, 0, j)),
                      act_spec(IDX_DIM)]
        out_shape += [out(KV_WIDTH, BF16), jax.ShapeDtypeStruct((b, KV_WIDTH, t), BF16),
                      out(IDX_DIM, BF16)]

    return pl.pallas_call(
        functools.partial(_inproj_kernel, bb=bb, tt=tt),
        grid=(b // bb, t // tt),
        in_specs=[act_spec(d),
                  pl.BlockSpec((bb, 1, 6 * d), lambda i, j: (i, 0, 0)),
                  pl.BlockSpec((1, 1, d), lambda i, j: (0, 0, 0)),
                  pl.BlockSpec((d, IN_WIDTH_PAD), lambda i, j: (0, 0)),
                  tab_spec, tab_spec, tab_spec],
        out_specs=out_specs,
        out_shape=out_shape,
        compiler_params=pltpu.CompilerParams(
            dimension_semantics=("parallel", "parallel"),
            vmem_limit_bytes=VMEM_LIMIT_BYTES),
        name="inproj",
    )(x, mod.reshape(b, 1, 6 * d), norm1.reshape(1, 1, d), w_in_bf, cos, slo, shi)


def _hgrn_tables(cc):
    nlev = int(np.log2(cc))
    t = np.arange(cc)[:, None]
    u = np.arange(cc)[None, :]
    mats = []
    for l in range(nlev):
        m = cc >> (l + 1)
        ref = (t // (2 * m)) * (2 * m) + m - 1
        qside = ((t // m) % 2) == 1
        mats.append(np.where(qside, (u > ref) & (u <= t), (u > t) & (u <= ref)))
    mats.append(u <= t)
    mats.append(u > t)
    w = np.concatenate(mats, axis=0).astype(np.float32)
    w = -np.concatenate([w, w], axis=1)
    lvl = np.full((cc, cc), -1, np.int32)
    for l in range(nlev):
        m = cc >> (l + 1)
        same_parent = (t // (2 * m)) == (u // (2 * m))
        lvl[same_parent & ((t // m) % 2 == 1) & ((u // m) % 2 == 0)] = l
    lvl[np.arange(cc), np.arange(cc)] = nlev
    return jnp.asarray(w, BF16), jnp.asarray(lvl), nlev


def _hgrn_kernel(q_ref, f_ref, i_ref, g_ref, lb_ref, gn_ref, s0_ref, w_ref, lvl_ref,
                 o_ref, s_out_ref, st_scr, *, cc, nc, nlev):
    ci = pl.program_id(1)

    @pl.when(ci == 0)
    def _():
        for h in range(A_HEADS):
            st_scr[h] = s0_ref[0, h].T

    row = lax.broadcasted_iota(jnp.int32, (cc, A_DK), 0)
    qsides = [((row // (cc >> (l + 1))) % 2) == 1 for l in range(nlev)]
    lvl = lvl_ref[...]
    lvl_masks = [lvl == l for l in range(nlev + 1)]
    w = w_ref[...]
    lb = lb_ref[...]
    f_all = lb + (1.0 - lb) * jax.nn.sigmoid(f_ref[0])
    nl = -jnp.log2(f_all)
    nl_hi = nl.astype(BF16)
    nl_lo = (nl - nl_hi.astype(F32)).astype(BF16)
    for c in range(nc):
        rs = slice(c * cc, (c + 1) * cc)
        dd_all = _dot(w, jnp.concatenate([nl_hi[rs], nl_lo[rs]], axis=0))
        for h in range(A_HEADS):
            sl = slice(h * A_DK, (h + 1) * A_DK)
            q = _silu(q_ref[0, rs, sl])
            kk = 1.0 - f_all[rs, sl]
            v = i_ref[0, rs, sl]
            dd = dd_all[:, sl]
            attn = jnp.zeros((cc, cc), F32)
            for l in range(nlev):
                e = jnp.exp2(dd[l * cc:(l + 1) * cc])
                xl = (jnp.where(qsides[l], q, kk) * e).astype(BF16)
                attn = attn + jnp.where(lvl_masks[l], _dot_nt(xl, xl), 0.0)
            attn = attn + jnp.where(lvl_masks[nlev], _dot_nt(q.astype(BF16), kk.astype(BF16)), 0.0)
            bcum = dd[nlev * cc:(nlev + 1) * cc]
            brev = dd[(nlev + 1) * cc:(nlev + 2) * cc]
            qg = (q * jnp.exp2(bcum)).astype(BF16)
            kg = (kk * jnp.exp2(brev)).astype(BF16)
            st = st_scr[h]
            o = _dot(attn.astype(BF16), v.astype(BF16)) + _dot_nt(qg, st.astype(BF16))
            dec = jnp.exp2(bcum[cc - 1:cc, :])
            st_scr[h] = st * dec + _dot(v.T.astype(BF16), kg)
            y = _rms(o) * gn_ref[:, sl]
            o_ref[0, rs, sl] = (y * _silu(g_ref[0, rs, sl])).astype(BF16)

    @pl.when(ci == pl.num_programs(1) - 1)
    def _():
        for h in range(A_HEADS):
            s_out_ref[0, h] = st_scr[h].T


def _hgrn_call(hg, lb, g_norm, s0, cc):
    b, t, _ = hg.shape
    w, lvl, nlev = _hgrn_tables(cc)
    nc = HGRN_CHUNKS_PER_STEP if t % (HGRN_CHUNKS_PER_STEP * cc) == 0 else 1
    rows = nc * cc

    def part(p):
        return pl.BlockSpec((1, rows, A_WIDTH), lambda i, j, p=p: (i, j, p))

    return pl.pallas_call(
        functools.partial(_hgrn_kernel, cc=cc, nc=nc, nlev=nlev),
        grid=(b, t // rows),
        in_specs=[part(0), part(1), part(2), part(3),
                  pl.BlockSpec((1, A_WIDTH), lambda i, j: (0, 0)),
                  pl.BlockSpec((1, A_WIDTH), lambda i, j: (0, 0)),
                  pl.BlockSpec((1, A_HEADS, A_DK, A_DV), lambda i, j: (i, 0, 0, 0)),
                  pl.BlockSpec(w.shape, lambda i, j: (0, 0)),
                  pl.BlockSpec(lvl.shape, lambda i, j: (0, 0))],
        out_specs=[pl.BlockSpec((1, rows, A_WIDTH), lambda i, j: (i, j, 0)),
                   pl.BlockSpec((1, A_HEADS, A_DK, A_DV), lambda i, j: (i, 0, 0, 0))],
        out_shape=[jax.ShapeDtypeStruct((b, t, A_WIDTH), BF16),
                   jax.ShapeDtypeStruct((b, A_HEADS, A_DK, A_DV), F32)],
        scratch_shapes=[pltpu.VMEM((A_HEADS, A_DV, A_DK), F32)],
        compiler_params=pltpu.CompilerParams(
            dimension_semantics=("parallel", "arbitrary"),
            vmem_limit_bytes=VMEM_LIMIT_BYTES),
        name="hgrn",
    )(hg, hg, hg, hg, lb.reshape(1, A_WIDTH), g_norm.reshape(1, A_WIDTH), s0, w, lvl)


def _fold8(x, op):
    parts = [x[r:r + SUBLANES] for r in range(0, x.shape[0], SUBLANES)]
    while len(parts) > 1:
        parts = [op(parts[i], parts[i + 1]) for i in range(0, len(parts) - 1, 2)] + (
            [parts[-1]] if len(parts) % 2 else [])
    return parts[0]


def _grouped_rhs(xs, heads, width, slot, slots):
    groups = len(xs)
    zero = jnp.zeros((xs[0].shape[0], width), xs[0].dtype)
    rows = []
    for h in heads:
        for g, x in enumerate(xs):
            parts = [zero] * (groups * slots)
            parts[g * slots + slot] = x[:, h * width:(h + 1) * width]
            rows.append(jnp.concatenate(parts, axis=1) if len(parts) > 1 else parts[0])
    return jnp.concatenate(rows, axis=0)


def _dsa_kernel(qi_ref, wit_ref, qb_ref, ki_ref, k_ref, vt_ref, tri_ref, o_ref,
                score_scr, logit_scr, acc_scr, *, groups, gq, qreal, tk, topk, pos0):
    qb = groups * gq
    blk = pl.program_id(1)
    last_pos = pos0 + (blk + 1) * qreal - 1
    extent = jnp.minimum((last_pos // CHUNK + 1) * CHUNK, tk)
    ntile = (extent + KEY_TILE - 1) // KEY_TILE
    lane = lax.broadcasted_iota(jnp.int32, (1, qb), 1)
    qpos = pos0 + blk * qreal + lane % gq
    key_end = jnp.minimum((qpos // CHUNK + 1) * CHUNK, tk)
    neg_inf = jnp.float32(-jnp.inf)

    def lane_cat(parts):
        return jnp.concatenate(parts, axis=1) if len(parts) > 1 else parts[0]

    qis = [qi_ref[g] for g in range(groups)]
    wit = lane_cat([wit_ref[g] for g in range(groups)])
    qi_pairs = [_grouped_rhs(qis, (2 * p, 2 * p + 1), IDX_DIM, 0, 1)
                for p in range(IDX_HEADS // 2)]

    def tile_start(j):
        return pl.multiple_of(j * KEY_TILE, KEY_TILE)

    def tile_loop(body, init):
        def run(first, count, c):
            for u in range(count):
                c = body(first + u, c)
            return c
        carry = lax.fori_loop(0, ntile // 4, lambda i, c: run(4 * i, 4, c), init)
        done = (ntile // 4) * 4
        carry = lax.cond((ntile & 2) != 0, lambda c: run(done, 2, c), lambda c: c, carry)
        return lax.cond((ntile & 1) != 0, lambda c: body(ntile - 1, c), lambda c: c, carry)

    def score_tile(j, carry):
        k0 = tile_start(j)
        ki_t = lane_cat([ki_ref[g, pl.ds(k0, KEY_TILE), :] for g in range(groups)])
        acc = jnp.zeros((KEY_TILE, qb), F32)
        for p in range(IDX_HEADS // 2):
            s2 = jnp.maximum(_dot_nt(ki_t, qi_pairs[p]), 0.0)
            acc = acc + wit[2 * p:2 * p + 1, :] * s2[:, :qb]
            acc = acc + wit[2 * p + 1:2 * p + 2, :] * s2[:, qb:]
        kidx = k0 + lax.broadcasted_iota(jnp.int32, (KEY_TILE, qb), 0)
        score_scr[pl.ds(k0, KEY_TILE), :] = jnp.where(kidx < key_end, acc, neg_inf)
        return carry

    tile_loop(score_tile, 0)

    def count(pred_fn):
        def body(j, acc):
            s = score_scr[pl.ds(tile_start(j), KEY_TILE), :]
            return acc + _fold8(jnp.where(pred_fn(s), 1.0, 0.0), jnp.add)
        acc = tile_loop(body, jnp.zeros((SUBLANES, qb), F32))
        return jnp.sum(acc, axis=0, keepdims=True)

    def decode(t_u):
        key = t_u ^ INT_MIN
        return pltpu.bitcast(key ^ ((key >> 31) & np.int32(0x7FFFFFFF)), F32)

    def bisect(i, t_u):
        cand_u = t_u | (jnp.int32(1) << (31 - i))
        cand = decode(cand_u)
        cnt = count(lambda s: s >= cand)
        return jnp.where(cnt >= topk, cand_u, t_u)

    thr = decode(lax.fori_loop(0, 32, bisect, jnp.zeros((1, qb), jnp.int32)))
    thr = jnp.where(thr != thr, neg_inf, thr)
    need = topk - count(lambda s: s > thr)
    need = jnp.where(thr == neg_inf, 0.0, need)

    qqs = [qb_ref[g] for g in range(groups)]
    per_q = B_HEADS // B_KV_HEADS
    per_block = LANES // B_HD
    q_pairs = [_grouped_rhs(qqs, (n * per_q, n * per_q + 1), B_HD, n % per_block, per_block)
               for n in range(B_KV_HEADS)]
    tri = tri_ref[...]

    def logit_tile(j, carry):
        offs, mx = carry
        k0 = tile_start(j)
        s = score_scr[pl.ds(k0, KEY_TILE), :]
        tie = jnp.where(s == thr, 1.0, 0.0)
        rank = _dot(tri, tie.astype(BF16)) + offs
        picked = jnp.where(s > thr, 1.0, jnp.where(rank < need, tie, 0.0))
        bias = jnp.where(picked > 0.0, 0.0, NEG_BIG)
        bias2 = jnp.concatenate([bias, bias], axis=1)
        new_mx = []
        for n in range(B_KV_HEADS):
            kblk = (n // per_block) * LANES
            k_t = lane_cat([k_ref[g, pl.ds(k0, KEY_TILE), kblk:kblk + LANES]
                            for g in range(groups)])
            lg = _dot_nt(k_t, q_pairs[n]) + bias2
            logit_scr[n, pl.ds(k0, KEY_TILE), :] = lg
            new_mx.append(jnp.maximum(mx[n], _fold8(lg, jnp.maximum)))
        offs = offs + jnp.sum(_fold8(tie, jnp.add), axis=0, keepdims=True)
        return offs, tuple(new_mx)

    mx0 = tuple(jnp.full((SUBLANES, 2 * qb), NEG_BIG, F32) for _ in range(B_KV_HEADS))
    _, mx = tile_loop(logit_tile, (jnp.zeros((1, qb), F32), mx0))
    mx = [jnp.max(m, axis=0, keepdims=True) for m in mx]

    acc_scr[...] = jnp.zeros_like(acc_scr)

    def pv_tile(j, den):
        k0 = tile_start(j)
        new_den = []
        for n in range(B_KV_HEADS):
            p = jnp.exp2(logit_scr[n, pl.ds(k0, KEY_TILE), :] - mx[n])
            new_den.append(den[n] + _fold8(p, jnp.add))
            vts = [vt_ref[g, n * B_HD:(n + 1) * B_HD, pl.ds(k0, KEY_TILE)] for g in range(groups)]
            vt = jnp.concatenate(vts, axis=0) if groups > 1 else vts[0]
            acc_scr[n] += _dot(vt, p.astype(BF16))
        return tuple(new_den)

    den0 = tuple(jnp.zeros((SUBLANES, 2 * qb), F32) for _ in range(B_KV_HEADS))
    den = tile_loop(pv_tile, den0)
    lane_group = (lax.broadcasted_iota(jnp.int32, (1, 2 * qb), 1) % qb) // gq
    rows = []
    for n in range(B_KV_HEADS):
        acc = acc_scr[n]
        o2 = acc[0:B_HD]
        for g in range(1, groups):
            o2 = jnp.where(lane_group == g, acc[g * B_HD:(g + 1) * B_HD], o2)
        o2 = o2 / jnp.sum(den[n], axis=0, keepdims=True)
        rows += [o2[:, :qb], o2[:, qb:]]
    o_ref[...] = jnp.concatenate(rows, axis=0).T.astype(BF16).reshape(groups, gq, B_WIDTH)


def _dsa_call(qi, wi, qbs, k_bf, vt, ki_bf, tk, pos0):
    b, t, _ = qi.shape
    if t % DSA_QUERIES == 0:
        groups, gq = 1, DSA_QUERIES
    else:
        assert LANES % t == 0 and b % (LANES // t) == 0
        groups, gq = LANES // t, t
    qb = groups * gq
    tkp = ki_bf.shape[1]
    topk = min(TOPK_MAX, tk // 4)
    assert topk <= KEY_TILE
    wit = jnp.swapaxes(wi, 1, 2)
    tri = jnp.asarray(np.tril(np.ones((KEY_TILE, KEY_TILE), np.float32), -1), BF16)

    def q_spec(w):
        return pl.BlockSpec((groups, gq, w), lambda i, j: (i, j, 0))

    def kv_spec(rows, cols):
        return pl.BlockSpec((groups, rows, cols), lambda i, j: (i, 0, 0))

    return pl.pallas_call(
        functools.partial(_dsa_kernel, groups=groups, gq=gq, qreal=gq, tk=tk, topk=topk,
                          pos0=pos0),
        grid=(b // groups, t // gq),
        in_specs=[q_spec(IDX_WIDTH),
                  pl.BlockSpec((groups, IDX_HEADS, gq), lambda i, j: (i, 0, j)),
                  q_spec(B_WIDTH),
                  kv_spec(tkp, IDX_DIM), kv_spec(tkp, KV_WIDTH), kv_spec(KV_WIDTH, tkp),
                  pl.BlockSpec((KEY_TILE, KEY_TILE), lambda i, j: (0, 0))],
        out_specs=q_spec(B_WIDTH),
        out_shape=jax.ShapeDtypeStruct((b, t, B_WIDTH), BF16),
        scratch_shapes=[pltpu.VMEM((tkp, qb), F32),
                        pltpu.VMEM((B_KV_HEADS, tkp, 2 * qb), F32),
                        pltpu.VMEM((B_KV_HEADS, groups * B_HD, 2 * qb), F32)],
        compiler_params=pltpu.CompilerParams(
            dimension_semantics=("parallel", "parallel"),
            vmem_limit_bytes=VMEM_LIMIT_BYTES),
        name="dsa",
    )(qi, wit, qbs, ki_bf, k_bf, vt, tri)


def _out_kernel(x_ref, oa_ref, ob_ref, mod_ref, n2_ref, nf_ref, wo_ref, w1_ref, w2_ref,
                y_ref, *, bb, tt):
    rows = bb * tt
    x = x_ref[...]
    mod = mod_ref[...]
    g1 = mod[:, :, 2 * D_MODEL:3 * D_MODEL]
    sh2 = mod[:, :, 3 * D_MODEL:4 * D_MODEL]
    sc2 = mod[:, :, 4 * D_MODEL:5 * D_MODEL]
    g2 = mod[:, :, 5 * D_MODEL:6 * D_MODEL]
    oa = oa_ref[...].reshape(rows, A_WIDTH)
    ob = ob_ref[...].reshape(rows, B_WIDTH)
    mix = _dot(oa, wo_ref[0:A_WIDTH, :]) + _dot(ob, wo_ref[A_WIDTH:A_WIDTH + B_WIDTH, :])
    x = x + g1 * mix.reshape(bb, tt, D_MODEL)
    h2 = (_rms(x) * n2_ref[...]) * (1.0 + sc2) + sh2
    u = _dot(h2.reshape(rows, D_MODEL).astype(BF16), w1_ref[...])
    r = jnp.square(jnp.maximum(u, 0.0)).astype(BF16)
    x = x + g2 * _dot(r, w2_ref[...]).reshape(bb, tt, D_MODEL)
    y_ref[...] = _rms(x) * nf_ref[...]


def _out_call(x, oa, ob, mod, norm2, norm_f, wo_bf, w1_bf, w2_bf, bb, tt):
    b, t, d = x.shape

    def act_spec(w):
        return pl.BlockSpec((bb, tt, w), lambda i, j: (i, j, 0))

    def const_spec(shape):
        zeros = (0,) * len(shape)
        return pl.BlockSpec(shape, lambda i, j: zeros, pipeline_mode=pl.Buffered(1))

    return pl.pallas_call(
        functools.partial(_out_kernel, bb=bb, tt=tt),
        grid=(b // bb, t // tt),
        in_specs=[act_spec(d), act_spec(A_WIDTH), act_spec(B_WIDTH),
                  pl.BlockSpec((bb, 1, 6 * d), lambda i, j: (i, 0, 0)),
                  const_spec((1, 1, d)), const_spec((1, 1, d)),
                  const_spec(wo_bf.shape), const_spec(w1_bf.shape), const_spec(w2_bf.shape)],
        out_specs=act_spec(d),
        out_shape=jax.ShapeDtypeStruct((b, t, d), F32),
        compiler_params=pltpu.CompilerParams(
            dimension_semantics=("parallel", "parallel"),
            vmem_limit_bytes=VMEM_LIMIT_BYTES),
        name="out",
    )(x, oa, ob, mod.reshape(b, 1, 6 * d), norm2.reshape(1, 1, d), norm_f.reshape(1, 1, d),
      wo_bf, w1_bf, w2_bf)


def _layer(x, mod, pos0, s0, k_past, v_past, ki_past, weights, bb, tt, cc):
    norm1, w_in_bf, lb, g_norm, wo_bf, norm2, w1_bf, w2_bf, norm_f = weights
    b, t, _ = x.shape
    pos = pos0 + jnp.arange(t)
    outs = _inproj_call(x, mod, norm1, w_in_bf, pos, bb, tt, dsa_layouts=k_past is None)
    hg, qbs, k_new, v_new, qi, ki_new, wi = outs[:7]
    oa, s_new = _hgrn_call(hg, lb, g_norm, s0, cc)
    if k_past is None:
        assert t % KEY_TILE == 0
        tk = t
        k_bf, vt, ki_bf = outs[7:]
    else:
        n_past = k_past.shape[1]
        tk = n_past + t
        tkp = -(-tk // KEY_TILE) * KEY_TILE

        def cat(past, new):
            return jnp.pad(jnp.concatenate([past, new], axis=1),
                           ((0, 0), (0, tkp - tk), (0, 0))).astype(BF16)

        k_bf = cat(k_past.reshape(b, n_past, KV_WIDTH), k_new)
        vt = jnp.swapaxes(cat(v_past.reshape(b, n_past, KV_WIDTH), v_new), 1, 2)
        ki_bf = cat(ki_past, ki_new)
    ob = _dsa_call(qi, wi, qbs, k_bf, vt, ki_bf, tk, pos0)
    y = _out_call(x, oa, ob, mod, norm2, norm_f, wo_bf, w1_bf, w2_bf, bb, tt)
    return (y, k_new.reshape(b, t, B_KV_HEADS, B_HD), v_new.reshape(b, t, B_KV_HEADS, B_HD),
            ki_new, s_new)


def kernel(x_prompt, x_sample, cache_k, cache_v, cache_k_idx, state_hgrn, c_prompt, c_sample,
           w_mod, b_mod, norm1, w_in, lb_logits, g_norm_a, w_out, norm2, w_ff1, w_ff2, norm_f):
    depth = w_in.shape[0]
    assert depth == 1, "kernel is written for the single-layer configuration"
    lb_all = jnp.cumsum(jax.nn.softmax(lb_logits.astype(F32), axis=0), axis=0)
    bp, tp, _ = x_prompt.shape
    bs, ts, _ = x_sample.shape
    past = cache_k.shape[2]
    l = 0
    mod = _mod_call(jnp.concatenate([c_prompt, c_sample], axis=0), w_mod[l], b_mod[l])
    w_in_bf = jnp.pad(w_in[l], ((0, 0), (0, IN_WIDTH_PAD - IN_WIDTH))).astype(BF16)
    weights = (norm1[l], w_in_bf, lb_all[l], g_norm_a[l], w_out[l].astype(BF16), norm2[l],
               w_ff1[l].astype(BF16), w_ff2[l].astype(BF16), norm_f)
    s0 = jnp.zeros((bp, A_HEADS, A_DK, A_DV), F32)
    yp, kp, vp, kip, sp = _layer(x_prompt, mod[:bp], 0, s0, None, None, None, weights,
                                 bb=1, tt=min(512, tp), cc=min(128, tp))
    ys, ks, vs, kis, ss = _layer(x_sample, mod[bp:], past, state_hgrn[l], cache_k[l],
                                 cache_v[l], cache_k_idx[l], weights,
                                 bb=bs, tt=ts, cc=min(128, ts))
    return (yp, ys, kp[None], vp[None], kip[None], sp[None],
            ks[None], vs[None], kis[None], ss[None])
```

```python
import functools

import numpy as np
import jax
import jax.numpy as jnp
from jax import lax
from jax.experimental import pallas as pl
from jax.experimental.pallas import tpu as pltpu

D_MODEL = 1024
CHUNK = 64
A_HEADS = 4
A_DK = 128
A_DV = 128
A_WIDTH = A_HEADS * A_DV
B_HEADS = 8
B_KV_HEADS = 4
B_HD = 64
B_WIDTH = B_HEADS * B_HD
KV_WIDTH = B_KV_HEADS * B_HD
IDX_HEADS = 8
IDX_DIM = 64
IDX_WIDTH = IDX_HEADS * IDX_DIM
TOPK_MAX = 256
QBLOCK = 128
ROT_FRAC = 4
ROPE_THETA = 500000.0
D_FF = 4 * D_MODEL
EPS = 1e-6
IN_WIDTH = 4 * A_WIDTH + B_WIDTH + 2 * KV_WIDTH + IDX_WIDTH + IDX_DIM + IDX_HEADS

LANES = 128
SUBLANES = 8
BF16_ROWS = 16
KEY_TILE = 256
DSA_QUERIES = 256
HGRN_CHUNKS_PER_STEP = 8
IN_WIDTH_PAD = -(-IN_WIDTH // LANES) * LANES
VMEM_LIMIT_BYTES = 56 * 1024 * 1024

F32 = jnp.float32
BF16 = jnp.bfloat16
INT_MIN = np.int32(-2 ** 31)
NEG_BIG = -1e30
LOG2_E = 1.4426950408889634

OFF_HG = 0
OFF_QB = 4 * A_WIDTH
OFF_KB = OFF_QB + B_WIDTH
OFF_VB = OFF_KB + KV_WIDTH
OFF_QI = OFF_VB + KV_WIDTH
OFF_KI = OFF_QI + IDX_WIDTH
OFF_WI = OFF_KI + IDX_DIM


def _dot(a, b):
    return jnp.dot(a, b, preferred_element_type=F32)


def _dot_nt(a, b):
    return lax.dot_general(a, b, (((1,), (1,)), ((), ())), preferred_element_type=F32)


def _silu(x):
    return x * jax.nn.sigmoid(x)


def _rms(x):
    return x * lax.rsqrt(jnp.mean(jnp.square(x), axis=-1, keepdims=True) + EPS)


def _block_diag(x, width):
    zero = jnp.zeros((x.shape[0], width), x.dtype)
    return jnp.concatenate([jnp.concatenate([x[:, :width], zero], axis=1),
                            jnp.concatenate([zero, x[:, width:]], axis=1)], axis=0)


def _mod_kernel(c_ref, w_ref, b_ref, o_ref):
    a = _silu(c_ref[...])
    w = w_ref[...]
    a_hi = a.astype(BF16)
    a_lo = (a - a_hi.astype(F32)).astype(BF16)
    w_hi = w.astype(BF16)
    w_lo = (w - w_hi.astype(F32)).astype(BF16)
    o_ref[...] = _dot(a_hi, w_hi) + _dot(a_lo, w_hi) + _dot(a_hi, w_lo) + b_ref[...]


def _mod_call(c, w_mod, b_mod):
    rows, d = c.shape
    n = w_mod.shape[1]
    tn = 1024
    return pl.pallas_call(
        _mod_kernel,
        grid=(n // tn,),
        in_specs=[pl.BlockSpec((rows, d), lambda j: (0, 0)),
                  pl.BlockSpec((d, tn), lambda j: (0, j)),
                  pl.BlockSpec((1, tn), lambda j: (0, j))],
        out_specs=pl.BlockSpec((rows, tn), lambda j: (0, j)),
        out_shape=jax.ShapeDtypeStruct((rows, n), F32),
        compiler_params=pltpu.CompilerParams(vmem_limit_bytes=VMEM_LIMIT_BYTES),
        name="mod",
    )(c, w_mod, b_mod.reshape(1, n))


def _rope(x, cos, sin_lo, sin_hi):
    half = B_HD // ROT_FRAC // 2
    return (x * cos + pltpu.roll(x, half, 1) * sin_hi
            + pltpu.roll(x, LANES - half, 1) * sin_lo)


def _inproj_kernel(x_ref, mod_ref, n1_ref, w_ref, cos_ref, slo_ref, shi_ref,
                   hg_ref, qb_ref, k_ref, v_ref, qi_ref, ki_ref, wi_ref, *dsa_refs, bb, tt):
    rows = bb * tt
    x = x_ref[...]
    mod = mod_ref[...]
    sh1 = mod[:, :, 0:D_MODEL]
    sc1 = mod[:, :, D_MODEL:2 * D_MODEL]
    h = (_rms(x) * n1_ref[...]) * (1.0 + sc1) + sh1
    h = h.reshape(rows, D_MODEL).astype(BF16)
    z = _dot(h, w_ref[...])
    cos, slo, shi = cos_ref[...], slo_ref[...], shi_ref[...]

    def rope_cols(off, width):
        return [_rope(z[:, off + j:off + j + LANES], cos, slo, shi)
                for j in range(0, width, LANES)]

    hg_ref[...] = z[:, OFF_HG:OFF_QB].reshape(bb, tt, 4 * A_WIDTH)
    scale = B_HD ** -0.5 * LOG2_E
    qb = jnp.concatenate(rope_cols(OFF_QB, B_WIDTH), axis=1) * scale
    qb_ref[...] = qb.astype(BF16).reshape(bb, tt, B_WIDTH)
    kb = jnp.concatenate(rope_cols(OFF_KB, KV_WIDTH), axis=1)
    k_ref[...] = kb.reshape(bb, tt, KV_WIDTH)
    v_ref[...] = z[:, OFF_VB:OFF_QI].reshape(bb, tt, KV_WIDTH)
    qi = jnp.concatenate(rope_cols(OFF_QI, IDX_WIDTH), axis=1)
    qi_ref[...] = qi.astype(BF16).reshape(bb, tt, IDX_WIDTH)
    last = _rope(z[:, OFF_KI:OFF_KI + LANES], cos, slo, shi)
    ki_ref[...] = last[:, 0:IDX_DIM].reshape(bb, tt, IDX_DIM)
    wi = z[:, OFF_WI:OFF_WI + IDX_HEADS] * (IDX_WIDTH ** -0.5)
    wi_ref[...] = wi.reshape(bb, tt, IDX_HEADS)
    if dsa_refs:
        kbf_ref, vt_ref, kibf_ref = dsa_refs
        kbf_ref[0] = kb.astype(BF16)
        vt_ref[0] = z[:, OFF_VB:OFF_QI].T.astype(BF16)
        kibf_ref[0] = last[:, 0:IDX_DIM].astype(BF16)


def _rope_tables(pos, reps):
    rot = B_HD // ROT_FRAC
    half = rot // 2
    inv = jnp.power(ROPE_THETA, -jnp.arange(half, dtype=F32) * (2.0 / rot))
    ang = pos.astype(F32)[:, None] * inv[None, :]
    cos, sin = jnp.cos(ang), jnp.sin(ang)
    t = pos.shape[0]
    ones = jnp.ones((t, B_HD - rot), F32)
    zeros = jnp.zeros((t, B_HD - rot), F32)
    zh = jnp.zeros((t, half), F32)
    cos_h = jnp.concatenate([cos, cos, ones], axis=1)
    slo_h = jnp.concatenate([-sin, zh, zeros], axis=1)
    shi_h = jnp.concatenate([zh, sin, zeros], axis=1)
    per = LANES // B_HD
    return tuple(jnp.tile(a, (reps, per)) for a in (cos_h, slo_h, shi_h))


def _inproj_call(x, mod, norm1, w_in_bf, pos, bb, tt, dsa_layouts):
    b, t, d = x.shape
    cos, slo, shi = _rope_tables(pos, bb)
    rows = bb * tt
    if bb == 1:
        tab_spec = pl.BlockSpec((tt, LANES), lambda i, j: (j, 0))
    else:
        tab_spec = pl.BlockSpec((rows, LANES), lambda i, j: (0, 0))

    def act_spec(w):
        return pl.BlockSpec((bb, tt, w), lambda i, j: (i, j, 0))

    def out(w, dt):
        return jax.ShapeDtypeStruct((b, t, w), dt)

    out_specs = [act_spec(4 * A_WIDTH), act_spec(B_WIDTH), act_spec(KV_WIDTH),
                 act_spec(KV_WIDTH), act_spec(IDX_WIDTH), act_spec(IDX_DIM),
                 act_spec(IDX_HEADS)]
    out_shape = [out(4 * A_WIDTH, F32), out(B_WIDTH, BF16), out(KV_WIDTH, F32),
                 out(KV_WIDTH, F32), out(IDX_WIDTH, BF16), out(IDX_DIM, F32),
                 out(IDX_HEADS, F32)]
    if dsa_layouts:
        assert bb == 1
        out_specs += [act_spec(KV_WIDTH), pl.BlockSpec((1, KV_WIDTH, tt), lambda i, j: (i, 0, j)),
                      act_spec(IDX_DIM)]
        out_shape += [out(KV_WIDTH, BF16), jax.ShapeDtypeStruct((b, KV_WIDTH, t), BF16),
                      out(IDX_DIM, BF16)]

    return pl.pallas_call(
        functools.partial(_inproj_kernel, bb=bb, tt=tt),
        grid=(b // bb, t // tt),
        in_specs=[act_spec(d),
                  pl.BlockSpec((bb, 1, 6 * d), lambda i, j: (i, 0, 0)),
                  pl.BlockSpec((1, 1, d), lambda i, j: (0, 0, 0)),
                  pl.BlockSpec((d, IN_WIDTH_PAD), lambda i, j: (0, 0)),
                  tab_spec, tab_spec, tab_spec],
        out_specs=out_specs,
        out_shape=out_shape,
        compiler_params=pltpu.CompilerParams(
            dimension_semantics=("parallel", "parallel"),
            vmem_limit_bytes=VMEM_LIMIT_BYTES),
        name="inproj",
    )(x, mod.reshape(b, 1, 6 * d), norm1.reshape(1, 1, d), w_in_bf, cos, slo, shi)


def _hgrn_tables(cc):
    nlev = int(np.log2(cc))
    t = np.arange(cc)[:, None]
    u = np.arange(cc)[None, :]
    mats = []
    for l in range(nlev):
        m = cc >> (l + 1)
        ref = (t // (2 * m)) * (2 * m) + m - 1
        qside = ((t // m) % 2) == 1
        mats.append(np.where(qside, (u > ref) & (u <= t), (u > t) & (u <= ref)))
    mats.append(u <= t)
    mats.append(u > t)
    w = np.concatenate(mats, axis=0).astype(np.float32)
    w = -np.concatenate([w, w], axis=1)
    lvl = np.full((cc, cc), -1, np.int32)
    for l in range(nlev):
        m = cc >> (l + 1)
        same_parent = (t // (2 * m)) == (u // (2 * m))
        lvl[same_parent & ((t // m) % 2 == 1) & ((u // m) % 2 == 0)] = l
    lvl[np.arange(cc), np.arange(cc)] = nlev
    return jnp.asarray(w, BF16), jnp.asarray(lvl), nlev


def _hgrn_kernel(q_ref, f_ref, i_ref, g_ref, lb_ref, gn_ref, s0_ref, w_ref, lvl_ref,
                 o_ref, s_out_ref, st_scr, *, cc, nc, nlev):
    ci = pl.program_id(1)

    @pl.when(ci == 0)
    def _():
        for h in range(A_HEADS):
            st_scr[h] = s0_ref[0, h].T

    row = lax.broadcasted_iota(jnp.int32, (cc, 2 * A_DK), 0)
    qsides = [((row // (cc >> (l + 1))) % 2) == 1 for l in range(nlev)]
    qsides16 = [jnp.where(m, 1.0, 0.0).astype(BF16) > 0 for m in qsides]
    lvl = jnp.concatenate([lvl_ref[...]] * 2, axis=1)
    lvl_masks = [lvl == l for l in range(nlev + 1)]
    w = w_ref[...]
    lb = lb_ref[...]
    f_all = lb + (1.0 - lb) * jax.nn.sigmoid(f_ref[0])
    nl = -jnp.log2(f_all)
    nl_hi = nl.astype(BF16)
    nl_lo = (nl - nl_hi.astype(F32)).astype(BF16)
    for c in range(nc):
        rs = slice(c * cc, (c + 1) * cc)
        dd_all = _dot(w, jnp.concatenate([nl_hi[rs], nl_lo[rs]], axis=0))
        for hp in range(A_HEADS // 2):
            h0, h1 = 2 * hp, 2 * hp + 1
            sl = slice(h0 * A_DK, (h1 + 1) * A_DK)
            q = _silu(q_ref[0, rs, sl])
            kk = 1.0 - f_all[rs, sl]
            v = i_ref[0, rs, sl]
            dd = dd_all[:, sl]
            q16, k16, v16 = q.astype(BF16), kk.astype(BF16), v.astype(BF16)
            attn = jnp.zeros((cc, 2 * cc), F32)
            for l in range(nlev):
                m = cc >> (l + 1)
                e16 = jnp.exp2(dd[l * cc:(l + 1) * cc]).astype(BF16)
                if m % BF16_ROWS == 0:
                    qk = jnp.concatenate([(q16 if (r // m) % 2 else k16)[r:r + m]
                                          for r in range(0, cc, m)], axis=0)
                else:
                    qk = jnp.where(qsides16[l], q16, k16)
                xl = qk * e16
                attn = jnp.where(lvl_masks[l], _dot_nt(xl, _block_diag(xl, A_DK)), attn)
            attn = jnp.where(lvl_masks[nlev], _dot_nt(q16, _block_diag(k16, A_DK)), attn)
            bcum = dd[nlev * cc:(nlev + 1) * cc]
            brev = dd[(nlev + 1) * cc:(nlev + 2) * cc]
            qg = (q * jnp.exp2(bcum)).astype(BF16)
            kg = (kk * jnp.exp2(brev)).astype(BF16)
            st = jnp.concatenate([st_scr[h0], st_scr[h1]], axis=1)
            o = (_dot(attn.astype(BF16), _block_diag(v16, A_DV))
                 + _dot_nt(qg, _block_diag(st.astype(BF16), A_DK)))
            dec = jnp.exp2(bcum[cc - 1:cc, :])
            upd = _dot(v.T.astype(BF16), kg)
            st_scr[h0] = st[:, :A_DK] * dec[:, :A_DK] + upd[:A_DV, :A_DK]
            st_scr[h1] = st[:, A_DK:] * dec[:, A_DK:] + upd[A_DV:, A_DK:]
            for i, h in enumerate((h0, h1)):
                hs = slice(h * A_DV, (h + 1) * A_DV)
                y = _rms(o[:, i * A_DV:(i + 1) * A_DV]) * gn_ref[:, hs]
                o_ref[0, rs, hs] = (y * _silu(g_ref[0, rs, hs])).astype(BF16)

    @pl.when(ci == pl.num_programs(1) - 1)
    def _():
        for h in range(A_HEADS):
            s_out_ref[0, h] = st_scr[h].T


def _hgrn_call(hg, lb, g_norm, s0, cc):
    b, t, _ = hg.shape
    w, lvl, nlev = _hgrn_tables(cc)
    nc = HGRN_CHUNKS_PER_STEP if t % (HGRN_CHUNKS_PER_STEP * cc) == 0 else 1
    rows = nc * cc

    def part(p):
        return pl.BlockSpec((1, rows, A_WIDTH), lambda i, j, p=p: (i, j, p))

    return pl.pallas_call(
        functools.partial(_hgrn_kernel, cc=cc, nc=nc, nlev=nlev),
        grid=(b, t // rows),
        in_specs=[part(0), part(1), part(2), part(3),
                  pl.BlockSpec((1, A_WIDTH), lambda i, j: (0, 0)),
                  pl.BlockSpec((1, A_WIDTH), lambda i, j: (0, 0)),
                  pl.BlockSpec((1, A_HEADS, A_DK, A_DV), lambda i, j: (i, 0, 0, 0)),
                  pl.BlockSpec(w.shape, lambda i, j: (0, 0)),
                  pl.BlockSpec(lvl.shape, lambda i, j: (0, 0))],
        out_specs=[pl.BlockSpec((1, rows, A_WIDTH), lambda i, j: (i, j, 0)),
                   pl.BlockSpec((1, A_HEADS, A_DK, A_DV), lambda i, j: (i, 0, 0, 0))],
        out_shape=[jax.ShapeDtypeStruct((b, t, A_WIDTH), BF16),
                   jax.ShapeDtypeStruct((b, A_HEADS, A_DK, A_DV), F32)],
        scratch_shapes=[pltpu.VMEM((A_HEADS, A_DV, A_DK), F32)],
        compiler_params=pltpu.CompilerParams(
            dimension_semantics=("parallel", "arbitrary"),
            vmem_limit_bytes=VMEM_LIMIT_BYTES),
        name="hgrn",
    )(hg, hg, hg, hg, lb.reshape(1, A_WIDTH), g_norm.reshape(1, A_WIDTH), s0, w, lvl)


def _fold8(x, op):
    parts = [x[r:r + SUBLANES] for r in range(0, x.shape[0], SUBLANES)]
    while len(parts) > 1:
        parts = [op(parts[i], parts[i + 1]) for i in range(0, len(parts) - 1, 2)] + (
            [parts[-1]] if len(parts) % 2 else [])
    return parts[0]


def _grouped_rhs(xs, heads, width, slot, slots):
    groups = len(xs)
    zero = jnp.zeros((xs[0].shape[0], width), xs[0].dtype)
    rows = []
    for h in heads:
        for g, x in enumerate(xs):
            parts = [zero] * (groups * slots)
            parts[g * slots + slot] = x[:, h * width:(h + 1) * width]
            rows.append(jnp.concatenate(parts, axis=1) if len(parts) > 1 else parts[0])
    return jnp.concatenate(rows, axis=0)


def _dsa_kernel(qi_ref, wit_ref, qb_ref, ki_ref, k_ref, vt_ref, tri_ref, o_ref,
                score_scr, logit_scr, acc_scr, *, groups, gq, qreal, tk, topk, pos0):
    qb = groups * gq
    blk = pl.program_id(1)
    last_pos = pos0 + (blk + 1) * qreal - 1
    extent = jnp.minimum((last_pos // CHUNK + 1) * CHUNK, tk)
    ntile = (extent + KEY_TILE - 1) // KEY_TILE
    lane = lax.broadcasted_iota(jnp.int32, (1, qb), 1)
    qpos = pos0 + blk * qreal + lane % gq
    key_end = jnp.minimum((qpos // CHUNK + 1) * CHUNK, tk)
    neg_inf = jnp.float32(-jnp.inf)

    def lane_cat(parts):
        return jnp.concatenate(parts, axis=1) if len(parts) > 1 else parts[0]

    qis = [qi_ref[g] for g in range(groups)]
    wit = lane_cat([wit_ref[g] for g in range(groups)])
    qi_pairs = [_grouped_rhs(qis, (2 * p, 2 * p + 1), IDX_DIM, 0, 1)
                for p in range(IDX_HEADS // 2)]

    def tile_start(j):
        return pl.multiple_of(j * KEY_TILE, KEY_TILE)

    def tile_loop(body, init):
        def run(first, count, c):
            for u in range(count):
                c = body(first + u, c)
            return c
        carry = lax.fori_loop(0, ntile // 4, lambda i, c: run(4 * i, 4, c), init)
        done = (ntile // 4) * 4
        carry = lax.cond((ntile & 2) != 0, lambda c: run(done, 2, c), lambda c: c, carry)
        return lax.cond((ntile & 1) != 0, lambda c: body(ntile - 1, c), lambda c: c, carry)

    def score_tile(j, carry):
        k0 = tile_start(j)
        ki_t = lane_cat([ki_ref[g, pl.ds(k0, KEY_TILE), :] for g in range(groups)])
        acc = jnp.zeros((KEY_TILE, qb), F32)
        for p in range(IDX_HEADS // 2):
            s2 = jnp.maximum(_dot_nt(ki_t, qi_pairs[p]), 0.0)
            acc = acc + wit[2 * p:2 * p + 1, :] * s2[:, :qb]
            acc = acc + wit[2 * p + 1:2 * p + 2, :] * s2[:, qb:]
        kidx = k0 + lax.broadcasted_iota(jnp.int32, (KEY_TILE, qb), 0)
        score_scr[pl.ds(k0, KEY_TILE), :] = jnp.where(kidx < key_end, acc, neg_inf)
        return carry

    tile_loop(score_tile, 0)

    def count(pred_fn):
        def body(j, acc):
            s = score_scr[pl.ds(tile_start(j), KEY_TILE), :]
            return acc + _fold8(jnp.where(pred_fn(s), 1.0, 0.0), jnp.add)
        acc = tile_loop(body, jnp.zeros((SUBLANES, qb), F32))
        return jnp.sum(acc, axis=0, keepdims=True)

    def decode(t_u):
        key = t_u ^ INT_MIN
        return pltpu.bitcast(key ^ ((key >> 31) & np.int32(0x7FFFFFFF)), F32)

    def bisect(i, t_u):
        cand_u = t_u | (jnp.int32(1) << (31 - i))
        cand = decode(cand_u)
        cnt = count(lambda s: s >= cand)
        return jnp.where(cnt >= topk, cand_u, t_u)

    thr = decode(lax.fori_loop(0, 32, bisect, jnp.zeros((1, qb), jnp.int32)))
    thr = jnp.where(thr != thr, neg_inf, thr)
    need = topk - count(lambda s: s > thr)
    need = jnp.where(thr == neg_inf, 0.0, need)

    qqs = [qb_ref[g] for g in range(groups)]
    per_q = B_HEADS // B_KV_HEADS
    per_block = LANES // B_HD
    q_pairs = [_grouped_rhs(qqs, (n * per_q, n * per_q + 1), B_HD, n % per_block, per_block)
               for n in range(B_KV_HEADS)]
    tri = tri_ref[...]

    def logit_tile(j, carry):
        offs, mx = carry
        k0 = tile_start(j)
        s = score_scr[pl.ds(k0, KEY_TILE), :]
        tie = jnp.where(s == thr, 1.0, 0.0)
        rank = _dot(tri, tie.astype(BF16)) + offs
        picked = jnp.where(s > thr, 1.0, jnp.where(rank < need, tie, 0.0))
        bias = jnp.where(picked > 0.0, 0.0, NEG_BIG)
        bias2 = jnp.concatenate([bias, bias], axis=1)
        new_mx = []
        for n in range(B_KV_HEADS):
            kblk = (n // per_block) * LANES
            k_t = lane_cat([k_ref[g, pl.ds(k0, KEY_TILE), kblk:kblk + LANES]
                            for g in range(groups)])
            lg = _dot_nt(k_t, q_pairs[n]) + bias2
            logit_scr[n, pl.ds(k0, KEY_TILE), :] = lg
            new_mx.append(jnp.maximum(mx[n], _fold8(lg, jnp.maximum)))
        offs = offs + jnp.sum(_fold8(tie, jnp.add), axis=0, keepdims=True)
        return offs, tuple(new_mx)

    mx0 = tuple(jnp.full((SUBLANES, 2 * qb), NEG_BIG, F32) for _ in range(B_KV_HEADS))
    _, mx = tile_loop(logit_tile, (jnp.zeros((1, qb), F32), mx0))
    mx = [jnp.max(m, axis=0, keepdims=True) for m in mx]

    acc_scr[...] = jnp.zeros_like(acc_scr)

    def pv_tile(j, den):
        k0 = tile_start(j)
        new_den = []
        for n in range(B_KV_HEADS):
            p = jnp.exp2(logit_scr[n, pl.ds(k0, KEY_TILE), :] - mx[n])
            new_den.append(den[n] + _fold8(p, jnp.add))
            vts = [vt_ref[g, n * B_HD:(n + 1) * B_HD, pl.ds(k0, KEY_TILE)] for g in range(groups)]
            vt = jnp.concatenate(vts, axis=0) if groups > 1 else vts[0]
            acc_scr[n] += _dot(vt, p.astype(BF16))
        return tuple(new_den)

    den0 = tuple(jnp.zeros((SUBLANES, 2 * qb), F32) for _ in range(B_KV_HEADS))
    den = tile_loop(pv_tile, den0)
    lane_group = (lax.broadcasted_iota(jnp.int32, (1, 2 * qb), 1) % qb) // gq
    rows = []
    for n in range(B_KV_HEADS):
        acc = acc_scr[n]
        o2 = acc[0:B_HD]
        for g in range(1, groups):
            o2 = jnp.where(lane_group == g, acc[g * B_HD:(g + 1) * B_HD], o2)
        o2 = o2 / jnp.sum(den[n], axis=0, keepdims=True)
        rows += [o2[:, :qb], o2[:, qb:]]
    o_ref[...] = jnp.concatenate(rows, axis=0).T.astype(BF16).reshape(groups, gq, B_WIDTH)


def _dsa_call(qi, wi, qbs, k_bf, vt, ki_bf, tk, pos0):
    b, t, _ = qi.shape
    if t % DSA_QUERIES == 0:
        groups, gq = 1, DSA_QUERIES
    else:
        assert LANES % t == 0 and b % (LANES // t) == 0
        groups, gq = LANES // t, t
    qb = groups * gq
    tkp = ki_bf.shape[1]
    topk = min(TOPK_MAX, tk // 4)
    assert topk <= KEY_TILE
    wit = jnp.swapaxes(wi, 1, 2)
    tri = jnp.asarray(np.tril(np.ones((KEY_TILE, KEY_TILE), np.float32), -1), BF16)

    def q_spec(w):
        return pl.BlockSpec((groups, gq, w), lambda i, j: (i, j, 0))

    def kv_spec(rows, cols):
        return pl.BlockSpec((groups, rows, cols), lambda i, j: (i, 0, 0))

    return pl.pallas_call(
        functools.partial(_dsa_kernel, groups=groups, gq=gq, qreal=gq, tk=tk, topk=topk,
                          pos0=pos0),
        grid=(b // groups, t // gq),
        in_specs=[q_spec(IDX_WIDTH),
                  pl.BlockSpec((groups, IDX_HEADS, gq), lambda i, j: (i, 0, j)),
                  q_spec(B_WIDTH),
                  kv_spec(tkp, IDX_DIM), kv_spec(tkp, KV_WIDTH), kv_spec(KV_WIDTH, tkp),
                  pl.BlockSpec((KEY_TILE, KEY_TILE), lambda i, j: (0, 0))],
        out_specs=q_spec(B_WIDTH),
        out_shape=jax.ShapeDtypeStruct((b, t, B_WIDTH), BF16),
        scratch_shapes=[pltpu.VMEM((tkp, qb), F32),
                        pltpu.VMEM((B_KV_HEADS, tkp, 2 * qb), F32),
                        pltpu.VMEM((B_KV_HEADS, groups * B_HD, 2 * qb), F32)],
        compiler_params=pltpu.CompilerParams(
            dimension_semantics=("parallel", "parallel"),
            vmem_limit_bytes=VMEM_LIMIT_BYTES),
        name="dsa",
    )(qi, wit, qbs, ki_bf, k_bf, vt, tri)


def _out_kernel(x_ref, oa_ref, ob_ref, mod_ref, n2_ref, nf_ref, wo_ref, w1_ref, w2_ref,
                y_ref, *, bb, tt):
    rows = bb * tt
    x = x_ref[...]
    mod = mod_ref[...]
    g1 = mod[:, :, 2 * D_MODEL:3 * D_MODEL]
    sh2 = mod[:, :, 3 * D_MODEL:4 * D_MODEL]
    sc2 = mod[:, :, 4 * D_MODEL:5 * D_MODEL]
    g2 = mod[:, :, 5 * D_MODEL:6 * D_MODEL]
    oa = oa_ref[...].reshape(rows, A_WIDTH)
    ob = ob_ref[...].reshape(rows, B_WIDTH)
    mix = _dot(oa, wo_ref[0:A_WIDTH, :]) + _dot(ob, wo_ref[A_WIDTH:A_WIDTH + B_WIDTH, :])
    x = x + g1 * mix.reshape(bb, tt, D_MODEL)
    h2 = (_rms(x) * n2_ref[...]) * (1.0 + sc2) + sh2
    u = _dot(h2.reshape(rows, D_MODEL).astype(BF16), w1_ref[...])
    r = jnp.square(jnp.maximum(u, 0.0)).astype(BF16)
    x = x + g2 * _dot(r, w2_ref[...]).reshape(bb, tt, D_MODEL)
    y_ref[...] = _rms(x) * nf_ref[...]


def _out_call(x, oa, ob, mod, norm2, norm_f, wo_bf, w1_bf, w2_bf, bb, tt):
    b, t, d = x.shape

    def act_spec(w):
        return pl.BlockSpec((bb, tt, w), lambda i, j: (i, j, 0))

    def const_spec(shape):
        zeros = (0,) * len(shape)
        return pl.BlockSpec(shape, lambda i, j: zeros, pipeline_mode=pl.Buffered(1))

    return pl.pallas_call(
        functools.partial(_out_kernel, bb=bb, tt=tt),
        grid=(b // bb, t // tt),
        in_specs=[act_spec(d), act_spec(A_WIDTH), act_spec(B_WIDTH),
                  pl.BlockSpec((bb, 1, 6 * d), lambda i, j: (i, 0, 0)),
                  const_spec((1, 1, d)), const_spec((1, 1, d)),
                  const_spec(wo_bf.shape), const_spec(w1_bf.shape), const_spec(w2_bf.shape)],
        out_specs=act_spec(d),
        out_shape=jax.ShapeDtypeStruct((b, t, d), F32),
        compiler_params=pltpu.CompilerParams(
            dimension_semantics=("parallel", "parallel"),
            vmem_limit_bytes=VMEM_LIMIT_BYTES),
        name="out",
    )(x, oa, ob, mod.reshape(b, 1, 6 * d), norm2.reshape(1, 1, d), norm_f.reshape(1, 1, d),
      wo_bf, w1_bf, w2_bf)


def _layer(x, mod, pos0, s0, k_past, v_past, ki_past, weights, bb, tt, cc):
    norm1, w_in_bf, lb, g_norm, wo_bf, norm2, w1_bf, w2_bf, norm_f = weights
    b, t, _ = x.shape
    pos = pos0 + jnp.arange(t)
    outs = _inproj_call(x, mod, norm1, w_in_bf, pos, bb, tt, dsa_layouts=k_past is None)
    hg, qbs, k_new, v_new, qi, ki_new, wi = outs[:7]
    oa, s_new = _hgrn_call(hg, lb, g_norm, s0, cc)
    if k_past is None:
        assert t % KEY_TILE == 0
        tk = t
        k_bf, vt, ki_bf = outs[7:]
    else:
        n_past = k_past.shape[1]
        tk = n_past + t
        tkp = -(-tk // KEY_TILE) * KEY_TILE

        def cat(past, new):
            return jnp.pad(jnp.concatenate([past, new], axis=1),
                           ((0, 0), (0, tkp - tk), (0, 0))).astype(BF16)

        k_bf = cat(k_past.reshape(b, n_past, KV_WIDTH), k_new)
        vt = jnp.swapaxes(cat(v_past.reshape(b, n_past, KV_WIDTH), v_new), 1, 2)
        ki_bf = cat(ki_past, ki_new)
    ob = _dsa_call(qi, wi, qbs, k_bf, vt, ki_bf, tk, pos0)
    y = _out_call(x, oa, ob, mod, norm2, norm_f, wo_bf, w1_bf, w2_bf, bb, tt)
    return (y, k_new.reshape(b, t, B_KV_HEADS, B_HD), v_new.reshape(b, t, B_KV_HEADS, B_HD),
            ki_new, s_new)


def kernel(x_prompt, x_sample, cache_k, cache_v, cache_k_idx, state_hgrn, c_prompt, c_sample,
           w_mod, b_mod, norm1, w_in, lb_logits, g_norm_a, w_out, norm2, w_ff1, w_ff2, norm_f):
    depth = w_in.shape[0]
    assert depth == 1, "kernel is written for the single-layer configuration"
    lb_all = jnp.cumsum(jax.nn.softmax(lb_logits.astype(F32), axis=0), axis=0)
    bp, tp, _ = x_prompt.shape
    bs, ts, _ = x_sample.shape
    past = cache_k.shape[2]
    l = 0
    mod = _mod_call(jnp.concatenate([c_prompt, c_sample], axis=0), w_mod[l], b_mod[l])
    w_in_bf = jnp.pad(w_in[l], ((0, 0), (0, IN_WIDTH_PAD - IN_WIDTH))).astype(BF16)
    weights = (norm1[l], w_in_bf, lb_all[l], g_norm_a[l], w_out[l].astype(BF16), norm2[l],
               w_ff1[l].astype(BF16), w_ff2[l].astype(BF16), norm_f)
    s0 = jnp.zeros((bp, A_HEADS, A_DK, A_DV), F32)
    yp, kp, vp, kip, sp = _layer(x_prompt, mod[:bp], 0, s0, None, None, None, weights,
                                 bb=1, tt=min(512, tp), cc=min(128, tp))
    ys, ks, vs, kis, ss = _layer(x_sample, mod[bp:], past, state_hgrn[l], cache_k[l],
                                 cache_v[l], cache_k_idx[l], weights,
                                 bb=bs, tt=ts, cc=min(128, ts))
    return (yp, ys, kp[None], vp[None], kip[None], sp[None],
            ks[None], vs[None], kis[None], ss[None])
```

```python
import functools

import numpy as np
import jax
import jax.numpy as jnp
from jax import lax
from jax.experimental import pallas as pl
from jax.experimental.pallas import tpu as pltpu

D_MODEL = 1024
CHUNK = 64
A_HEADS = 4
A_DK = 128
A_DV = 128
A_WIDTH = A_HEADS * A_DV
B_HEADS = 8
B_KV_HEADS = 4
B_HD = 64
B_WIDTH = B_HEADS * B_HD
KV_WIDTH = B_KV_HEADS * B_HD
IDX_HEADS = 8
IDX_DIM = 64
IDX_WIDTH = IDX_HEADS * IDX_DIM
TOPK_MAX = 256
QBLOCK = 128
ROT_FRAC = 4
ROPE_THETA = 500000.0
D_FF = 4 * D_MODEL
EPS = 1e-6
IN_WIDTH = 4 * A_WIDTH + B_WIDTH + 2 * KV_WIDTH + IDX_WIDTH + IDX_DIM + IDX_HEADS

LANES = 128
SUBLANES = 8
BF16_ROWS = 16
KEY_TILE = 256
DSA_QUERIES = 256
HGRN_CHUNKS_PER_STEP = 8
IN_WIDTH_PAD = -(-IN_WIDTH // LANES) * LANES
VMEM_LIMIT_BYTES = 56 * 1024 * 1024

F32 = jnp.float32
BF16 = jnp.bfloat16
INT_MIN = np.int32(-2 ** 31)
NEG_INF_PATTERN = np.int32(0x007FFFFF)
NEG_BIG = -1e30
LOG2_E = 1.4426950408889634

OFF_HG = 0
OFF_QB = 4 * A_WIDTH
OFF_KB = OFF_QB + B_WIDTH
OFF_VB = OFF_KB + KV_WIDTH
OFF_QI = OFF_VB + KV_WIDTH
OFF_KI = OFF_QI + IDX_WIDTH
OFF_WI = OFF_KI + IDX_DIM


def _dot(a, b):
    return jnp.dot(a, b, preferred_element_type=F32)


def _dot_nt(a, b):
    return lax.dot_general(a, b, (((1,), (1,)), ((), ())), preferred_element_type=F32)


def _silu(x):
    return x * jax.nn.sigmoid(x)


def _rms(x):
    return x * lax.rsqrt(jnp.mean(jnp.square(x), axis=-1, keepdims=True) + EPS)


def _block_diag(x, width):
    zero = jnp.zeros((x.shape[0], width), x.dtype)
    return jnp.concatenate([jnp.concatenate([x[:, :width], zero], axis=1),
                            jnp.concatenate([zero, x[:, width:]], axis=1)], axis=0)


def _mod_kernel(c_ref, w_ref, b_ref, o_ref):
    a = _silu(c_ref[...])
    w = w_ref[...]
    a_hi = a.astype(BF16)
    a_lo = (a - a_hi.astype(F32)).astype(BF16)
    w_hi = w.astype(BF16)
    w_lo = (w - w_hi.astype(F32)).astype(BF16)
    o_ref[...] = _dot(a_hi, w_hi) + _dot(a_lo, w_hi) + _dot(a_hi, w_lo) + b_ref[...]


def _mod_call(c, w_mod, b_mod):
    rows, d = c.shape
    n = w_mod.shape[1]
    tn = 1024
    return pl.pallas_call(
        _mod_kernel,
        grid=(n // tn,),
        in_specs=[pl.BlockSpec((rows, d), lambda j: (0, 0)),
                  pl.BlockSpec((d, tn), lambda j: (0, j)),
                  pl.BlockSpec((1, tn), lambda j: (0, j))],
        out_specs=pl.BlockSpec((rows, tn), lambda j: (0, j)),
        out_shape=jax.ShapeDtypeStruct((rows, n), F32),
        compiler_params=pltpu.CompilerParams(vmem_limit_bytes=VMEM_LIMIT_BYTES),
        name="mod",
    )(c, w_mod, b_mod.reshape(1, n))


def _rope(x, cos, sin_lo, sin_hi):
    half = B_HD // ROT_FRAC // 2
    return (x * cos + pltpu.roll(x, half, 1) * sin_hi
            + pltpu.roll(x, LANES - half, 1) * sin_lo)


def _inproj_kernel(x_ref, mod_ref, n1_ref, w_ref, cos_ref, slo_ref, shi_ref,
                   hg_ref, qb_ref, k_ref, v_ref, qi_ref, ki_ref, wi_ref, *dsa_refs, bb, tt):
    rows = bb * tt
    x = x_ref[...]
    mod = mod_ref[...]
    sh1 = mod[:, :, 0:D_MODEL]
    sc1 = mod[:, :, D_MODEL:2 * D_MODEL]
    h = (_rms(x) * n1_ref[...]) * (1.0 + sc1) + sh1
    h = h.reshape(rows, D_MODEL).astype(BF16)
    z_att = _dot(h, w_ref[:, OFF_QB:])
    cos, slo, shi = cos_ref[...], slo_ref[...], shi_ref[...]

    def cols(off, width):
        return z_att[:, off - OFF_QB:off - OFF_QB + width]

    def rope_cols(off, width):
        return [_rope(cols(off + j, LANES), cos, slo, shi) for j in range(0, width, LANES)]

    scale = B_HD ** -0.5 * LOG2_E
    qb = jnp.concatenate(rope_cols(OFF_QB, B_WIDTH), axis=1) * scale
    qb_ref[...] = qb.astype(BF16).reshape(bb, tt, B_WIDTH)
    kb = jnp.concatenate(rope_cols(OFF_KB, KV_WIDTH), axis=1)
    k_ref[...] = kb.reshape(bb, tt, KV_WIDTH)
    vb = cols(OFF_VB, KV_WIDTH)
    v_ref[...] = vb.reshape(bb, tt, KV_WIDTH)
    qi = jnp.concatenate(rope_cols(OFF_QI, IDX_WIDTH), axis=1)
    qi_ref[...] = qi.astype(BF16).reshape(bb, tt, IDX_WIDTH)
    last = _rope(cols(OFF_KI, LANES), cos, slo, shi)
    ki_ref[...] = last[:, 0:IDX_DIM].reshape(bb, tt, IDX_DIM)
    wi = cols(OFF_WI, IDX_HEADS) * (IDX_WIDTH ** -0.5)
    wi_ref[...] = wi.reshape(bb, tt, IDX_HEADS)
    if dsa_refs:
        kbf_ref, vt_ref, kibf_ref = dsa_refs
        kbf_ref[0] = kb.astype(BF16)
        vt_ref[0] = vb.T.astype(BF16)
        kibf_ref[0] = last[:, 0:IDX_DIM].astype(BF16)
    hg_ref[...] = _dot(h, w_ref[:, OFF_HG:OFF_QB]).reshape(bb, tt, 4 * A_WIDTH)


def _rope_tables(pos, reps):
    rot = B_HD // ROT_FRAC
    half = rot // 2
    inv = jnp.power(ROPE_THETA, -jnp.arange(half, dtype=F32) * (2.0 / rot))
    ang = pos.astype(F32)[:, None] * inv[None, :]
    cos, sin = jnp.cos(ang), jnp.sin(ang)
    t = pos.shape[0]
    ones = jnp.ones((t, B_HD - rot), F32)
    zeros = jnp.zeros((t, B_HD - rot), F32)
    zh = jnp.zeros((t, half), F32)
    cos_h = jnp.concatenate([cos, cos, ones], axis=1)
    slo_h = jnp.concatenate([-sin, zh, zeros], axis=1)
    shi_h = jnp.concatenate([zh, sin, zeros], axis=1)
    per = LANES // B_HD
    return tuple(jnp.tile(a, (reps, per)) for a in (cos_h, slo_h, shi_h))


def _inproj_call(x, mod, norm1, w_in_bf, pos, bb, tt, dsa_layouts):
    b, t, d = x.shape
    cos, slo, shi = _rope_tables(pos, bb)
    rows = bb * tt
    if bb == 1:
        tab_spec = pl.BlockSpec((tt, LANES), lambda i, j: (j, 0))
    else:
        tab_spec = pl.BlockSpec((rows, LANES), lambda i, j: (0, 0))

    def act_spec(w):
        return pl.BlockSpec((bb, tt, w), lambda i, j: (i, j, 0))

    def out(w, dt):
        return jax.ShapeDtypeStruct((b, t, w), dt)

    out_specs = [act_spec(4 * A_WIDTH), act_spec(B_WIDTH), act_spec(KV_WIDTH),
                 act_spec(KV_WIDTH), act_spec(IDX_WIDTH), act_spec(IDX_DIM),
                 act_spec(IDX_HEADS)]
    out_shape = [out(4 * A_WIDTH, F32), out(B_WIDTH, BF16), out(KV_WIDTH, F32),
                 out(KV_WIDTH, F32), out(IDX_WIDTH, BF16), out(IDX_DIM, F32),
                 out(IDX_HEADS, F32)]
    if dsa_layouts:
        assert bb == 1
        out_specs += [act_spec(KV_WIDTH), pl.BlockSpec((1, KV_WIDTH, tt), lambda i, j: (i, 0, j)),
                      act_spec(IDX_DIM)]
        out_shape += [out(KV_WIDTH, BF16), jax.ShapeDtypeStruct((b, KV_WIDTH, t), BF16),
                      out(IDX_DIM, BF16)]

    return pl.pallas_call(
        functools.partial(_inproj_kernel, bb=bb, tt=tt),
        grid=(b // bb, t // tt),
        in_specs=[act_spec(d),
                  pl.BlockSpec((bb, 1, 6 * d), lambda i, j: (i, 0, 0)),
                  pl.BlockSpec((1, 1, d), lambda i, j: (0, 0, 0)),
                  pl.BlockSpec((d, IN_WIDTH_PAD), lambda i, j: (0, 0)),
                  tab_spec, tab_spec, tab_spec],
        out_specs=out_specs,
        out_shape=out_shape,
        compiler_params=pltpu.CompilerParams(
            dimension_semantics=("parallel", "parallel"),
            vmem_limit_bytes=VMEM_LIMIT_BYTES),
        name="inproj",
    )(x, mod.reshape(b, 1, 6 * d), norm1.reshape(1, 1, d), w_in_bf, cos, slo, shi)


def _hgrn_tables(cc):
    nlev = int(np.log2(cc))
    t = np.arange(cc)[:, None]
    u = np.arange(cc)[None, :]
    mats = [(u <= t).astype(np.float32)]
    for l in range(nlev):
        m = cc >> (l + 1)
        if m < SUBLANES:
            ref = (t // (2 * m)) * (2 * m) + m - 1
            qside = ((t // m) % 2) == 1
            seg = np.where(qside, (u > ref) & (u <= t), (u > t) & (u <= ref))
            mats.append(-seg.astype(np.float32))
    w = np.concatenate(mats, axis=0)
    w = np.concatenate([w] * 3, axis=1)
    lvl = np.full((cc, cc), -1, np.int32)
    for l in range(nlev):
        m = cc >> (l + 1)
        same_parent = (t // (2 * m)) == (u // (2 * m))
        lvl[same_parent & ((t // m) % 2 == 1) & ((u // m) % 2 == 0)] = l
    lvl[np.arange(cc), np.arange(cc)] = nlev
    return jnp.asarray(w, BF16), jnp.asarray(lvl), nlev


def _level_exponent(cum_ref, cols, cum, m):
    cc, n = cum.shape
    parts = []
    for p in range(0, cc, 2 * m):
        ref = jnp.broadcast_to(cum_ref[p + m - 1:p + m, cols], (m, n))
        parts += [cum[p:p + m] - ref, ref - cum[p + m:p + 2 * m]]
    return jnp.concatenate(parts, axis=0)


def _hgrn_kernel(q_ref, f_ref, i_ref, g_ref, lb_ref, gn_ref, s0_ref, w_ref, lvl_ref,
                 o_ref, s_out_ref, st_scr, cum_scr, *, cc, nc, nlev):
    ci = pl.program_id(1)

    @pl.when(ci == 0)
    def _():
        for h in range(A_HEADS):
            st_scr[h] = s0_ref[0, h].T

    row = lax.broadcasted_iota(jnp.int32, (cc, 2 * A_DK), 0)
    qsides = [((row // (cc >> (l + 1))) % 2) == 1 for l in range(nlev)]
    n_small = sum(1 for l in range(nlev) if (cc >> (l + 1)) < SUBLANES)
    qsides16 = [jnp.where(m, 1.0, 0.0).astype(BF16) > 0 for m in qsides]
    lvl = jnp.concatenate([lvl_ref[...]] * 2, axis=1)
    lvl_masks = [lvl == l for l in range(nlev + 1)]
    w = w_ref[...]
    lb = lb_ref[...]
    f_all = lb + (1.0 - lb) * jax.nn.sigmoid(f_ref[0])
    nl = -jnp.log2(f_all)
    nl_hi = nl.astype(BF16)
    rest = nl - nl_hi.astype(F32)
    nl_mid = rest.astype(BF16)
    nl_lo = (rest - nl_mid.astype(F32)).astype(BF16)
    for c in range(nc):
        rs = slice(c * cc, (c + 1) * cc)
        cum_scr[...] = _dot(w[0:cc], jnp.concatenate([nl_hi[rs], nl_mid[rs], nl_lo[rs]], axis=0))
        sums = _dot(w[cc:, 0:2 * cc], jnp.concatenate([nl_hi[rs], nl_mid[rs]], axis=0))
        for hp in range(A_HEADS // 2):
            h0, h1 = 2 * hp, 2 * hp + 1
            sl = slice(h0 * A_DK, (h1 + 1) * A_DK)
            q = _silu(q_ref[0, rs, sl])
            kk = 1.0 - f_all[rs, sl]
            v = i_ref[0, rs, sl]
            cum = cum_scr[:, sl]
            q16, k16, v16 = q.astype(BF16), kk.astype(BF16), v.astype(BF16)
            attn = jnp.zeros((cc, 2 * cc), F32)
            for l in range(nlev):
                m = cc >> (l + 1)
                if m >= SUBLANES:
                    ex = _level_exponent(cum_scr, sl, cum, m)
                else:
                    small = l - (nlev - n_small)
                    ex = sums[small * cc:(small + 1) * cc, sl]
                e16 = jnp.exp2(ex).astype(BF16)
                if m % BF16_ROWS == 0:
                    qk = jnp.concatenate([(q16 if (r // m) % 2 else k16)[r:r + m]
                                          for r in range(0, cc, m)], axis=0)
                else:
                    qk = jnp.where(qsides16[l], q16, k16)
                xl = qk * e16
                attn = jnp.where(lvl_masks[l], _dot_nt(xl, _block_diag(xl, A_DK)), attn)
            attn = jnp.where(lvl_masks[nlev], _dot_nt(q16, _block_diag(k16, A_DK)), attn)
            last = jnp.broadcast_to(cum_scr[cc - 1:cc, sl], (cc, 2 * A_DK))
            qg = (q * jnp.exp2(-cum)).astype(BF16)
            kg = (kk * jnp.exp2(cum - last)).astype(BF16)
            st = jnp.concatenate([st_scr[h0], st_scr[h1]], axis=1)
            o = (_dot(attn.astype(BF16), _block_diag(v16, A_DV))
                 + _dot_nt(qg, _block_diag(st.astype(BF16), A_DK)))
            dec = jnp.exp2(-cum[cc - 1:cc, :])
            upd = _dot(v.T.astype(BF16), kg)
            st_scr[h0] = st[:, :A_DK] * dec[:, :A_DK] + upd[:A_DV, :A_DK]
            st_scr[h1] = st[:, A_DK:] * dec[:, A_DK:] + upd[A_DV:, A_DK:]
            for i, h in enumerate((h0, h1)):
                hs = slice(h * A_DV, (h + 1) * A_DV)
                y = _rms(o[:, i * A_DV:(i + 1) * A_DV]) * gn_ref[:, hs]
                o_ref[0, rs, hs] = (y * _silu(g_ref[0, rs, hs])).astype(BF16)

    @pl.when(ci == pl.num_programs(1) - 1)
    def _():
        for h in range(A_HEADS):
            s_out_ref[0, h] = st_scr[h].T


def _hgrn_call(hg, lb, g_norm, s0, cc):
    b, t, _ = hg.shape
    w, lvl, nlev = _hgrn_tables(cc)
    nc = HGRN_CHUNKS_PER_STEP if t % (HGRN_CHUNKS_PER_STEP * cc) == 0 else 1
    rows = nc * cc

    def part(p):
        return pl.BlockSpec((1, rows, A_WIDTH), lambda i, j, p=p: (i, j, p))

    return pl.pallas_call(
        functools.partial(_hgrn_kernel, cc=cc, nc=nc, nlev=nlev),
        grid=(b, t // rows),
        in_specs=[part(0), part(1), part(2), part(3),
                  pl.BlockSpec((1, A_WIDTH), lambda i, j: (0, 0)),
                  pl.BlockSpec((1, A_WIDTH), lambda i, j: (0, 0)),
                  pl.BlockSpec((1, A_HEADS, A_DK, A_DV), lambda i, j: (i, 0, 0, 0)),
                  pl.BlockSpec(w.shape, lambda i, j: (0, 0)),
                  pl.BlockSpec(lvl.shape, lambda i, j: (0, 0))],
        out_specs=[pl.BlockSpec((1, rows, A_WIDTH), lambda i, j: (i, j, 0)),
                   pl.BlockSpec((1, A_HEADS, A_DK, A_DV), lambda i, j: (i, 0, 0, 0))],
        out_shape=[jax.ShapeDtypeStruct((b, t, A_WIDTH), BF16),
                   jax.ShapeDtypeStruct((b, A_HEADS, A_DK, A_DV), F32)],
        scratch_shapes=[pltpu.VMEM((A_HEADS, A_DV, A_DK), F32),
                        pltpu.VMEM((cc, A_WIDTH), F32)],
        compiler_params=pltpu.CompilerParams(
            dimension_semantics=("parallel", "arbitrary"),
            vmem_limit_bytes=VMEM_LIMIT_BYTES),
        name="hgrn",
    )(hg, hg, hg, hg, lb.reshape(1, A_WIDTH), g_norm.reshape(1, A_WIDTH), s0, w, lvl)


def _fold8(x, op):
    parts = [x[r:r + SUBLANES] for r in range(0, x.shape[0], SUBLANES)]
    while len(parts) > 1:
        parts = [op(parts[i], parts[i + 1]) for i in range(0, len(parts) - 1, 2)] + (
            [parts[-1]] if len(parts) % 2 else [])
    return parts[0]


def _grouped_rhs(xs, heads, width, slot, slots):
    groups = len(xs)
    zero = jnp.zeros((xs[0].shape[0], width), xs[0].dtype)
    rows = []
    for h in heads:
        for g, x in enumerate(xs):
            parts = [zero] * (groups * slots)
            parts[g * slots + slot] = x[:, h * width:(h + 1) * width]
            rows.append(jnp.concatenate(parts, axis=1) if len(parts) > 1 else parts[0])
    return jnp.concatenate(rows, axis=0)


def _dsa_kernel(qi_ref, wit_ref, qb_ref, ki_ref, k_ref, vt_ref, tri_ref, o_ref,
                score_scr, logit_scr, acc_scr, *, groups, gq, qreal, tk, topk, pos0):
    qb = groups * gq
    blk = pl.program_id(1)
    last_pos = pos0 + (blk + 1) * qreal - 1
    extent = jnp.minimum((last_pos // CHUNK + 1) * CHUNK, tk)
    ntile = (extent + KEY_TILE - 1) // KEY_TILE
    lane = lax.broadcasted_iota(jnp.int32, (1, qb), 1)
    qpos = pos0 + blk * qreal + lane % gq
    key_end = jnp.minimum((qpos // CHUNK + 1) * CHUNK, tk)
    neg_inf = jnp.float32(-jnp.inf)

    def lane_cat(parts):
        return jnp.concatenate(parts, axis=1) if len(parts) > 1 else parts[0]

    qis = [qi_ref[g] for g in range(groups)]
    wit = lane_cat([wit_ref[g] for g in range(groups)])
    qi_pairs = [_grouped_rhs(qis, (2 * p, 2 * p + 1), IDX_DIM, 0, 1)
                for p in range(IDX_HEADS // 2)]

    def tile_start(j):
        return pl.multiple_of(j * KEY_TILE, KEY_TILE)

    def tile_loop(body, init):
        def run(first, count, c):
            for u in range(count):
                c = body(first + u, c)
            return c
        carry = lax.fori_loop(0, ntile // 4, lambda i, c: run(4 * i, 4, c), init)
        done = (ntile // 4) * 4
        carry = lax.cond((ntile & 2) != 0, lambda c: run(done, 2, c), lambda c: c, carry)
        return lax.cond((ntile & 1) != 0, lambda c: body(ntile - 1, c), lambda c: c, carry)

    def score_tile(j, carry):
        k0 = tile_start(j)
        ki_t = lane_cat([ki_ref[g, pl.ds(k0, KEY_TILE), :] for g in range(groups)])
        acc = jnp.zeros((KEY_TILE, qb), F32)
        for p in range(IDX_HEADS // 2):
            s2 = jnp.maximum(_dot_nt(ki_t, qi_pairs[p]), 0.0)
            acc = acc + wit[2 * p:2 * p + 1, :] * s2[:, :qb]
            acc = acc + wit[2 * p + 1:2 * p + 2, :] * s2[:, qb:]
        kidx = k0 + lax.broadcasted_iota(jnp.int32, (KEY_TILE, qb), 0)
        score_scr[pl.ds(k0, KEY_TILE), :] = jnp.where(kidx < key_end, acc, neg_inf)
        return carry

    tile_loop(score_tile, 0)

    def count(pred_fn):
        def body(j, acc):
            s = score_scr[pl.ds(tile_start(j), KEY_TILE), :]
            return acc + _fold8(jnp.where(pred_fn(s), 1.0, 0.0), jnp.add)
        acc = tile_loop(body, jnp.zeros((SUBLANES, qb), F32))
        return jnp.sum(acc, axis=0, keepdims=True)

    def decode(t_u):
        key = t_u ^ INT_MIN
        return pltpu.bitcast(key ^ ((key >> 31) & np.int32(0x7FFFFFFF)), F32)

    def bisect(i, t_u):
        cand_u = t_u | (jnp.int32(1) << (31 - i))
        cand = decode(cand_u)
        cnt = count(lambda s: s >= cand)
        return jnp.where(cnt >= topk, cand_u, t_u)

    passes = jnp.where(extent > topk, 32, 0)
    t_u = lax.fori_loop(0, passes, bisect, jnp.zeros((1, qb), jnp.int32))
    t_u = jnp.where((t_u >= 0) & (t_u < NEG_INF_PATTERN), NEG_INF_PATTERN, t_u)
    thr = decode(t_u)
    need = topk - count(lambda s: s > thr)
    need = jnp.where(thr == neg_inf, 0.0, need)

    qqs = [qb_ref[g] for g in range(groups)]
    per_q = B_HEADS // B_KV_HEADS
    per_block = LANES // B_HD
    q_pairs = [_grouped_rhs(qqs, (n * per_q, n * per_q + 1), B_HD, n % per_block, per_block)
               for n in range(B_KV_HEADS)]
    tri = tri_ref[...]

    def logit_tile(j, carry):
        offs, mx = carry
        k0 = tile_start(j)
        s = score_scr[pl.ds(k0, KEY_TILE), :]
        tie = jnp.where(s == thr, 1.0, 0.0)
        rank = _dot(tri, tie.astype(BF16)) + offs
        picked = jnp.where(s > thr, 1.0, jnp.where(rank < need, tie, 0.0))
        bias = jnp.where(picked > 0.0, 0.0, NEG_BIG)
        bias2 = jnp.concatenate([bias, bias], axis=1)
        new_mx = []
        for n in range(B_KV_HEADS):
            kblk = (n // per_block) * LANES
            k_t = lane_cat([k_ref[g, pl.ds(k0, KEY_TILE), kblk:kblk + LANES]
                            for g in range(groups)])
            lg = _dot_nt(k_t, q_pairs[n]) + bias2
            logit_scr[n, pl.ds(k0, KEY_TILE), :] = lg
            new_mx.append(jnp.maximum(mx[n], _fold8(lg, jnp.maximum)))
        offs = offs + jnp.sum(_fold8(tie, jnp.add), axis=0, keepdims=True)
        return offs, tuple(new_mx)

    mx0 = tuple(jnp.full((SUBLANES, 2 * qb), NEG_BIG, F32) for _ in range(B_KV_HEADS))
    _, mx = tile_loop(logit_tile, (jnp.zeros((1, qb), F32), mx0))
    mx = [jnp.max(m, axis=0, keepdims=True) for m in mx]

    acc_scr[...] = jnp.zeros_like(acc_scr)

    def pv_tile(j, den):
        k0 = tile_start(j)
        new_den = []
        for n in range(B_KV_HEADS):
            p = jnp.exp2(logit_scr[n, pl.ds(k0, KEY_TILE), :] - mx[n])
            new_den.append(den[n] + _fold8(p, jnp.add))
            vts = [vt_ref[g, n * B_HD:(n + 1) * B_HD, pl.ds(k0, KEY_TILE)] for g in range(groups)]
            vt = jnp.concatenate(vts, axis=0) if groups > 1 else vts[0]
            acc_scr[n] += _dot(vt, p.astype(BF16))
        return tuple(new_den)

    den0 = tuple(jnp.zeros((SUBLANES, 2 * qb), F32) for _ in range(B_KV_HEADS))
    den = tile_loop(pv_tile, den0)
    lane_group = (lax.broadcasted_iota(jnp.int32, (1, 2 * qb), 1) % qb) // gq
    rows = []
    for n in range(B_KV_HEADS):
        acc = acc_scr[n]
        o2 = acc[0:B_HD]
        for g in range(1, groups):
            o2 = jnp.where(lane_group == g, acc[g * B_HD:(g + 1) * B_HD], o2)
        o2 = o2 / jnp.sum(den[n], axis=0, keepdims=True)
        rows += [o2[:, :qb], o2[:, qb:]]
    o_ref[...] = jnp.concatenate(rows, axis=0).T.astype(BF16).reshape(groups, gq, B_WIDTH)


def _dsa_call(qi, wi, qbs, k_bf, vt, ki_bf, tk, pos0):
    b, t, _ = qi.shape
    if t % DSA_QUERIES == 0:
        groups, gq = 1, DSA_QUERIES
    else:
        assert LANES % t == 0 and b % (LANES // t) == 0
        groups, gq = LANES // t, t
    qb = groups * gq
    tkp = ki_bf.shape[1]
    topk = min(TOPK_MAX, tk // 4)
    assert topk <= KEY_TILE
    wit = jnp.swapaxes(wi, 1, 2)
    tri = jnp.asarray(np.tril(np.ones((KEY_TILE, KEY_TILE), np.float32), -1), BF16)

    def q_spec(w):
        return pl.BlockSpec((groups, gq, w), lambda i, j: (i, j, 0))

    def kv_spec(rows, cols):
        return pl.BlockSpec((groups, rows, cols), lambda i, j: (i, 0, 0))

    return pl.pallas_call(
        functools.partial(_dsa_kernel, groups=groups, gq=gq, qreal=gq, tk=tk, topk=topk,
                          pos0=pos0),
        grid=(b // groups, t // gq),
        in_specs=[q_spec(IDX_WIDTH),
                  pl.BlockSpec((groups, IDX_HEADS, gq), lambda i, j: (i, 0, j)),
                  q_spec(B_WIDTH),
                  kv_spec(tkp, IDX_DIM), kv_spec(tkp, KV_WIDTH), kv_spec(KV_WIDTH, tkp),
                  pl.BlockSpec((KEY_TILE, KEY_TILE), lambda i, j: (0, 0))],
        out_specs=q_spec(B_WIDTH),
        out_shape=jax.ShapeDtypeStruct((b, t, B_WIDTH), BF16),
        scratch_shapes=[pltpu.VMEM((tkp, qb), F32),
                        pltpu.VMEM((B_KV_HEADS, tkp, 2 * qb), F32),
                        pltpu.VMEM((B_KV_HEADS, groups * B_HD, 2 * qb), F32)],
        compiler_params=pltpu.CompilerParams(
            dimension_semantics=("parallel", "parallel"),
            vmem_limit_bytes=VMEM_LIMIT_BYTES),
        name="dsa",
    )(qi, wit, qbs, ki_bf, k_bf, vt, tri)


def _out_kernel(x_ref, oa_ref, ob_ref, mod_ref, n2_ref, nf_ref, wo_ref, w1_ref, w2_ref,
                y_ref, *, bb, tt):
    rows = bb * tt
    x = x_ref[...]
    mod = mod_ref[...]
    g1 = mod[:, :, 2 * D_MODEL:3 * D_MODEL]
    sh2 = mod[:, :, 3 * D_MODEL:4 * D_MODEL]
    sc2 = mod[:, :, 4 * D_MODEL:5 * D_MODEL]
    g2 = mod[:, :, 5 * D_MODEL:6 * D_MODEL]
    oa = oa_ref[...].reshape(rows, A_WIDTH)
    ob = ob_ref[...].reshape(rows, B_WIDTH)
    mix = _dot(oa, wo_ref[0:A_WIDTH, :]) + _dot(ob, wo_ref[A_WIDTH:A_WIDTH + B_WIDTH, :])
    x = x + g1 * mix.reshape(bb, tt, D_MODEL)
    h2 = (_rms(x) * n2_ref[...]) * (1.0 + sc2) + sh2
    u = _dot(h2.reshape(rows, D_MODEL).astype(BF16), w1_ref[...])
    r = jnp.square(jnp.maximum(u, 0.0)).astype(BF16)
    x = x + g2 * _dot(r, w2_ref[...]).reshape(bb, tt, D_MODEL)
    y_ref[...] = _rms(x) * nf_ref[...]


def _out_call(x, oa, ob, mod, norm2, norm_f, wo_bf, w1_bf, w2_bf, bb, tt):
    b, t, d = x.shape

    def act_spec(w):
        return pl.BlockSpec((bb, tt, w), lambda i, j: (i, j, 0))

    def const_spec(shape):
        zeros = (0,) * len(shape)
        return pl.BlockSpec(shape, lambda i, j: zeros, pipeline_mode=pl.Buffered(1))

    return pl.pallas_call(
        functools.partial(_out_kernel, bb=bb, tt=tt),
        grid=(b // bb, t // tt),
        in_specs=[act_spec(d), act_spec(A_WIDTH), act_spec(B_WIDTH),
                  pl.BlockSpec((bb, 1, 6 * d), lambda i, j: (i, 0, 0)),
                  const_spec((1, 1, d)), const_spec((1, 1, d)),
                  const_spec(wo_bf.shape), const_spec(w1_bf.shape), const_spec(w2_bf.shape)],
        out_specs=act_spec(d),
        out_shape=jax.ShapeDtypeStruct((b, t, d), F32),
        compiler_params=pltpu.CompilerParams(
            dimension_semantics=("parallel", "parallel"),
            vmem_limit_bytes=VMEM_LIMIT_BYTES),
        name="out",
    )(x, oa, ob, mod.reshape(b, 1, 6 * d), norm2.reshape(1, 1, d), norm_f.reshape(1, 1, d),
      wo_bf, w1_bf, w2_bf)


def _layer(x, mod, pos0, s0, k_past, v_past, ki_past, weights, bb, tt, cc):
    norm1, w_in_bf, lb, g_norm, wo_bf, norm2, w1_bf, w2_bf, norm_f = weights
    b, t, _ = x.shape
    pos = pos0 + jnp.arange(t)
    outs = _inproj_call(x, mod, norm1, w_in_bf, pos, bb, tt, dsa_layouts=k_past is None)
    hg, qbs, k_new, v_new, qi, ki_new, wi = outs[:7]
    oa, s_new = _hgrn_call(hg, lb, g_norm, s0, cc)
    if k_past is None:
        assert t % KEY_TILE == 0
        tk = t
        k_bf, vt, ki_bf = outs[7:]
    else:
        n_past = k_past.shape[1]
        tk = n_past + t
        tkp = -(-tk // KEY_TILE) * KEY_TILE

        def cat(past, new):
            return jnp.pad(jnp.concatenate([past, new], axis=1),
                           ((0, 0), (0, tkp - tk), (0, 0))).astype(BF16)

        k_bf = cat(k_past.reshape(b, n_past, KV_WIDTH), k_new)
        vt = jnp.swapaxes(cat(v_past.reshape(b, n_past, KV_WIDTH), v_new), 1, 2)
        ki_bf = cat(ki_past, ki_new)
    ob = _dsa_call(qi, wi, qbs, k_bf, vt, ki_bf, tk, pos0)
    y = _out_call(x, oa, ob, mod, norm2, norm_f, wo_bf, w1_bf, w2_bf, bb, tt)
    return (y, k_new.reshape(b, t, B_KV_HEADS, B_HD), v_new.reshape(b, t, B_KV_HEADS, B_HD),
            ki_new, s_new)


def kernel(x_prompt, x_sample, cache_k, cache_v, cache_k_idx, state_hgrn, c_prompt, c_sample,
           w_mod, b_mod, norm1, w_in, lb_logits, g_norm_a, w_out, norm2, w_ff1, w_ff2, norm_f):
    depth = w_in.shape[0]
    assert depth == 1, "kernel is written for the single-layer configuration"
    lb_all = jnp.cumsum(jax.nn.softmax(lb_logits.astype(F32), axis=0), axis=0)
    bp, tp, _ = x_prompt.shape
    bs, ts, _ = x_sample.shape
    past = cache_k.shape[2]
    l = 0
    mod = _mod_call(jnp.concatenate([c_prompt, c_sample], axis=0), w_mod[l], b_mod[l])
    w_in_bf = jnp.pad(w_in[l], ((0, 0), (0, IN_WIDTH_PAD - IN_WIDTH))).astype(BF16)
    weights = (norm1[l], w_in_bf, lb_all[l], g_norm_a[l], w_out[l].astype(BF16), norm2[l],
               w_ff1[l].astype(BF16), w_ff2[l].astype(BF16), norm_f)
    s0 = jnp.zeros((bp, A_HEADS, A_DK, A_DV), F32)
    yp, kp, vp, kip, sp = _layer(x_prompt, mod[:bp], 0, s0, None, None, None, weights,
                                 bb=1, tt=min(512, tp), cc=min(128, tp))
    ys, ks, vs, kis, ss = _layer(x_sample, mod[bp:], past, state_hgrn[l], cache_k[l],
                                 cache_v[l], cache_k_idx[l], weights,
                                 bb=bs, tt=ts, cc=min(128, ts))
    return (yp, ys, kp[None], vp[None], kip[None], sp[None],
            ks[None], vs[None], kis[None], ss[None])
```

```python
import functools

import numpy as np
import jax
import jax.numpy as jnp
from jax import lax
from jax.experimental import pallas as pl
from jax.experimental.pallas import tpu as pltpu

D_MODEL = 1024
CHUNK = 64
A_HEADS = 4
A_DK = 128
A_DV = 128
A_WIDTH = A_HEADS * A_DV
B_HEADS = 8
B_KV_HEADS = 4
B_HD = 64
B_WIDTH = B_HEADS * B_HD
KV_WIDTH = B_KV_HEADS * B_HD
IDX_HEADS = 8
IDX_DIM = 64
IDX_WIDTH = IDX_HEADS * IDX_DIM
TOPK_MAX = 256
QBLOCK = 128
ROT_FRAC = 4
ROPE_THETA = 500000.0
D_FF = 4 * D_MODEL
EPS = 1e-6
IN_WIDTH = 4 * A_WIDTH + B_WIDTH + 2 * KV_WIDTH + IDX_WIDTH + IDX_DIM + IDX_HEADS

LANES = 128
SUBLANES = 8
BF16_ROWS = 16
KEY_TILE = 256
DSA_QUERIES = 256
HGRN_CHUNKS_PER_STEP = 8
IN_WIDTH_PAD = -(-IN_WIDTH // LANES) * LANES
VMEM_LIMIT_BYTES = 56 * 1024 * 1024

F32 = jnp.float32
BF16 = jnp.bfloat16
INT_MIN = np.int32(-2 ** 31)
NEG_INF_PATTERN16 = 0x007F
NEG_BIG = -1e30
LOG2_E = 1.4426950408889634

OFF_HG = 0
OFF_QB = 4 * A_WIDTH
OFF_KB = OFF_QB + B_WIDTH
OFF_VB = OFF_KB + KV_WIDTH
OFF_QI = OFF_VB + KV_WIDTH
OFF_KI = OFF_QI + IDX_WIDTH
OFF_WI = OFF_KI + IDX_DIM


def _dot(a, b):
    return jnp.dot(a, b, preferred_element_type=F32)


def _dot_nt(a, b):
    return lax.dot_general(a, b, (((1,), (1,)), ((), ())), preferred_element_type=F32)


def _silu(x):
    return x * jax.nn.sigmoid(x)


def _rms(x):
    return x * lax.rsqrt(jnp.mean(jnp.square(x), axis=-1, keepdims=True) + EPS)


def _block_diag(x, width):
    zero = jnp.zeros((x.shape[0], width), x.dtype)
    return jnp.concatenate([jnp.concatenate([x[:, :width], zero], axis=1),
                            jnp.concatenate([zero, x[:, width:]], axis=1)], axis=0)


def _mod_kernel(c_ref, w_ref, b_ref, o_ref):
    a = _silu(c_ref[...])
    w = w_ref[...]
    a_hi = a.astype(BF16)
    a_lo = (a - a_hi.astype(F32)).astype(BF16)
    w_hi = w.astype(BF16)
    w_lo = (w - w_hi.astype(F32)).astype(BF16)
    o_ref[...] = _dot(a_hi, w_hi) + _dot(a_lo, w_hi) + _dot(a_hi, w_lo) + b_ref[...]


def _mod_call(c, w_mod, b_mod):
    rows, d = c.shape
    n = w_mod.shape[1]
    tn = 1024
    return pl.pallas_call(
        _mod_kernel,
        grid=(n // tn,),
        in_specs=[pl.BlockSpec((rows, d), lambda j: (0, 0)),
                  pl.BlockSpec((d, tn), lambda j: (0, j)),
                  pl.BlockSpec((1, tn), lambda j: (0, j))],
        out_specs=pl.BlockSpec((rows, tn), lambda j: (0, j)),
        out_shape=jax.ShapeDtypeStruct((rows, n), F32),
        compiler_params=pltpu.CompilerParams(vmem_limit_bytes=VMEM_LIMIT_BYTES),
        name="mod",
    )(c, w_mod, b_mod.reshape(1, n))


def _rope(x, cos, sin_lo, sin_hi):
    half = B_HD // ROT_FRAC // 2
    return (x * cos + pltpu.roll(x, half, 1) * sin_hi
            + pltpu.roll(x, LANES - half, 1) * sin_lo)


def _inproj_kernel(x_ref, mod_ref, n1_ref, w_ref, cos_ref, slo_ref, shi_ref,
                   hg_ref, qb_ref, k_ref, v_ref, qi_ref, ki_ref, wi_ref, *dsa_refs, bb, tt):
    rows = bb * tt
    x = x_ref[...]
    mod = mod_ref[...]
    sh1 = mod[:, :, 0:D_MODEL]
    sc1 = mod[:, :, D_MODEL:2 * D_MODEL]
    h = (_rms(x) * n1_ref[...]) * (1.0 + sc1) + sh1
    h = h.reshape(rows, D_MODEL).astype(BF16)
    z_att = _dot(h, w_ref[:, OFF_QB:])
    cos, slo, shi = cos_ref[...], slo_ref[...], shi_ref[...]

    def cols(off, width):
        return z_att[:, off - OFF_QB:off - OFF_QB + width]

    def rope_cols(off, width):
        return [_rope(cols(off + j, LANES), cos, slo, shi) for j in range(0, width, LANES)]

    scale = B_HD ** -0.5 * LOG2_E
    qb = jnp.concatenate(rope_cols(OFF_QB, B_WIDTH), axis=1) * scale
    qb_ref[...] = qb.astype(BF16).reshape(bb, tt, B_WIDTH)
    kb = jnp.concatenate(rope_cols(OFF_KB, KV_WIDTH), axis=1)
    k_ref[...] = kb.reshape(bb, tt, KV_WIDTH)
    vb = cols(OFF_VB, KV_WIDTH)
    v_ref[...] = vb.reshape(bb, tt, KV_WIDTH)
    qi = jnp.concatenate(rope_cols(OFF_QI, IDX_WIDTH), axis=1)
    qi_ref[...] = qi.astype(BF16).reshape(bb, tt, IDX_WIDTH)
    last = _rope(cols(OFF_KI, LANES), cos, slo, shi)
    ki_ref[...] = last[:, 0:IDX_DIM].reshape(bb, tt, IDX_DIM)
    wi = cols(OFF_WI, IDX_HEADS) * (IDX_WIDTH ** -0.5)
    wi_ref[...] = wi.reshape(bb, tt, IDX_HEADS)
    if dsa_refs:
        kbf_ref, vt_ref, kibf_ref = dsa_refs
        kbf_ref[0] = kb.astype(BF16)
        vt_ref[0] = vb.T.astype(BF16)
        kibf_ref[0] = last[:, 0:IDX_DIM].astype(BF16)
    hg_ref[...] = _dot(h, w_ref[:, OFF_HG:OFF_QB]).reshape(bb, tt, 4 * A_WIDTH)


def _rope_tables(pos, reps):
    rot = B_HD // ROT_FRAC
    half = rot // 2
    inv = jnp.power(ROPE_THETA, -jnp.arange(half, dtype=F32) * (2.0 / rot))
    ang = pos.astype(F32)[:, None] * inv[None, :]
    cos, sin = jnp.cos(ang), jnp.sin(ang)
    t = pos.shape[0]
    ones = jnp.ones((t, B_HD - rot), F32)
    zeros = jnp.zeros((t, B_HD - rot), F32)
    zh = jnp.zeros((t, half), F32)
    cos_h = jnp.concatenate([cos, cos, ones], axis=1)
    slo_h = jnp.concatenate([-sin, zh, zeros], axis=1)
    shi_h = jnp.concatenate([zh, sin, zeros], axis=1)
    per = LANES // B_HD
    return tuple(jnp.tile(a, (reps, per)) for a in (cos_h, slo_h, shi_h))


def _inproj_call(x, mod, norm1, w_in_bf, pos, bb, tt, dsa_layouts):
    b, t, d = x.shape
    cos, slo, shi = _rope_tables(pos, bb)
    rows = bb * tt
    if bb == 1:
        tab_spec = pl.BlockSpec((tt, LANES), lambda i, j: (j, 0))
    else:
        tab_spec = pl.BlockSpec((rows, LANES), lambda i, j: (0, 0))

    def act_spec(w):
        return pl.BlockSpec((bb, tt, w), lambda i, j: (i, j, 0))

    def out(w, dt):
        return jax.ShapeDtypeStruct((b, t, w), dt)

    out_specs = [act_spec(4 * A_WIDTH), act_spec(B_WIDTH), act_spec(KV_WIDTH),
                 act_spec(KV_WIDTH), act_spec(IDX_WIDTH), act_spec(IDX_DIM),
                 act_spec(IDX_HEADS)]
    out_shape = [out(4 * A_WIDTH, F32), out(B_WIDTH, BF16), out(KV_WIDTH, F32),
                 out(KV_WIDTH, F32), out(IDX_WIDTH, BF16), out(IDX_DIM, F32),
                 out(IDX_HEADS, F32)]
    if dsa_layouts:
        assert bb == 1
        out_specs += [act_spec(KV_WIDTH), pl.BlockSpec((1, KV_WIDTH, tt), lambda i, j: (i, 0, j)),
                      act_spec(IDX_DIM)]
        out_shape += [out(KV_WIDTH, BF16), jax.ShapeDtypeStruct((b, KV_WIDTH, t), BF16),
                      out(IDX_DIM, BF16)]

    return pl.pallas_call(
        functools.partial(_inproj_kernel, bb=bb, tt=tt),
        grid=(b // bb, t // tt),
        in_specs=[act_spec(d),
                  pl.BlockSpec((bb, 1, 6 * d), lambda i, j: (i, 0, 0)),
                  pl.BlockSpec((1, 1, d), lambda i, j: (0, 0, 0)),
                  pl.BlockSpec((d, IN_WIDTH_PAD), lambda i, j: (0, 0)),
                  tab_spec, tab_spec, tab_spec],
        out_specs=out_specs,
        out_shape=out_shape,
        compiler_params=pltpu.CompilerParams(
            dimension_semantics=("parallel", "parallel"),
            vmem_limit_bytes=VMEM_LIMIT_BYTES),
        name="inproj",
    )(x, mod.reshape(b, 1, 6 * d), norm1.reshape(1, 1, d), w_in_bf, cos, slo, shi)


def _hgrn_tables(cc):
    nlev = int(np.log2(cc))
    t = np.arange(cc)[:, None]
    u = np.arange(cc)[None, :]
    mats = [(u <= t).astype(np.float32)]
    for l in range(nlev):
        m = cc >> (l + 1)
        if m < SUBLANES:
            ref = (t // (2 * m)) * (2 * m) + m - 1
            qside = ((t // m) % 2) == 1
            seg = np.where(qside, (u > ref) & (u <= t), (u > t) & (u <= ref))
            mats.append(-seg.astype(np.float32))
    w = np.concatenate(mats, axis=0)
    w = np.concatenate([w] * 3, axis=1)
    lvl = np.full((cc, cc), -1, np.int32)
    for l in range(nlev):
        m = cc >> (l + 1)
        same_parent = (t // (2 * m)) == (u // (2 * m))
        lvl[same_parent & ((t // m) % 2 == 1) & ((u // m) % 2 == 0)] = l
    lvl[np.arange(cc), np.arange(cc)] = nlev
    return jnp.asarray(w, BF16), jnp.asarray(lvl), nlev


def _level_exponent(cum_ref, cols, cum, m):
    cc, n = cum.shape
    parts = []
    for p in range(0, cc, 2 * m):
        ref = jnp.broadcast_to(cum_ref[p + m - 1:p + m, cols], (m, n))
        parts += [cum[p:p + m] - ref, ref - cum[p + m:p + 2 * m]]
    return jnp.concatenate(parts, axis=0)


def _hgrn_kernel(q_ref, f_ref, i_ref, g_ref, lb_ref, gn_ref, s0_ref, w_ref, lvl_ref,
                 o_ref, s_out_ref, st_scr, cum_scr, *, cc, nc, nlev):
    ci = pl.program_id(1)

    @pl.when(ci == 0)
    def _():
        for h in range(A_HEADS):
            st_scr[h] = s0_ref[0, h].T

    row = lax.broadcasted_iota(jnp.int32, (cc, 2 * A_DK), 0)
    qsides = [((row // (cc >> (l + 1))) % 2) == 1 for l in range(nlev)]
    n_small = sum(1 for l in range(nlev) if (cc >> (l + 1)) < SUBLANES)
    qsides16 = [jnp.where(m, 1.0, 0.0).astype(BF16) > 0 for m in qsides]
    lvl = jnp.concatenate([lvl_ref[...]] * 2, axis=1)
    lvl_masks = [lvl == l for l in range(nlev + 1)]
    w = w_ref[...]
    lb = lb_ref[...]
    f_all = lb + (1.0 - lb) * jax.nn.sigmoid(f_ref[0])
    nl = -jnp.log2(f_all)
    nl_hi = nl.astype(BF16)
    rest = nl - nl_hi.astype(F32)
    nl_mid = rest.astype(BF16)
    nl_lo = (rest - nl_mid.astype(F32)).astype(BF16)
    for c in range(nc):
        rs = slice(c * cc, (c + 1) * cc)
        cum_scr[...] = _dot(w[0:cc], jnp.concatenate([nl_hi[rs], nl_mid[rs], nl_lo[rs]], axis=0))
        sums = _dot(w[cc:, 0:2 * cc], jnp.concatenate([nl_hi[rs], nl_mid[rs]], axis=0))
        for hp in range(A_HEADS // 2):
            h0, h1 = 2 * hp, 2 * hp + 1
            sl = slice(h0 * A_DK, (h1 + 1) * A_DK)
            q = _silu(q_ref[0, rs, sl])
            kk = 1.0 - f_all[rs, sl]
            v = i_ref[0, rs, sl]
            cum = cum_scr[:, sl]
            q16, k16, v16 = q.astype(BF16), kk.astype(BF16), v.astype(BF16)
            attn = jnp.zeros((cc, 2 * cc), F32)
            for l in range(nlev):
                m = cc >> (l + 1)
                if m >= SUBLANES:
                    ex = _level_exponent(cum_scr, sl, cum, m)
                else:
                    small = l - (nlev - n_small)
                    ex = sums[small * cc:(small + 1) * cc, sl]
                e16 = jnp.exp2(ex).astype(BF16)
                if m % BF16_ROWS == 0:
                    qk = jnp.concatenate([(q16 if (r // m) % 2 else k16)[r:r + m]
                                          for r in range(0, cc, m)], axis=0)
                else:
                    qk = jnp.where(qsides16[l], q16, k16)
                xl = qk * e16
                attn = jnp.where(lvl_masks[l], _dot_nt(xl, _block_diag(xl, A_DK)), attn)
            attn = jnp.where(lvl_masks[nlev], _dot_nt(q16, _block_diag(k16, A_DK)), attn)
            last = jnp.broadcast_to(cum_scr[cc - 1:cc, sl], (cc, 2 * A_DK))
            qg = (q * jnp.exp2(-cum)).astype(BF16)
            kg = (kk * jnp.exp2(cum - last)).astype(BF16)
            st = jnp.concatenate([st_scr[h0], st_scr[h1]], axis=1)
            o = (_dot(attn.astype(BF16), _block_diag(v16, A_DV))
                 + _dot_nt(qg, _block_diag(st.astype(BF16), A_DK)))
            dec = jnp.exp2(-cum[cc - 1:cc, :])
            upd = _dot(v.T.astype(BF16), kg)
            st_scr[h0] = st[:, :A_DK] * dec[:, :A_DK] + upd[:A_DV, :A_DK]
            st_scr[h1] = st[:, A_DK:] * dec[:, A_DK:] + upd[A_DV:, A_DK:]
            for i, h in enumerate((h0, h1)):
                hs = slice(h * A_DV, (h + 1) * A_DV)
                y = _rms(o[:, i * A_DV:(i + 1) * A_DV]) * gn_ref[:, hs]
                o_ref[0, rs, hs] = (y * _silu(g_ref[0, rs, hs])).astype(BF16)

    @pl.when(ci == pl.num_programs(1) - 1)
    def _():
        for h in range(A_HEADS):
            s_out_ref[0, h] = st_scr[h].T


def _hgrn_call(hg, lb, g_norm, s0, cc):
    b, t, _ = hg.shape
    w, lvl, nlev = _hgrn_tables(cc)
    nc = HGRN_CHUNKS_PER_STEP if t % (HGRN_CHUNKS_PER_STEP * cc) == 0 else 1
    rows = nc * cc

    def part(p):
        return pl.BlockSpec((1, rows, A_WIDTH), lambda i, j, p=p: (i, j, p))

    return pl.pallas_call(
        functools.partial(_hgrn_kernel, cc=cc, nc=nc, nlev=nlev),
        grid=(b, t // rows),
        in_specs=[part(0), part(1), part(2), part(3),
                  pl.BlockSpec((1, A_WIDTH), lambda i, j: (0, 0)),
                  pl.BlockSpec((1, A_WIDTH), lambda i, j: (0, 0)),
                  pl.BlockSpec((1, A_HEADS, A_DK, A_DV), lambda i, j: (i, 0, 0, 0)),
                  pl.BlockSpec(w.shape, lambda i, j: (0, 0)),
                  pl.BlockSpec(lvl.shape, lambda i, j: (0, 0))],
        out_specs=[pl.BlockSpec((1, rows, A_WIDTH), lambda i, j: (i, j, 0)),
                   pl.BlockSpec((1, A_HEADS, A_DK, A_DV), lambda i, j: (i, 0, 0, 0))],
        out_shape=[jax.ShapeDtypeStruct((b, t, A_WIDTH), BF16),
                   jax.ShapeDtypeStruct((b, A_HEADS, A_DK, A_DV), F32)],
        scratch_shapes=[pltpu.VMEM((A_HEADS, A_DV, A_DK), F32),
                        pltpu.VMEM((cc, A_WIDTH), F32)],
        compiler_params=pltpu.CompilerParams(
            dimension_semantics=("parallel", "arbitrary"),
            vmem_limit_bytes=VMEM_LIMIT_BYTES),
        name="hgrn",
    )(hg, hg, hg, hg, lb.reshape(1, A_WIDTH), g_norm.reshape(1, A_WIDTH), s0, w, lvl)


def _fold8(x, op):
    parts = [x[r:r + SUBLANES] for r in range(0, x.shape[0], SUBLANES)]
    while len(parts) > 1:
        parts = [op(parts[i], parts[i + 1]) for i in range(0, len(parts) - 1, 2)] + (
            [parts[-1]] if len(parts) % 2 else [])
    return parts[0]


def _grouped_rhs(xs, heads, width, slot, slots):
    groups = len(xs)
    zero = jnp.zeros((xs[0].shape[0], width), xs[0].dtype)
    rows = []
    for h in heads:
        for g, x in enumerate(xs):
            parts = [zero] * (groups * slots)
            parts[g * slots + slot] = x[:, h * width:(h + 1) * width]
            rows.append(jnp.concatenate(parts, axis=1) if len(parts) > 1 else parts[0])
    return jnp.concatenate(rows, axis=0)


def _dsa_kernel(qi_ref, wit_ref, qb_ref, ki_ref, k_ref, vt_ref, tri_ref, o_ref,
                score_scr, score16_scr, logit_scr, acc_scr, *, groups, gq, qreal, tk, topk, pos0):
    qb = groups * gq
    blk = pl.program_id(1)
    last_pos = pos0 + (blk + 1) * qreal - 1
    extent = jnp.minimum((last_pos // CHUNK + 1) * CHUNK, tk)
    ntile = (extent + KEY_TILE - 1) // KEY_TILE
    lane = lax.broadcasted_iota(jnp.int32, (1, qb), 1)
    qpos = pos0 + blk * qreal + lane % gq
    key_end = jnp.minimum((qpos // CHUNK + 1) * CHUNK, tk)
    neg_inf = jnp.float32(-jnp.inf)

    def lane_cat(parts):
        return jnp.concatenate(parts, axis=1) if len(parts) > 1 else parts[0]

    qis = [qi_ref[g] for g in range(groups)]
    wit = lane_cat([wit_ref[g] for g in range(groups)])
    qi_pairs = [_grouped_rhs(qis, (2 * p, 2 * p + 1), IDX_DIM, 0, 1)
                for p in range(IDX_HEADS // 2)]

    def tile_start(j):
        return pl.multiple_of(j * KEY_TILE, KEY_TILE)

    def tile_loop(body, init):
        def run(first, count, c):
            for u in range(count):
                c = body(first + u, c)
            return c
        carry = lax.fori_loop(0, ntile // 4, lambda i, c: run(4 * i, 4, c), init)
        done = (ntile // 4) * 4
        carry = lax.cond((ntile & 2) != 0, lambda c: run(done, 2, c), lambda c: c, carry)
        return lax.cond((ntile & 1) != 0, lambda c: body(ntile - 1, c), lambda c: c, carry)

    def score_tile(j, carry):
        k0 = tile_start(j)
        ki_t = lane_cat([ki_ref[g, pl.ds(k0, KEY_TILE), :] for g in range(groups)])
        acc = jnp.zeros((KEY_TILE, qb), F32)
        for p in range(IDX_HEADS // 2):
            s2 = jnp.maximum(_dot_nt(ki_t, qi_pairs[p]), 0.0)
            acc = acc + wit[2 * p:2 * p + 1, :] * s2[:, :qb]
            acc = acc + wit[2 * p + 1:2 * p + 2, :] * s2[:, qb:]
        kidx = k0 + lax.broadcasted_iota(jnp.int32, (KEY_TILE, qb), 0)
        masked = jnp.where(kidx < key_end, acc, neg_inf)
        score_scr[pl.ds(k0, KEY_TILE), :] = masked
        score16_scr[pl.ds(k0, KEY_TILE), :] = masked.astype(BF16)
        return carry

    tile_loop(score_tile, 0)

    def count(pred_fn):
        def body(j, acc):
            s = score_scr[pl.ds(tile_start(j), KEY_TILE), :]
            return acc + _fold8(jnp.where(pred_fn(s), 1.0, 0.0), jnp.add)
        acc = tile_loop(body, jnp.zeros((SUBLANES, qb), F32))
        return jnp.sum(acc, axis=0, keepdims=True)

    def count16(cand):
        one, zero = jnp.ones((), BF16), jnp.zeros((), BF16)

        def body(j, acc):
            r = score16_scr[pl.ds(tile_start(j), KEY_TILE), :]
            hit = jnp.where(r >= cand, one, zero)
            parts = [hit[i:i + BF16_ROWS] for i in range(0, KEY_TILE, BF16_ROWS)]
            while len(parts) > 1:
                parts = [parts[i] + parts[i + 1] for i in range(0, len(parts), 2)]
            return acc + parts[0]
        acc = tile_loop(body, jnp.zeros((BF16_ROWS, qb), BF16))
        return jnp.sum(acc.astype(F32), axis=0, keepdims=True)

    def order_key(x):
        return x ^ ((x >> 31) & np.int32(0x7FFFFFFF))

    def decode16(t16):
        b16 = jnp.where(t16 >= 0x8000, t16 & 0x7FFF, ~t16 & 0xFFFF)
        return pltpu.bitcast(b16 << 16, F32)

    def bisect16(i, t16):
        cand16 = t16 | (jnp.int32(1) << (15 - i))
        cnt = count16(decode16(cand16).astype(BF16))
        return jnp.where(cnt >= topk, cand16, t16)

    search = extent > topk
    t16 = lax.fori_loop(0, jnp.where(search, 16, 0), bisect16, jnp.zeros((1, qb), jnp.int32))
    few = t16 <= NEG_INF_PATTERN16
    head = decode16(jnp.maximum(t16, NEG_INF_PATTERN16))
    base = order_key(pltpu.bitcast(head, jnp.int32)) - (1 << 15) - 1

    def bisect17(i, d):
        cand_d = d | (jnp.int32(1) << (16 - i))
        cand = pltpu.bitcast(order_key(base + cand_d), F32)
        cnt = count(lambda s: s >= cand)
        return jnp.where(cnt >= topk, cand_d, d)

    d = lax.fori_loop(0, jnp.where(search, 17, 0), bisect17, jnp.zeros((1, qb), jnp.int32))
    thr = jnp.where(few, neg_inf, pltpu.bitcast(order_key(base + d), F32))
    need = topk - count(lambda s: s > thr)
    need = jnp.where(few, 0.0, need)

    qqs = [qb_ref[g] for g in range(groups)]
    per_q = B_HEADS // B_KV_HEADS
    per_block = LANES // B_HD
    q_pairs = [_grouped_rhs(qqs, (n * per_q, n * per_q + 1), B_HD, n % per_block, per_block)
               for n in range(B_KV_HEADS)]
    tri = tri_ref[...]

    def logit_tile(j, carry):
        offs, mx = carry
        k0 = tile_start(j)
        s = score_scr[pl.ds(k0, KEY_TILE), :]
        tie = jnp.where(s == thr, 1.0, 0.0)
        rank = _dot(tri, tie.astype(BF16)) + offs
        picked = jnp.where(s > thr, 1.0, jnp.where(rank < need, tie, 0.0))
        bias = jnp.where(picked > 0.0, 0.0, NEG_BIG)
        bias2 = jnp.concatenate([bias, bias], axis=1)
        new_mx = []
        for n in range(B_KV_HEADS):
            kblk = (n // per_block) * LANES
            k_t = lane_cat([k_ref[g, pl.ds(k0, KEY_TILE), kblk:kblk + LANES]
                            for g in range(groups)])
            lg = _dot_nt(k_t, q_pairs[n]) + bias2
            logit_scr[n, pl.ds(k0, KEY_TILE), :] = lg
            new_mx.append(jnp.maximum(mx[n], _fold8(lg, jnp.maximum)))
        offs = offs + jnp.sum(_fold8(tie, jnp.add), axis=0, keepdims=True)
        return offs, tuple(new_mx)

    mx0 = tuple(jnp.full((SUBLANES, 2 * qb), NEG_BIG, F32) for _ in range(B_KV_HEADS))
    _, mx = tile_loop(logit_tile, (jnp.zeros((1, qb), F32), mx0))
    mx = [jnp.max(m, axis=0, keepdims=True) for m in mx]

    acc_scr[...] = jnp.zeros_like(acc_scr)

    def pv_tile(j, den):
        k0 = tile_start(j)
        new_den = []
        for n in range(B_KV_HEADS):
            p = jnp.exp2(logit_scr[n, pl.ds(k0, KEY_TILE), :] - mx[n])
            new_den.append(den[n] + _fold8(p, jnp.add))
            vts = [vt_ref[g, n * B_HD:(n + 1) * B_HD, pl.ds(k0, KEY_TILE)] for g in range(groups)]
            vt = jnp.concatenate(vts, axis=0) if groups > 1 else vts[0]
            acc_scr[n] += _dot(vt, p.astype(BF16))
        return tuple(new_den)

    den0 = tuple(jnp.zeros((SUBLANES, 2 * qb), F32) for _ in range(B_KV_HEADS))
    den = tile_loop(pv_tile, den0)
    lane_group = (lax.broadcasted_iota(jnp.int32, (1, 2 * qb), 1) % qb) // gq
    rows = []
    for n in range(B_KV_HEADS):
        acc = acc_scr[n]
        o2 = acc[0:B_HD]
        for g in range(1, groups):
            o2 = jnp.where(lane_group == g, acc[g * B_HD:(g + 1) * B_HD], o2)
        o2 = o2 / jnp.sum(den[n], axis=0, keepdims=True)
        rows += [o2[:, :qb], o2[:, qb:]]
    o_ref[...] = jnp.concatenate(rows, axis=0).T.astype(BF16).reshape(groups, gq, B_WIDTH)


def _dsa_call(qi, wi, qbs, k_bf, vt, ki_bf, tk, pos0):
    b, t, _ = qi.shape
    if t % DSA_QUERIES == 0:
        groups, gq = 1, DSA_QUERIES
    else:
        assert LANES % t == 0 and b % (LANES // t) == 0
        groups, gq = LANES // t, t
    qb = groups * gq
    tkp = ki_bf.shape[1]
    topk = min(TOPK_MAX, tk // 4)
    assert topk <= KEY_TILE
    wit = jnp.swapaxes(wi, 1, 2)
    tri = jnp.asarray(np.tril(np.ones((KEY_TILE, KEY_TILE), np.float32), -1), BF16)

    def q_spec(w):
        return pl.BlockSpec((groups, gq, w), lambda i, j: (i, j, 0))

    def kv_spec(rows, cols):
        return pl.BlockSpec((groups, rows, cols), lambda i, j: (i, 0, 0))

    return pl.pallas_call(
        functools.partial(_dsa_kernel, groups=groups, gq=gq, qreal=gq, tk=tk, topk=topk,
                          pos0=pos0),
        grid=(b // groups, t // gq),
        in_specs=[q_spec(IDX_WIDTH),
                  pl.BlockSpec((groups, IDX_HEADS, gq), lambda i, j: (i, 0, j)),
                  q_spec(B_WIDTH),
                  kv_spec(tkp, IDX_DIM), kv_spec(tkp, KV_WIDTH), kv_spec(KV_WIDTH, tkp),
                  pl.BlockSpec((KEY_TILE, KEY_TILE), lambda i, j: (0, 0))],
        out_specs=q_spec(B_WIDTH),
        out_shape=jax.ShapeDtypeStruct((b, t, B_WIDTH), BF16),
        scratch_shapes=[pltpu.VMEM((tkp, qb), F32),
                        pltpu.VMEM((tkp, qb), BF16),
                        pltpu.VMEM((B_KV_HEADS, tkp, 2 * qb), F32),
                        pltpu.VMEM((B_KV_HEADS, groups * B_HD, 2 * qb), F32)],
        compiler_params=pltpu.CompilerParams(
            dimension_semantics=("parallel", "parallel"),
            vmem_limit_bytes=VMEM_LIMIT_BYTES),
        name="dsa",
    )(qi, wit, qbs, ki_bf, k_bf, vt, tri)


def _out_kernel(x_ref, oa_ref, ob_ref, mod_ref, n2_ref, nf_ref, wo_ref, w1_ref, w2_ref,
                y_ref, *, bb, tt):
    rows = bb * tt
    x = x_ref[...]
    mod = mod_ref[...]
    g1 = mod[:, :, 2 * D_MODEL:3 * D_MODEL]
    sh2 = mod[:, :, 3 * D_MODEL:4 * D_MODEL]
    sc2 = mod[:, :, 4 * D_MODEL:5 * D_MODEL]
    g2 = mod[:, :, 5 * D_MODEL:6 * D_MODEL]
    oa = oa_ref[...].reshape(rows, A_WIDTH)
    ob = ob_ref[...].reshape(rows, B_WIDTH)
    mix = _dot(oa, wo_ref[0:A_WIDTH, :]) + _dot(ob, wo_ref[A_WIDTH:A_WIDTH + B_WIDTH, :])
    x = x + g1 * mix.reshape(bb, tt, D_MODEL)
    h2 = (_rms(x) * n2_ref[...]) * (1.0 + sc2) + sh2
    u = _dot(h2.reshape(rows, D_MODEL).astype(BF16), w1_ref[...])
    r = jnp.square(jnp.maximum(u, 0.0)).astype(BF16)
    x = x + g2 * _dot(r, w2_ref[...]).reshape(bb, tt, D_MODEL)
    y_ref[...] = _rms(x) * nf_ref[...]


def _out_call(x, oa, ob, mod, norm2, norm_f, wo_bf, w1_bf, w2_bf, bb, tt):
    b, t, d = x.shape

    def act_spec(w):
        return pl.BlockSpec((bb, tt, w), lambda i, j: (i, j, 0))

    def const_spec(shape):
        zeros = (0,) * len(shape)
        return pl.BlockSpec(shape, lambda i, j: zeros, pipeline_mode=pl.Buffered(1))

    return pl.pallas_call(
        functools.partial(_out_kernel, bb=bb, tt=tt),
        grid=(b // bb, t // tt),
        in_specs=[act_spec(d), act_spec(A_WIDTH), act_spec(B_WIDTH),
                  pl.BlockSpec((bb, 1, 6 * d), lambda i, j: (i, 0, 0)),
                  const_spec((1, 1, d)), const_spec((1, 1, d)),
                  const_spec(wo_bf.shape), const_spec(w1_bf.shape), const_spec(w2_bf.shape)],
        out_specs=act_spec(d),
        out_shape=jax.ShapeDtypeStruct((b, t, d), F32),
        compiler_params=pltpu.CompilerParams(
            dimension_semantics=("parallel", "parallel"),
            vmem_limit_bytes=VMEM_LIMIT_BYTES),
        name="out",
    )(x, oa, ob, mod.reshape(b, 1, 6 * d), norm2.reshape(1, 1, d), norm_f.reshape(1, 1, d),
      wo_bf, w1_bf, w2_bf)


def _layer(x, mod, pos0, s0, k_past, v_past, ki_past, weights, bb, tt, cc):
    norm1, w_in_bf, lb, g_norm, wo_bf, norm2, w1_bf, w2_bf, norm_f = weights
    b, t, _ = x.shape
    pos = pos0 + jnp.arange(t)
    outs = _inproj_call(x, mod, norm1, w_in_bf, pos, bb, tt, dsa_layouts=k_past is None)
    hg, qbs, k_new, v_new, qi, ki_new, wi = outs[:7]
    oa, s_new = _hgrn_call(hg, lb, g_norm, s0, cc)
    if k_past is None:
        assert t % KEY_TILE == 0
        tk = t
        k_bf, vt, ki_bf = outs[7:]
    else:
        n_past = k_past.shape[1]
        tk = n_past + t
        tkp = -(-tk // KEY_TILE) * KEY_TILE

        def cat(past, new):
            return jnp.pad(jnp.concatenate([past, new], axis=1),
                           ((0, 0), (0, tkp - tk), (0, 0))).astype(BF16)

        k_bf = cat(k_past.reshape(b, n_past, KV_WIDTH), k_new)
        vt = jnp.swapaxes(cat(v_past.reshape(b, n_past, KV_WIDTH), v_new), 1, 2)
        ki_bf = cat(ki_past, ki_new)
    ob = _dsa_call(qi, wi, qbs, k_bf, vt, ki_bf, tk, pos0)
    y = _out_call(x, oa, ob, mod, norm2, norm_f, wo_bf, w1_bf, w2_bf, bb, tt)
    return (y, k_new.reshape(b, t, B_KV_HEADS, B_HD), v_new.reshape(b, t, B_KV_HEADS, B_HD),
            ki_new, s_new)


def kernel(x_prompt, x_sample, cache_k, cache_v, cache_k_idx, state_hgrn, c_prompt, c_sample,
           w_mod, b_mod, norm1, w_in, lb_logits, g_norm_a, w_out, norm2, w_ff1, w_ff2, norm_f):
    depth = w_in.shape[0]
    assert depth == 1, "kernel is written for the single-layer configuration"
    lb_all = jnp.cumsum(jax.nn.softmax(lb_logits.astype(F32), axis=0), axis=0)
    bp, tp, _ = x_prompt.shape
    bs, ts, _ = x_sample.shape
    past = cache_k.shape[2]
    l = 0
    mod = _mod_call(jnp.concatenate([c_prompt, c_sample], axis=0), w_mod[l], b_mod[l])
    w_in_bf = jnp.pad(w_in[l], ((0, 0), (0, IN_WIDTH_PAD - IN_WIDTH))).astype(BF16)
    weights = (norm1[l], w_in_bf, lb_all[l], g_norm_a[l], w_out[l].astype(BF16), norm2[l],
               w_ff1[l].astype(BF16), w_ff2[l].astype(BF16), norm_f)
    s0 = jnp.zeros((bp, A_HEADS, A_DK, A_DV), F32)
    yp, kp, vp, kip, sp = _layer(x_prompt, mod[:bp], 0, s0, None, None, None, weights,
                                 bb=1, tt=min(512, tp), cc=min(128, tp))
    ys, ks, vs, kis, ss = _layer(x_sample, mod[bp:], past, state_hgrn[l], cache_k[l],
                                 cache_v[l], cache_k_idx[l], weights,
                                 bb=bs, tt=ts, cc=min(128, ts))
    return (yp, ys, kp[None], vp[None], kip[None], sp[None],
            ks[None], vs[None], kis[None], ss[None])
```

```python
import functools

import numpy as np
import jax
import jax.numpy as jnp
from jax import lax
from jax.experimental import pallas as pl
from jax.experimental.pallas import tpu as pltpu

D_MODEL = 1024
CHUNK = 64
A_HEADS = 4
A_DK = 128
A_DV = 128
A_WIDTH = A_HEADS * A_DV
B_HEADS = 8
B_KV_HEADS = 4
B_HD = 64
B_WIDTH = B_HEADS * B_HD
KV_WIDTH = B_KV_HEADS * B_HD
IDX_HEADS = 8
IDX_DIM = 64
IDX_WIDTH = IDX_HEADS * IDX_DIM
TOPK_MAX = 256
ROT_FRAC = 4
ROPE_THETA = 500000.0
D_FF = 4 * D_MODEL
EPS = 1e-6
IN_WIDTH = 4 * A_WIDTH + B_WIDTH + 2 * KV_WIDTH + IDX_WIDTH + IDX_DIM + IDX_HEADS

LANES = 128
SUBLANES = 8
BF16_ROWS = 16
KEY_TILE = 256
DSA_QUERIES = 256
ROW_TILE = 512
HGRN_CHUNK = 128
HGRN_CHUNKS_PER_STEP = 8
IN_WIDTH_PAD = -(-IN_WIDTH // LANES) * LANES
VMEM_LIMIT_BYTES = 56 * 1024 * 1024

F32 = jnp.float32
BF16 = jnp.bfloat16
INT_MIN = np.int32(-2 ** 31)
NEG_INF_PATTERN16 = 0x007F
NEG_BIG = -1e30
LOG2_E = 1.4426950408889634

OFF_HG = 0
OFF_QB = 4 * A_WIDTH
OFF_KB = OFF_QB + B_WIDTH
OFF_VB = OFF_KB + KV_WIDTH
OFF_QI = OFF_VB + KV_WIDTH
OFF_KI = OFF_QI + IDX_WIDTH
OFF_WI = OFF_KI + IDX_DIM


def _dot(a, b):
    return jnp.dot(a, b, preferred_element_type=F32)


def _dot_nt(a, b):
    return lax.dot_general(a, b, (((1,), (1,)), ((), ())), preferred_element_type=F32)


def _silu(x):
    return x * jax.nn.sigmoid(x)


def _rms(x):
    return x * lax.rsqrt(jnp.mean(jnp.square(x), axis=-1, keepdims=True) + EPS)


def _block_diag(x, width):
    zero = jnp.zeros((x.shape[0], width), x.dtype)
    return jnp.concatenate([jnp.concatenate([x[:, :width], zero], axis=1),
                            jnp.concatenate([zero, x[:, width:]], axis=1)], axis=0)


def _mod_kernel(c_ref, w_ref, b_ref, o_ref):
    a = _silu(c_ref[...])
    w = w_ref[...]
    a_hi = a.astype(BF16)
    a_lo = (a - a_hi.astype(F32)).astype(BF16)
    w_hi = w.astype(BF16)
    w_lo = (w - w_hi.astype(F32)).astype(BF16)
    o_ref[...] = _dot(a_hi, w_hi) + _dot(a_lo, w_hi) + _dot(a_hi, w_lo) + b_ref[...]


def _mod_call(c, w_mod, b_mod):
    rows, d = c.shape
    n = w_mod.shape[1]
    tn = 1024
    return pl.pallas_call(
        _mod_kernel,
        grid=(n // tn,),
        in_specs=[pl.BlockSpec((rows, d), lambda j: (0, 0)),
                  pl.BlockSpec((d, tn), lambda j: (0, j)),
                  pl.BlockSpec((1, tn), lambda j: (0, j))],
        out_specs=pl.BlockSpec((rows, tn), lambda j: (0, j)),
        out_shape=jax.ShapeDtypeStruct((rows, n), F32),
        compiler_params=pltpu.CompilerParams(vmem_limit_bytes=VMEM_LIMIT_BYTES),
        name="mod",
    )(c, w_mod, b_mod.reshape(1, n))


def _rope(x, cos, sin_lo, sin_hi):
    half = B_HD // ROT_FRAC // 2
    return (x * cos + pltpu.roll(x, half, 1) * sin_hi
            + pltpu.roll(x, LANES - half, 1) * sin_lo)


def _inproj_kernel(x_ref, mod_ref, n1_ref, w_ref, cos_ref, slo_ref, shi_ref,
                   hg_ref, qb_ref, k_ref, v_ref, qi_ref, ki_ref, wi_ref, *dsa_refs, bb, tt):
    rows = bb * tt
    x = x_ref[...]
    mod = mod_ref[...]
    sh1 = mod[:, :, 0:D_MODEL]
    sc1 = mod[:, :, D_MODEL:2 * D_MODEL]
    h = (_rms(x) * n1_ref[...]) * (1.0 + sc1) + sh1
    h = h.reshape(rows, D_MODEL).astype(BF16)
    z_att = _dot(h, w_ref[:, OFF_QB:])
    cos, slo, shi = cos_ref[...], slo_ref[...], shi_ref[...]

    def cols(off, width):
        return z_att[:, off - OFF_QB:off - OFF_QB + width]

    def rope_cols(off, width):
        return [_rope(cols(off + j, LANES), cos, slo, shi) for j in range(0, width, LANES)]

    scale = B_HD ** -0.5 * LOG2_E
    qb = jnp.concatenate(rope_cols(OFF_QB, B_WIDTH), axis=1) * scale
    qb_ref[...] = qb.astype(BF16).reshape(bb, tt, B_WIDTH)
    kb = jnp.concatenate(rope_cols(OFF_KB, KV_WIDTH), axis=1)
    k_ref[...] = kb.reshape(bb, tt, KV_WIDTH)
    vb = cols(OFF_VB, KV_WIDTH)
    v_ref[...] = vb.reshape(bb, tt, KV_WIDTH)
    qi = jnp.concatenate(rope_cols(OFF_QI, IDX_WIDTH), axis=1)
    qi_ref[...] = qi.astype(BF16).reshape(bb, tt, IDX_WIDTH)
    last = _rope(cols(OFF_KI, LANES), cos, slo, shi)
    ki_ref[...] = last[:, 0:IDX_DIM].reshape(bb, tt, IDX_DIM)
    wi = cols(OFF_WI, IDX_HEADS) * (IDX_WIDTH ** -0.5)
    wi_ref[...] = wi.reshape(bb, tt, IDX_HEADS)
    if dsa_refs:
        kbf_ref, vt_ref, kibf_ref = dsa_refs
        kbf_ref[0] = kb.astype(BF16)
        vt_ref[0] = vb.T.astype(BF16)
        kibf_ref[0] = last[:, 0:IDX_DIM].astype(BF16)
    hg_ref[...] = _dot(h, w_ref[:, OFF_HG:OFF_QB]).reshape(bb, tt, 4 * A_WIDTH)


def _rope_tables(pos, reps):
    rot = B_HD // ROT_FRAC
    half = rot // 2
    inv = jnp.power(ROPE_THETA, -jnp.arange(half, dtype=F32) * (2.0 / rot))
    ang = pos.astype(F32)[:, None] * inv[None, :]
    cos, sin = jnp.cos(ang), jnp.sin(ang)
    t = pos.shape[0]
    ones = jnp.ones((t, B_HD - rot), F32)
    zeros = jnp.zeros((t, B_HD - rot), F32)
    zh = jnp.zeros((t, half), F32)
    cos_h = jnp.concatenate([cos, cos, ones], axis=1)
    slo_h = jnp.concatenate([-sin, zh, zeros], axis=1)
    shi_h = jnp.concatenate([zh, sin, zeros], axis=1)
    per = LANES // B_HD
    return tuple(jnp.tile(a, (reps, per)) for a in (cos_h, slo_h, shi_h))


def _inproj_call(x, mod, norm1, w_in_bf, pos, bb, tt, dsa_layouts):
    b, t, d = x.shape
    cos, slo, shi = _rope_tables(pos, bb)
    rows = bb * tt
    if bb == 1:
        tab_spec = pl.BlockSpec((tt, LANES), lambda i, j: (j, 0))
    else:
        tab_spec = pl.BlockSpec((rows, LANES), lambda i, j: (0, 0))

    def act_spec(w):
        return pl.BlockSpec((bb, tt, w), lambda i, j: (i, j, 0))

    def out(w, dt):
        return jax.ShapeDtypeStruct((b, t, w), dt)

    out_specs = [act_spec(4 * A_WIDTH), act_spec(B_WIDTH), act_spec(KV_WIDTH),
                 act_spec(KV_WIDTH), act_spec(IDX_WIDTH), act_spec(IDX_DIM),
                 act_spec(IDX_HEADS)]
    out_shape = [out(4 * A_WIDTH, F32), out(B_WIDTH, BF16), out(KV_WIDTH, F32),
                 out(KV_WIDTH, F32), out(IDX_WIDTH, BF16), out(IDX_DIM, F32),
                 out(IDX_HEADS, F32)]
    if dsa_layouts:
        assert bb == 1
        out_specs += [act_spec(KV_WIDTH), pl.BlockSpec((1, KV_WIDTH, tt), lambda i, j: (i, 0, j)),
                      act_spec(IDX_DIM)]
        out_shape += [out(KV_WIDTH, BF16), jax.ShapeDtypeStruct((b, KV_WIDTH, t), BF16),
                      out(IDX_DIM, BF16)]

    return pl.pallas_call(
        functools.partial(_inproj_kernel, bb=bb, tt=tt),
        grid=(b // bb, t // tt),
        in_specs=[act_spec(d),
                  pl.BlockSpec((bb, 1, 6 * d), lambda i, j: (i, 0, 0)),
                  pl.BlockSpec((1, 1, d), lambda i, j: (0, 0, 0)),
                  pl.BlockSpec((d, IN_WIDTH_PAD), lambda i, j: (0, 0)),
                  tab_spec, tab_spec, tab_spec],
        out_specs=out_specs,
        out_shape=out_shape,
        compiler_params=pltpu.CompilerParams(
            dimension_semantics=("parallel", "parallel"),
            vmem_limit_bytes=VMEM_LIMIT_BYTES),
        name="inproj",
    )(x, mod.reshape(b, 1, 6 * d), norm1.reshape(1, 1, d), w_in_bf, cos, slo, shi)


def _hgrn_tables(cc):
    nlev = int(np.log2(cc))
    t = np.arange(cc)[:, None]
    u = np.arange(cc)[None, :]
    mats = [(u <= t).astype(np.float32)]
    for l in range(nlev):
        m = cc >> (l + 1)
        if m < SUBLANES:
            ref = (t // (2 * m)) * (2 * m) + m - 1
            qside = ((t // m) % 2) == 1
            seg = np.where(qside, (u > ref) & (u <= t), (u > t) & (u <= ref))
            mats.append(-seg.astype(np.float32))
    w = np.concatenate(mats, axis=0)
    w = np.concatenate([w] * 3, axis=1)
    lvl = np.full((cc, cc), -1, np.int32)
    for l in range(nlev):
        m = cc >> (l + 1)
        same_parent = (t // (2 * m)) == (u // (2 * m))
        lvl[same_parent & ((t // m) % 2 == 1) & ((u // m) % 2 == 0)] = l
    lvl[np.arange(cc), np.arange(cc)] = nlev
    return jnp.asarray(w, BF16), jnp.asarray(lvl), nlev


def _level_exponent(cum_ref, cols, cum, m):
    cc, n = cum.shape
    parts = []
    for p in range(0, cc, 2 * m):
        ref = jnp.broadcast_to(cum_ref[p + m - 1:p + m, cols], (m, n))
        parts += [cum[p:p + m] - ref, ref - cum[p + m:p + 2 * m]]
    return jnp.concatenate(parts, axis=0)


def _hgrn_kernel(q_ref, f_ref, i_ref, g_ref, lb_ref, gn_ref, s0_ref, w_ref, lvl_ref,
                 o_ref, s_out_ref, st_scr, cum_scr, *, cc, nc, nlev):
    ci = pl.program_id(1)

    @pl.when(ci == 0)
    def _():
        for h in range(A_HEADS):
            st_scr[h] = s0_ref[0, h].T

    row = lax.broadcasted_iota(jnp.int32, (cc, 2 * A_DK), 0)
    qsides = [((row // (cc >> (l + 1))) % 2) == 1 for l in range(nlev)]
    n_small = sum(1 for l in range(nlev) if (cc >> (l + 1)) < SUBLANES)
    qsides16 = [jnp.where(m, 1.0, 0.0).astype(BF16) > 0 for m in qsides]
    lvl = jnp.concatenate([lvl_ref[...]] * 2, axis=1)
    lvl_masks = [lvl == l for l in range(nlev + 1)]
    w = w_ref[...]
    lb = lb_ref[...]
    f_all = lb + (1.0 - lb) * jax.nn.sigmoid(f_ref[0])
    nl = -jnp.log2(f_all)
    nl_hi = nl.astype(BF16)
    rest = nl - nl_hi.astype(F32)
    nl_mid = rest.astype(BF16)
    nl_lo = (rest - nl_mid.astype(F32)).astype(BF16)
    for c in range(nc):
        rs = slice(c * cc, (c + 1) * cc)
        cum_scr[...] = _dot(w[0:cc], jnp.concatenate([nl_hi[rs], nl_mid[rs], nl_lo[rs]], axis=0))
        sums = _dot(w[cc:, 0:2 * cc], jnp.concatenate([nl_hi[rs], nl_mid[rs]], axis=0))
        for hp in range(A_HEADS // 2):
            h0, h1 = 2 * hp, 2 * hp + 1
            sl = slice(h0 * A_DK, (h1 + 1) * A_DK)
            q = _silu(q_ref[0, rs, sl])
            kk = 1.0 - f_all[rs, sl]
            v = i_ref[0, rs, sl]
            cum = cum_scr[:, sl]
            q16, k16, v16 = q.astype(BF16), kk.astype(BF16), v.astype(BF16)
            attn = jnp.zeros((cc, 2 * cc), F32)
            for l in range(nlev):
                m = cc >> (l + 1)
                if m >= SUBLANES:
                    ex = _level_exponent(cum_scr, sl, cum, m)
                else:
                    small = l - (nlev - n_small)
                    ex = sums[small * cc:(small + 1) * cc, sl]
                e16 = jnp.exp2(ex).astype(BF16)
                if m % BF16_ROWS == 0:
                    qk = jnp.concatenate([(q16 if (r // m) % 2 else k16)[r:r + m]
                                          for r in range(0, cc, m)], axis=0)
                else:
                    qk = jnp.where(qsides16[l], q16, k16)
                xl = qk * e16
                attn = jnp.where(lvl_masks[l], _dot_nt(xl, _block_diag(xl, A_DK)), attn)
            attn = jnp.where(lvl_masks[nlev], _dot_nt(q16, _block_diag(k16, A_DK)), attn)
            last = jnp.broadcast_to(cum_scr[cc - 1:cc, sl], (cc, 2 * A_DK))
            qg = (q * jnp.exp2(-cum)).astype(BF16)
            kg = (kk * jnp.exp2(cum - last)).astype(BF16)
            st = jnp.concatenate([st_scr[h0], st_scr[h1]], axis=1)
            o = (_dot(attn.astype(BF16), _block_diag(v16, A_DV))
                 + _dot_nt(qg, _block_diag(st.astype(BF16), A_DK)))
            dec = jnp.exp2(-cum[cc - 1:cc, :])
            upd = _dot(v.T.astype(BF16), kg)
            st_scr[h0] = st[:, :A_DK] * dec[:, :A_DK] + upd[:A_DV, :A_DK]
            st_scr[h1] = st[:, A_DK:] * dec[:, A_DK:] + upd[A_DV:, A_DK:]
            for i, h in enumerate((h0, h1)):
                hs = slice(h * A_DV, (h + 1) * A_DV)
                y = _rms(o[:, i * A_DV:(i + 1) * A_DV]) * gn_ref[:, hs]
                o_ref[0, rs, hs] = (y * _silu(g_ref[0, rs, hs])).astype(BF16)

    @pl.when(ci == pl.num_programs(1) - 1)
    def _():
        for h in range(A_HEADS):
            s_out_ref[0, h] = st_scr[h].T


def _hgrn_call(hg, lb, g_norm, s0, cc):
    b, t, _ = hg.shape
    w, lvl, nlev = _hgrn_tables(cc)
    nc = HGRN_CHUNKS_PER_STEP if t % (HGRN_CHUNKS_PER_STEP * cc) == 0 else 1
    rows = nc * cc

    def part(p):
        return pl.BlockSpec((1, rows, A_WIDTH), lambda i, j, p=p: (i, j, p))

    return pl.pallas_call(
        functools.partial(_hgrn_kernel, cc=cc, nc=nc, nlev=nlev),
        grid=(b, t // rows),
        in_specs=[part(0), part(1), part(2), part(3),
                  pl.BlockSpec((1, A_WIDTH), lambda i, j: (0, 0)),
                  pl.BlockSpec((1, A_WIDTH), lambda i, j: (0, 0)),
                  pl.BlockSpec((1, A_HEADS, A_DK, A_DV), lambda i, j: (i, 0, 0, 0)),
                  pl.BlockSpec(w.shape, lambda i, j: (0, 0)),
                  pl.BlockSpec(lvl.shape, lambda i, j: (0, 0))],
        out_specs=[pl.BlockSpec((1, rows, A_WIDTH), lambda i, j: (i, j, 0)),
                   pl.BlockSpec((1, A_HEADS, A_DK, A_DV), lambda i, j: (i, 0, 0, 0))],
        out_shape=[jax.ShapeDtypeStruct((b, t, A_WIDTH), BF16),
                   jax.ShapeDtypeStruct((b, A_HEADS, A_DK, A_DV), F32)],
        scratch_shapes=[pltpu.VMEM((A_HEADS, A_DV, A_DK), F32),
                        pltpu.VMEM((cc, A_WIDTH), F32)],
        compiler_params=pltpu.CompilerParams(
            dimension_semantics=("parallel", "arbitrary"),
            vmem_limit_bytes=VMEM_LIMIT_BYTES),
        name="hgrn",
    )(hg, hg, hg, hg, lb.reshape(1, A_WIDTH), g_norm.reshape(1, A_WIDTH), s0, w, lvl)


def _fold8(x, op):
    parts = [x[r:r + SUBLANES] for r in range(0, x.shape[0], SUBLANES)]
    while len(parts) > 1:
        parts = [op(parts[i], parts[i + 1]) for i in range(0, len(parts) - 1, 2)] + (
            [parts[-1]] if len(parts) % 2 else [])
    return parts[0]


def _grouped_rhs(xs, heads, width, slot, slots):
    groups = len(xs)
    zero = jnp.zeros((xs[0].shape[0], width), xs[0].dtype)
    rows = []
    for h in heads:
        for g, x in enumerate(xs):
            parts = [zero] * (groups * slots)
            parts[g * slots + slot] = x[:, h * width:(h + 1) * width]
            rows.append(jnp.concatenate(parts, axis=1) if len(parts) > 1 else parts[0])
    return jnp.concatenate(rows, axis=0)


def _dsa_kernel(qi_ref, wit_ref, qb_ref, ki_ref, k_ref, vt_ref, tri_ref, o_ref,
                score_scr, score16_scr, logit_scr, acc_scr, *, groups, gq, qreal, tk, topk, pos0):
    qb = groups * gq
    blk = pl.program_id(1)
    last_pos = pos0 + (blk + 1) * qreal - 1
    extent = jnp.minimum((last_pos // CHUNK + 1) * CHUNK, tk)
    ntile = (extent + KEY_TILE - 1) // KEY_TILE
    lane = lax.broadcasted_iota(jnp.int32, (1, qb), 1)
    qpos = pos0 + blk * qreal + lane % gq
    key_end = jnp.minimum((qpos // CHUNK + 1) * CHUNK, tk)
    neg_inf = jnp.float32(-jnp.inf)

    def lane_cat(parts):
        return jnp.concatenate(parts, axis=1) if len(parts) > 1 else parts[0]

    qis = [qi_ref[g] for g in range(groups)]
    wit = lane_cat([wit_ref[g] for g in range(groups)])
    qi_pairs = [_grouped_rhs(qis, (2 * p, 2 * p + 1), IDX_DIM, 0, 1)
                for p in range(IDX_HEADS // 2)]

    def tile_start(j):
        return pl.multiple_of(j * KEY_TILE, KEY_TILE)

    def tile_loop(body, init):
        def run(first, count, c):
            for u in range(count):
                c = body(first + u, c)
            return c
        carry = lax.fori_loop(0, ntile // 4, lambda i, c: run(4 * i, 4, c), init)
        done = (ntile // 4) * 4
        carry = lax.cond((ntile & 2) != 0, lambda c: run(done, 2, c), lambda c: c, carry)
        return lax.cond((ntile & 1) != 0, lambda c: body(ntile - 1, c), lambda c: c, carry)

    def score_tile(j, carry):
        k0 = tile_start(j)
        ki_t = lane_cat([ki_ref[g, pl.ds(k0, KEY_TILE), :] for g in range(groups)])
        acc = jnp.zeros((KEY_TILE, qb), F32)
        for p in range(IDX_HEADS // 2):
            s2 = jnp.maximum(_dot_nt(ki_t, qi_pairs[p]), 0.0)
            acc = acc + wit[2 * p:2 * p + 1, :] * s2[:, :qb]
            acc = acc + wit[2 * p + 1:2 * p + 2, :] * s2[:, qb:]
        kidx = k0 + lax.broadcasted_iota(jnp.int32, (KEY_TILE, qb), 0)
        masked = jnp.where(kidx < key_end, acc, neg_inf)
        score_scr[pl.ds(k0, KEY_TILE), :] = masked
        score16_scr[pl.ds(k0, KEY_TILE), :] = masked.astype(BF16)
        return carry

    tile_loop(score_tile, 0)

    def count(pred_fn):
        def body(j, acc):
            s = score_scr[pl.ds(tile_start(j), KEY_TILE), :]
            return acc + _fold8(jnp.where(pred_fn(s), 1.0, 0.0), jnp.add)
        acc = tile_loop(body, jnp.zeros((SUBLANES, qb), F32))
        return jnp.sum(acc, axis=0, keepdims=True)

    def count16(cand):
        one, zero = jnp.ones((), BF16), jnp.zeros((), BF16)

        def body(j, acc):
            r = score16_scr[pl.ds(tile_start(j), KEY_TILE), :]
            hit = jnp.where(r >= cand, one, zero)
            parts = [hit[i:i + BF16_ROWS] for i in range(0, KEY_TILE, BF16_ROWS)]
            while len(parts) > 1:
                parts = [parts[i] + parts[i + 1] for i in range(0, len(parts), 2)]
            return acc + parts[0]
        acc = tile_loop(body, jnp.zeros((BF16_ROWS, qb), BF16))
        return jnp.sum(acc.astype(F32), axis=0, keepdims=True)

    def order_key(x):
        return x ^ ((x >> 31) & np.int32(0x7FFFFFFF))

    def decode16(t16):
        b16 = jnp.where(t16 >= 0x8000, t16 & 0x7FFF, ~t16 & 0xFFFF)
        return pltpu.bitcast(b16 << 16, F32)

    def bisect16(i, t16):
        cand16 = t16 | (jnp.int32(1) << (15 - i))
        cnt = count16(decode16(cand16).astype(BF16))
        return jnp.where(cnt >= topk, cand16, t16)

    search = extent > topk
    t16 = lax.fori_loop(0, jnp.where(search, 16, 0), bisect16, jnp.zeros((1, qb), jnp.int32))
    few = t16 <= NEG_INF_PATTERN16
    head = decode16(jnp.maximum(t16, NEG_INF_PATTERN16))
    base = order_key(pltpu.bitcast(head, jnp.int32)) - (1 << 15) - 1

    def bisect17(i, d):
        cand_d = d | (jnp.int32(1) << (16 - i))
        cand = pltpu.bitcast(order_key(base + cand_d), F32)
        cnt = count(lambda s: s >= cand)
        return jnp.where(cnt >= topk, cand_d, d)

    d = lax.fori_loop(0, jnp.where(search, 17, 0), bisect17, jnp.zeros((1, qb), jnp.int32))
    thr = jnp.where(few, neg_inf, pltpu.bitcast(order_key(base + d), F32))
    need = topk - count(lambda s: s > thr)
    need = jnp.where(few, 0.0, need)

    qqs = [qb_ref[g] for g in range(groups)]
    per_q = B_HEADS // B_KV_HEADS
    per_block = LANES // B_HD
    q_pairs = [_grouped_rhs(qqs, (n * per_q, n * per_q + 1), B_HD, n % per_block, per_block)
               for n in range(B_KV_HEADS)]
    tri = tri_ref[...]

    def logit_tile(j, carry):
        offs, mx = carry
        k0 = tile_start(j)
        s = score_scr[pl.ds(k0, KEY_TILE), :]
        tie = jnp.where(s == thr, 1.0, 0.0)
        rank = _dot(tri, tie.astype(BF16)) + offs
        picked = jnp.where(s > thr, 1.0, jnp.where(rank < need, tie, 0.0))
        bias = jnp.where(picked > 0.0, 0.0, NEG_BIG)
        bias2 = jnp.concatenate([bias, bias], axis=1)
        new_mx = []
        for n in range(B_KV_HEADS):
            kblk = (n // per_block) * LANES
            k_t = lane_cat([k_ref[g, pl.ds(k0, KEY_TILE), kblk:kblk + LANES]
                            for g in range(groups)])
            lg = _dot_nt(k_t, q_pairs[n]) + bias2
            logit_scr[n, pl.ds(k0, KEY_TILE), :] = lg
            new_mx.append(jnp.maximum(mx[n], _fold8(lg, jnp.maximum)))
        offs = offs + jnp.sum(_fold8(tie, jnp.add), axis=0, keepdims=True)
        return offs, tuple(new_mx)

    mx0 = tuple(jnp.full((SUBLANES, 2 * qb), NEG_BIG, F32) for _ in range(B_KV_HEADS))
    _, mx = tile_loop(logit_tile, (jnp.zeros((1, qb), F32), mx0))
    mx = [jnp.max(m, axis=0, keepdims=True) for m in mx]

    acc_scr[...] = jnp.zeros_like(acc_scr)

    def pv_tile(j, den):
        k0 = tile_start(j)
        new_den = []
        for n in range(B_KV_HEADS):
            p = jnp.exp2(logit_scr[n, pl.ds(k0, KEY_TILE), :] - mx[n])
            new_den.append(den[n] + _fold8(p, jnp.add))
            vts = [vt_ref[g, n * B_HD:(n + 1) * B_HD, pl.ds(k0, KEY_TILE)] for g in range(groups)]
            vt = jnp.concatenate(vts, axis=0) if groups > 1 else vts[0]
            acc_scr[n] += _dot(vt, p.astype(BF16))
        return tuple(new_den)

    den0 = tuple(jnp.zeros((SUBLANES, 2 * qb), F32) for _ in range(B_KV_HEADS))
    den = tile_loop(pv_tile, den0)
    lane_group = (lax.broadcasted_iota(jnp.int32, (1, 2 * qb), 1) % qb) // gq
    rows = []
    for n in range(B_KV_HEADS):
        acc = acc_scr[n]
        o2 = acc[0:B_HD]
        for g in range(1, groups):
            o2 = jnp.where(lane_group == g, acc[g * B_HD:(g + 1) * B_HD], o2)
        o2 = o2 / jnp.sum(den[n], axis=0, keepdims=True)
        rows += [o2[:, :qb], o2[:, qb:]]
    o_ref[...] = jnp.concatenate(rows, axis=0).T.astype(BF16).reshape(groups, gq, B_WIDTH)


def _dsa_call(qi, wi, qbs, k_bf, vt, ki_bf, tk, pos0):
    b, t, _ = qi.shape
    if t % DSA_QUERIES == 0:
        groups, gq = 1, DSA_QUERIES
    else:
        assert LANES % t == 0 and b % (LANES // t) == 0
        groups, gq = LANES // t, t
    qb = groups * gq
    tkp = ki_bf.shape[1]
    topk = min(TOPK_MAX, tk // 4)
    assert topk <= KEY_TILE
    wit = jnp.swapaxes(wi, 1, 2)
    tri = jnp.asarray(np.tril(np.ones((KEY_TILE, KEY_TILE), np.float32), -1), BF16)

    def q_spec(w):
        return pl.BlockSpec((groups, gq, w), lambda i, j: (i, j, 0))

    def kv_spec(rows, cols):
        return pl.BlockSpec((groups, rows, cols), lambda i, j: (i, 0, 0))

    return pl.pallas_call(
        functools.partial(_dsa_kernel, groups=groups, gq=gq, qreal=gq, tk=tk, topk=topk,
                          pos0=pos0),
        grid=(b // groups, t // gq),
        in_specs=[q_spec(IDX_WIDTH),
                  pl.BlockSpec((groups, IDX_HEADS, gq), lambda i, j: (i, 0, j)),
                  q_spec(B_WIDTH),
                  kv_spec(tkp, IDX_DIM), kv_spec(tkp, KV_WIDTH), kv_spec(KV_WIDTH, tkp),
                  pl.BlockSpec((KEY_TILE, KEY_TILE), lambda i, j: (0, 0))],
        out_specs=q_spec(B_WIDTH),
        out_shape=jax.ShapeDtypeStruct((b, t, B_WIDTH), BF16),
        scratch_shapes=[pltpu.VMEM((tkp, qb), F32),
                        pltpu.VMEM((tkp, qb), BF16),
                        pltpu.VMEM((B_KV_HEADS, tkp, 2 * qb), F32),
                        pltpu.VMEM((B_KV_HEADS, groups * B_HD, 2 * qb), F32)],
        compiler_params=pltpu.CompilerParams(
            dimension_semantics=("parallel", "parallel"),
            vmem_limit_bytes=VMEM_LIMIT_BYTES),
        name="dsa",
    )(qi, wit, qbs, ki_bf, k_bf, vt, tri)


def _out_kernel(x_ref, oa_ref, ob_ref, mod_ref, n2_ref, nf_ref, wo_ref, w1_ref, w2_ref,
                y_ref, *, bb, tt):
    rows = bb * tt
    x = x_ref[...]
    mod = mod_ref[...]
    g1 = mod[:, :, 2 * D_MODEL:3 * D_MODEL]
    sh2 = mod[:, :, 3 * D_MODEL:4 * D_MODEL]
    sc2 = mod[:, :, 4 * D_MODEL:5 * D_MODEL]
    g2 = mod[:, :, 5 * D_MODEL:6 * D_MODEL]
    oa = oa_ref[...].reshape(rows, A_WIDTH)
    ob = ob_ref[...].reshape(rows, B_WIDTH)
    mix = _dot(oa, wo_ref[0:A_WIDTH, :]) + _dot(ob, wo_ref[A_WIDTH:A_WIDTH + B_WIDTH, :])
    x = x + g1 * mix.reshape(bb, tt, D_MODEL)
    h2 = (_rms(x) * n2_ref[...]) * (1.0 + sc2) + sh2
    u = _dot(h2.reshape(rows, D_MODEL).astype(BF16), w1_ref[...])
    r = jnp.square(jnp.maximum(u, 0.0)).astype(BF16)
    x = x + g2 * _dot(r, w2_ref[...]).reshape(bb, tt, D_MODEL)
    y_ref[...] = _rms(x) * nf_ref[...]


def _out_call(x, oa, ob, mod, norm2, norm_f, wo_bf, w1_bf, w2_bf, bb, tt):
    b, t, d = x.shape

    def act_spec(w):
        return pl.BlockSpec((bb, tt, w), lambda i, j: (i, j, 0))

    def const_spec(shape):
        zeros = (0,) * len(shape)
        return pl.BlockSpec(shape, lambda i, j: zeros, pipeline_mode=pl.Buffered(1))

    return pl.pallas_call(
        functools.partial(_out_kernel, bb=bb, tt=tt),
        grid=(b // bb, t // tt),
        in_specs=[act_spec(d), act_spec(A_WIDTH), act_spec(B_WIDTH),
                  pl.BlockSpec((bb, 1, 6 * d), lambda i, j: (i, 0, 0)),
                  const_spec((1, 1, d)), const_spec((1, 1, d)),
                  const_spec(wo_bf.shape), const_spec(w1_bf.shape), const_spec(w2_bf.shape)],
        out_specs=act_spec(d),
        out_shape=jax.ShapeDtypeStruct((b, t, d), F32),
        compiler_params=pltpu.CompilerParams(
            dimension_semantics=("parallel", "parallel"),
            vmem_limit_bytes=VMEM_LIMIT_BYTES),
        name="out",
    )(x, oa, ob, mod.reshape(b, 1, 6 * d), norm2.reshape(1, 1, d), norm_f.reshape(1, 1, d),
      wo_bf, w1_bf, w2_bf)


def _layer(x, mod, pos0, s0, k_past, v_past, ki_past, weights, bb, tt, cc):
    norm1, w_in_bf, lb, g_norm, wo_bf, norm2, w1_bf, w2_bf, norm_f = weights
    b, t, _ = x.shape
    pos = pos0 + jnp.arange(t)
    outs = _inproj_call(x, mod, norm1, w_in_bf, pos, bb, tt, dsa_layouts=k_past is None)
    hg, qbs, k_new, v_new, qi, ki_new, wi = outs[:7]
    oa, s_new = _hgrn_call(hg, lb, g_norm, s0, cc)
    if k_past is None:
        assert t % KEY_TILE == 0
        tk = t
        k_bf, vt, ki_bf = outs[7:]
    else:
        n_past = k_past.shape[1]
        tk = n_past + t
        tkp = -(-tk // KEY_TILE) * KEY_TILE

        def cat(past, new, axis):
            pads = [(0, 0)] * 3
            pads[axis] = (0, tkp - tk)
            return jnp.pad(jnp.concatenate([past.astype(BF16), new.astype(BF16)], axis=axis), pads)

        k_bf = cat(k_past.reshape(b, n_past, KV_WIDTH), k_new, 1)
        vt = cat(jnp.transpose(v_past, (0, 2, 3, 1)).reshape(b, KV_WIDTH, n_past),
                 jnp.swapaxes(v_new, 1, 2), 2)
        ki_bf = cat(ki_past, ki_new, 1)
    ob = _dsa_call(qi, wi, qbs, k_bf, vt, ki_bf, tk, pos0)
    y = _out_call(x, oa, ob, mod, norm2, norm_f, wo_bf, w1_bf, w2_bf, bb, tt)
    return (y, k_new.reshape(b, t, B_KV_HEADS, B_HD), v_new.reshape(b, t, B_KV_HEADS, B_HD),
            ki_new, s_new)


def _tiling(b, t):
    if t >= ROW_TILE:
        assert t % ROW_TILE == 0
        bb, tt = 1, ROW_TILE
    else:
        bb, tt = min(b, ROW_TILE // t), t
        assert b % bb == 0
    return dict(bb=bb, tt=tt, cc=min(HGRN_CHUNK, t))


def kernel(x_prompt, x_sample, cache_k, cache_v, cache_k_idx, state_hgrn, c_prompt, c_sample,
           w_mod, b_mod, norm1, w_in, lb_logits, g_norm_a, w_out, norm2, w_ff1, w_ff2, norm_f):
    depth = w_in.shape[0]
    assert depth == 1, "kernel is written for the single-layer configuration"
    lb_all = jnp.cumsum(jax.nn.softmax(lb_logits.astype(F32), axis=0), axis=0)
    bp, tp, _ = x_prompt.shape
    bs, ts, _ = x_sample.shape
    past = cache_k.shape[2]
    l = 0
    mod = _mod_call(jnp.concatenate([c_prompt, c_sample], axis=0), w_mod[l], b_mod[l])
    w_in_bf = jnp.pad(w_in[l].astype(BF16), ((0, 0), (0, IN_WIDTH_PAD - IN_WIDTH)))
    weights = (norm1[l], w_in_bf, lb_all[l], g_norm_a[l], w_out[l].astype(BF16), norm2[l],
               w_ff1[l].astype(BF16), w_ff2[l].astype(BF16), norm_f)
    s0 = jnp.zeros((bp, A_HEADS, A_DK, A_DV), F32)
    yp, kp, vp, kip, sp = _layer(x_prompt, mod[:bp], 0, s0, None, None, None, weights,
                                 **_tiling(bp, tp))
    ys, ks, vs, kis, ss = _layer(x_sample, mod[bp:], past, state_hgrn[l], cache_k[l],
                                 cache_v[l], cache_k_idx[l], weights, **_tiling(bs, ts))
    return (yp, ys, kp[None], vp[None], kip[None], sp[None],
            ks[None], vs[None], kis[None], ss[None])
```

```python
import functools

import numpy as np
import jax
import jax.numpy as jnp
from jax import lax
from jax.experimental import pallas as pl
from jax.experimental.pallas import tpu as pltpu

D_MODEL = 1024
CHUNK = 64
A_HEADS = 4
A_DK = 128
A_DV = 128
A_WIDTH = A_HEADS * A_DV
B_HEADS = 8
B_KV_HEADS = 4
B_HD = 64
B_WIDTH = B_HEADS * B_HD
KV_WIDTH = B_KV_HEADS * B_HD
IDX_HEADS = 8
IDX_DIM = 64
IDX_WIDTH = IDX_HEADS * IDX_DIM
TOPK_MAX = 256
ROT_FRAC = 4
ROPE_THETA = 500000.0
D_FF = 4 * D_MODEL
EPS = 1e-6
IN_WIDTH = 4 * A_WIDTH + B_WIDTH + 2 * KV_WIDTH + IDX_WIDTH + IDX_DIM + IDX_HEADS

LANES = 128
SUBLANES = 8
BF16_ROWS = 16
KEY_TILE = 256
DSA_QUERIES = 256
ROW_TILE = 512
HGRN_CHUNK = 128
HGRN_CHUNKS_PER_STEP = 8
VMEM_LIMIT_BYTES = 56 * 1024 * 1024

F32 = jnp.float32
BF16 = jnp.bfloat16
INT_MIN = np.int32(-2 ** 31)
NEG_INF_PATTERN16 = 0x007F
NEG_BIG = -1e30
LOG2_E = 1.4426950408889634

OFF_HG = 0
OFF_QB = 4 * A_WIDTH
OFF_KB = OFF_QB + B_WIDTH
OFF_VB = OFF_KB + KV_WIDTH
OFF_QI = OFF_VB + KV_WIDTH
OFF_KI = OFF_QI + IDX_WIDTH
OFF_WI = OFF_KI + IDX_DIM


def _dot(a, b):
    return jnp.dot(a, b, preferred_element_type=F32)


def _dot_nt(a, b):
    return lax.dot_general(a, b, (((1,), (1,)), ((), ())), preferred_element_type=F32)


def _silu(x):
    return x * jax.nn.sigmoid(x)


def _rms(x):
    return x * lax.rsqrt(jnp.mean(jnp.square(x), axis=-1, keepdims=True) + EPS)


def _block_diag(x, width):
    zero = jnp.zeros((x.shape[0], width), x.dtype)
    return jnp.concatenate([jnp.concatenate([x[:, :width], zero], axis=1),
                            jnp.concatenate([zero, x[:, width:]], axis=1)], axis=0)


def _mod_kernel(c_ref, w_ref, b_ref, o_ref):
    a = _silu(c_ref[...])
    w = w_ref[...]
    a_hi = a.astype(BF16)
    a_lo = (a - a_hi.astype(F32)).astype(BF16)
    w_hi = w.astype(BF16)
    w_lo = (w - w_hi.astype(F32)).astype(BF16)
    o_ref[...] = _dot(a_hi, w_hi) + _dot(a_lo, w_hi) + _dot(a_hi, w_lo) + b_ref[...]


def _mod_call(c, w_mod, b_mod):
    rows, d = c.shape
    n = w_mod.shape[1]
    tn = 1024
    return pl.pallas_call(
        _mod_kernel,
        grid=(n // tn,),
        in_specs=[pl.BlockSpec((rows, d), lambda j: (0, 0)),
                  pl.BlockSpec((d, tn), lambda j: (0, j)),
                  pl.BlockSpec((1, tn), lambda j: (0, j))],
        out_specs=pl.BlockSpec((rows, tn), lambda j: (0, j)),
        out_shape=jax.ShapeDtypeStruct((rows, n), F32),
        compiler_params=pltpu.CompilerParams(vmem_limit_bytes=VMEM_LIMIT_BYTES),
        name="mod",
    )(c, w_mod, b_mod.reshape(1, n))


def _rope(x, cos, sin_lo, sin_hi):
    half = B_HD // ROT_FRAC // 2
    return (x * cos + pltpu.roll(x, half, 1) * sin_hi
            + pltpu.roll(x, LANES - half, 1) * sin_lo)


def _inproj_kernel(x_ref, mod_ref, n1_ref, w_ref, wt_ref, cos_ref, slo_ref, shi_ref,
                   hg_ref, qb_ref, k_ref, v_ref, qi_ref, ki_ref, wi_ref, *dsa_refs, bb, tt):
    rows = bb * tt
    x = x_ref[...]
    mod = mod_ref[...]
    sh1 = mod[:, :, 0:D_MODEL]
    sc1 = mod[:, :, D_MODEL:2 * D_MODEL]
    h = (_rms(x) * n1_ref[...]) * (1.0 + sc1) + sh1
    h = h.reshape(rows, D_MODEL).astype(BF16)
    z_att = _dot(h, w_ref[:, OFF_QB:])
    z_tail = _dot(h, wt_ref[...])
    cos, slo, shi = cos_ref[...], slo_ref[...], shi_ref[...]

    def cols(off, width):
        return z_att[:, off - OFF_QB:off - OFF_QB + width]

    def rope_cols(off, width):
        return [_rope(cols(off + j, LANES), cos, slo, shi) for j in range(0, width, LANES)]

    scale = B_HD ** -0.5 * LOG2_E
    qb = jnp.concatenate(rope_cols(OFF_QB, B_WIDTH), axis=1) * scale
    qb_ref[...] = qb.astype(BF16).reshape(bb, tt, B_WIDTH)
    kb = jnp.concatenate(rope_cols(OFF_KB, KV_WIDTH), axis=1)
    k_ref[...] = kb.reshape(bb, tt, KV_WIDTH)
    vb = cols(OFF_VB, KV_WIDTH)
    v_ref[...] = vb.reshape(bb, tt, KV_WIDTH)
    qi = jnp.concatenate(rope_cols(OFF_QI, IDX_WIDTH), axis=1)
    qi_ref[...] = qi.astype(BF16).reshape(bb, tt, IDX_WIDTH)
    last = _rope(z_tail, cos, slo, shi)
    ki_ref[...] = last[:, 0:IDX_DIM].reshape(bb, tt, IDX_DIM)
    wi = z_tail[:, OFF_WI - OFF_KI:OFF_WI - OFF_KI + IDX_HEADS] * (IDX_WIDTH ** -0.5)
    wi_ref[...] = wi.reshape(bb, tt, IDX_HEADS)
    if dsa_refs:
        kbf_ref, vt_ref, kibf_ref = dsa_refs
        kbf_ref[0] = kb.astype(BF16)
        vt_ref[0] = vb.T.astype(BF16)
        kibf_ref[0] = last[:, 0:IDX_DIM].astype(BF16)
    hg_ref[...] = _dot(h, w_ref[:, OFF_HG:OFF_QB]).reshape(bb, tt, 4 * A_WIDTH)


def _rope_tables(pos, reps):
    rot = B_HD // ROT_FRAC
    half = rot // 2
    inv = jnp.power(ROPE_THETA, -jnp.arange(half, dtype=F32) * (2.0 / rot))
    ang = pos.astype(F32)[:, None] * inv[None, :]
    cos, sin = jnp.cos(ang), jnp.sin(ang)
    t = pos.shape[0]
    ones = jnp.ones((t, B_HD - rot), F32)
    zeros = jnp.zeros((t, B_HD - rot), F32)
    zh = jnp.zeros((t, half), F32)
    cos_h = jnp.concatenate([cos, cos, ones], axis=1)
    slo_h = jnp.concatenate([-sin, zh, zeros], axis=1)
    shi_h = jnp.concatenate([zh, sin, zeros], axis=1)
    per = LANES // B_HD
    return tuple(jnp.tile(a, (reps, per)) for a in (cos_h, slo_h, shi_h))


def _inproj_call(x, mod, norm1, w_main, w_tail, pos, bb, tt, dsa_layouts):
    b, t, d = x.shape
    cos, slo, shi = _rope_tables(pos, bb)
    rows = bb * tt
    if bb == 1:
        tab_spec = pl.BlockSpec((tt, LANES), lambda i, j: (j, 0))
    else:
        tab_spec = pl.BlockSpec((rows, LANES), lambda i, j: (0, 0))

    def act_spec(w):
        return pl.BlockSpec((bb, tt, w), lambda i, j: (i, j, 0))

    def out(w, dt):
        return jax.ShapeDtypeStruct((b, t, w), dt)

    out_specs = [act_spec(4 * A_WIDTH), act_spec(B_WIDTH), act_spec(KV_WIDTH),
                 act_spec(KV_WIDTH), act_spec(IDX_WIDTH), act_spec(IDX_DIM),
                 act_spec(IDX_HEADS)]
    out_shape = [out(4 * A_WIDTH, F32), out(B_WIDTH, BF16), out(KV_WIDTH, F32),
                 out(KV_WIDTH, F32), out(IDX_WIDTH, BF16), out(IDX_DIM, F32),
                 out(IDX_HEADS, F32)]
    if dsa_layouts:
        assert bb == 1
        out_specs += [act_spec(KV_WIDTH), pl.BlockSpec((1, KV_WIDTH, tt), lambda i, j: (i, 0, j)),
                      act_spec(IDX_DIM)]
        out_shape += [out(KV_WIDTH, BF16), jax.ShapeDtypeStruct((b, KV_WIDTH, t), BF16),
                      out(IDX_DIM, BF16)]

    return pl.pallas_call(
        functools.partial(_inproj_kernel, bb=bb, tt=tt),
        grid=(b // bb, t // tt),
        in_specs=[act_spec(d),
                  pl.BlockSpec((bb, 1, 6 * d), lambda i, j: (i, 0, 0)),
                  pl.BlockSpec((1, 1, d), lambda i, j: (0, 0, 0)),
                  pl.BlockSpec(w_main.shape, lambda i, j: (0, 0)),
                  pl.BlockSpec(w_tail.shape, lambda i, j: (0, 0)),
                  tab_spec, tab_spec, tab_spec],
        out_specs=out_specs,
        out_shape=out_shape,
        compiler_params=pltpu.CompilerParams(
            dimension_semantics=("parallel", "parallel"),
            vmem_limit_bytes=VMEM_LIMIT_BYTES),
        name="inproj",
    )(x, mod.reshape(b, 1, 6 * d), norm1.reshape(1, 1, d), w_main, w_tail, cos, slo, shi)


def _hgrn_tables(cc):
    nlev = int(np.log2(cc))
    t = np.arange(cc)[:, None]
    u = np.arange(cc)[None, :]
    mats = [(u <= t).astype(np.float32)]
    for l in range(nlev):
        m = cc >> (l + 1)
        if m < SUBLANES:
            ref = (t // (2 * m)) * (2 * m) + m - 1
            qside = ((t // m) % 2) == 1
            seg = np.where(qside, (u > ref) & (u <= t), (u > t) & (u <= ref))
            mats.append(-seg.astype(np.float32))
    w = np.concatenate(mats, axis=0)
    w = np.concatenate([w] * 3, axis=1)
    lvl = np.full((cc, cc), -1, np.int32)
    for l in range(nlev):
        m = cc >> (l + 1)
        same_parent = (t // (2 * m)) == (u // (2 * m))
        lvl[same_parent & ((t // m) % 2 == 1) & ((u // m) % 2 == 0)] = l
    lvl[np.arange(cc), np.arange(cc)] = nlev
    return jnp.asarray(w, BF16), jnp.asarray(lvl), nlev


def _level_exponent(cum_ref, cols, cum, m):
    cc, n = cum.shape
    parts = []
    for p in range(0, cc, 2 * m):
        ref = jnp.broadcast_to(cum_ref[p + m - 1:p + m, cols], (m, n))
        parts += [cum[p:p + m] - ref, ref - cum[p + m:p + 2 * m]]
    return jnp.concatenate(parts, axis=0)


def _hgrn_kernel(q_ref, f_ref, i_ref, g_ref, lb_ref, gn_ref, s0_ref, w_ref, lvl_ref,
                 o_ref, s_out_ref, st_scr, cum_scr, *, cc, nc, nlev):
    ci = pl.program_id(1)

    @pl.when(ci == 0)
    def _():
        for h in range(A_HEADS):
            st_scr[h] = s0_ref[0, h].T

    row = lax.broadcasted_iota(jnp.int32, (cc, 2 * A_DK), 0)
    qsides = [((row // (cc >> (l + 1))) % 2) == 1 for l in range(nlev)]
    n_small = sum(1 for l in range(nlev) if (cc >> (l + 1)) < SUBLANES)
    qsides16 = [jnp.where(m, 1.0, 0.0).astype(BF16) > 0 for m in qsides]
    lvl = jnp.concatenate([lvl_ref[...]] * 2, axis=1)
    lvl_masks = [lvl == l for l in range(nlev + 1)]
    w = w_ref[...]
    lb = lb_ref[...]
    f_all = lb + (1.0 - lb) * jax.nn.sigmoid(f_ref[0])
    nl = -jnp.log2(f_all)
    nl_hi = nl.astype(BF16)
    rest = nl - nl_hi.astype(F32)
    nl_mid = rest.astype(BF16)
    nl_lo = (rest - nl_mid.astype(F32)).astype(BF16)
    for c in range(nc):
        rs = slice(c * cc, (c + 1) * cc)
        cum_scr[...] = _dot(w[0:cc], jnp.concatenate([nl_hi[rs], nl_mid[rs], nl_lo[rs]], axis=0))
        sums = _dot(w[cc:, 0:2 * cc], jnp.concatenate([nl_hi[rs], nl_mid[rs]], axis=0))
        for hp in range(A_HEADS // 2):
            h0, h1 = 2 * hp, 2 * hp + 1
            sl = slice(h0 * A_DK, (h1 + 1) * A_DK)
            q = _silu(q_ref[0, rs, sl])
            kk = 1.0 - f_all[rs, sl]
            v = i_ref[0, rs, sl]
            cum = cum_scr[:, sl]
            q16, k16, v16 = q.astype(BF16), kk.astype(BF16), v.astype(BF16)
            attn = jnp.zeros((cc, 2 * cc), F32)
            for l in range(nlev):
                m = cc >> (l + 1)
                if m >= SUBLANES:
                    ex = _level_exponent(cum_scr, sl, cum, m)
                else:
                    small = l - (nlev - n_small)
                    ex = sums[small * cc:(small + 1) * cc, sl]
                e16 = jnp.exp2(ex).astype(BF16)
                if m % BF16_ROWS == 0:
                    qk = jnp.concatenate([(q16 if (r // m) % 2 else k16)[r:r + m]
                                          for r in range(0, cc, m)], axis=0)
                else:
                    qk = jnp.where(qsides16[l], q16, k16)
                xl = qk * e16
                attn = jnp.where(lvl_masks[l], _dot_nt(xl, _block_diag(xl, A_DK)), attn)
            attn = jnp.where(lvl_masks[nlev], _dot_nt(q16, _block_diag(k16, A_DK)), attn)
            last = jnp.broadcast_to(cum_scr[cc - 1:cc, sl], (cc, 2 * A_DK))
            qg = (q * jnp.exp2(-cum)).astype(BF16)
            kg = (kk * jnp.exp2(cum - last)).astype(BF16)
            st = jnp.concatenate([st_scr[h0], st_scr[h1]], axis=1)
            o = (_dot(attn.astype(BF16), _block_diag(v16, A_DV))
                 + _dot_nt(qg, _block_diag(st.astype(BF16), A_DK)))
            dec = jnp.exp2(-cum[cc - 1:cc, :])
            upd = _dot(v.T.astype(BF16), kg)
            st_scr[h0] = st[:, :A_DK] * dec[:, :A_DK] + upd[:A_DV, :A_DK]
            st_scr[h1] = st[:, A_DK:] * dec[:, A_DK:] + upd[A_DV:, A_DK:]
            for i, h in enumerate((h0, h1)):
                hs = slice(h * A_DV, (h + 1) * A_DV)
                y = _rms(o[:, i * A_DV:(i + 1) * A_DV]) * gn_ref[:, hs]
                o_ref[0, rs, hs] = (y * _silu(g_ref[0, rs, hs])).astype(BF16)

    @pl.when(ci == pl.num_programs(1) - 1)
    def _():
        for h in range(A_HEADS):
            s_out_ref[0, h] = st_scr[h].T


def _hgrn_call(hg, lb, g_norm, s0, cc):
    b, t, _ = hg.shape
    w, lvl, nlev = _hgrn_tables(cc)
    nc = HGRN_CHUNKS_PER_STEP if t % (HGRN_CHUNKS_PER_STEP * cc) == 0 else 1
    rows = nc * cc

    def part(p):
        return pl.BlockSpec((1, rows, A_WIDTH), lambda i, j, p=p: (i, j, p))

    return pl.pallas_call(
        functools.partial(_hgrn_kernel, cc=cc, nc=nc, nlev=nlev),
        grid=(b, t // rows),
        in_specs=[part(0), part(1), part(2), part(3),
                  pl.BlockSpec((1, A_WIDTH), lambda i, j: (0, 0)),
                  pl.BlockSpec((1, A_WIDTH), lambda i, j: (0, 0)),
                  pl.BlockSpec((1, A_HEADS, A_DK, A_DV), lambda i, j: (i, 0, 0, 0)),
                  pl.BlockSpec(w.shape, lambda i, j: (0, 0)),
                  pl.BlockSpec(lvl.shape, lambda i, j: (0, 0))],
        out_specs=[pl.BlockSpec((1, rows, A_WIDTH), lambda i, j: (i, j, 0)),
                   pl.BlockSpec((1, A_HEADS, A_DK, A_DV), lambda i, j: (i, 0, 0, 0))],
        out_shape=[jax.ShapeDtypeStruct((b, t, A_WIDTH), BF16),
                   jax.ShapeDtypeStruct((b, A_HEADS, A_DK, A_DV), F32)],
        scratch_shapes=[pltpu.VMEM((A_HEADS, A_DV, A_DK), F32),
                        pltpu.VMEM((cc, A_WIDTH), F32)],
        compiler_params=pltpu.CompilerParams(
            dimension_semantics=("parallel", "arbitrary"),
            vmem_limit_bytes=VMEM_LIMIT_BYTES),
        name="hgrn",
    )(hg, hg, hg, hg, lb.reshape(1, A_WIDTH), g_norm.reshape(1, A_WIDTH), s0, w, lvl)


def _fold8(x, op):
    parts = [x[r:r + SUBLANES] for r in range(0, x.shape[0], SUBLANES)]
    while len(parts) > 1:
        parts = [op(parts[i], parts[i + 1]) for i in range(0, len(parts) - 1, 2)] + (
            [parts[-1]] if len(parts) % 2 else [])
    return parts[0]


def _grouped_rhs(xs, heads, width, slot, slots):
    groups = len(xs)
    zero = jnp.zeros((xs[0].shape[0], width), xs[0].dtype)
    rows = []
    for h in heads:
        for g, x in enumerate(xs):
            parts = [zero] * (groups * slots)
            parts[g * slots + slot] = x[:, h * width:(h + 1) * width]
            rows.append(jnp.concatenate(parts, axis=1) if len(parts) > 1 else parts[0])
    return jnp.concatenate(rows, axis=0)


def _dsa_kernel(qi_ref, wit_ref, qb_ref, ki_ref, k_ref, vt_ref, tri_ref, o_ref,
                score_scr, score16_scr, logit_scr, acc_scr, *, groups, gq, qreal, tk, topk, pos0):
    qb = groups * gq
    blk = pl.program_id(1)
    last_pos = pos0 + (blk + 1) * qreal - 1
    extent = jnp.minimum((last_pos // CHUNK + 1) * CHUNK, tk)
    ntile = (extent + KEY_TILE - 1) // KEY_TILE
    lane = lax.broadcasted_iota(jnp.int32, (1, qb), 1)
    qpos = pos0 + blk * qreal + lane % gq
    key_end = jnp.minimum((qpos // CHUNK + 1) * CHUNK, tk)
    neg_inf = jnp.float32(-jnp.inf)

    def lane_cat(parts):
        return jnp.concatenate(parts, axis=1) if len(parts) > 1 else parts[0]

    qis = [qi_ref[g] for g in range(groups)]
    wit = lane_cat([wit_ref[g] for g in range(groups)])
    qi_pairs = [_grouped_rhs(qis, (2 * p, 2 * p + 1), IDX_DIM, 0, 1)
                for p in range(IDX_HEADS // 2)]

    def tile_start(j):
        return pl.multiple_of(j * KEY_TILE, KEY_TILE)

    def tile_loop(body, init):
        def run(first, count, c):
            for u in range(count):
                c = body(first + u, c)
            return c
        carry = lax.fori_loop(0, ntile // 4, lambda i, c: run(4 * i, 4, c), init)
        done = (ntile // 4) * 4
        carry = lax.cond((ntile & 2) != 0, lambda c: run(done, 2, c), lambda c: c, carry)
        return lax.cond((ntile & 1) != 0, lambda c: body(ntile - 1, c), lambda c: c, carry)

    def score_tile(j, carry):
        k0 = tile_start(j)
        ki_t = lane_cat([ki_ref[g, pl.ds(k0, KEY_TILE), :] for g in range(groups)])
        acc = jnp.zeros((KEY_TILE, qb), F32)
        for p in range(IDX_HEADS // 2):
            s2 = jnp.maximum(_dot_nt(ki_t, qi_pairs[p]), 0.0)
            acc = acc + wit[2 * p:2 * p + 1, :] * s2[:, :qb]
            acc = acc + wit[2 * p + 1:2 * p + 2, :] * s2[:, qb:]
        kidx = k0 + lax.broadcasted_iota(jnp.int32, (KEY_TILE, qb), 0)
        masked = jnp.where(kidx < key_end, acc, neg_inf)
        score_scr[pl.ds(k0, KEY_TILE), :] = masked
        score16_scr[pl.ds(k0, KEY_TILE), :] = masked.astype(BF16)
        return carry

    tile_loop(score_tile, 0)

    def count(pred_fn):
        def body(j, acc):
            s = score_scr[pl.ds(tile_start(j), KEY_TILE), :]
            return acc + _fold8(jnp.where(pred_fn(s), 1.0, 0.0), jnp.add)
        acc = tile_loop(body, jnp.zeros((SUBLANES, qb), F32))
        return jnp.sum(acc, axis=0, keepdims=True)

    def count16(cand):
        one, zero = jnp.ones((), BF16), jnp.zeros((), BF16)

        def body(j, acc):
            r = score16_scr[pl.ds(tile_start(j), KEY_TILE), :]
            hit = jnp.where(r >= cand, one, zero)
            parts = [hit[i:i + BF16_ROWS] for i in range(0, KEY_TILE, BF16_ROWS)]
            while len(parts) > 1:
                parts = [parts[i] + parts[i + 1] for i in range(0, len(parts), 2)]
            return acc + parts[0]
        acc = tile_loop(body, jnp.zeros((BF16_ROWS, qb), BF16))
        return jnp.sum(acc.astype(F32), axis=0, keepdims=True)

    def order_key(x):
        return x ^ ((x >> 31) & np.int32(0x7FFFFFFF))

    def decode16(t16):
        b16 = jnp.where(t16 >= 0x8000, t16 & 0x7FFF, ~t16 & 0xFFFF)
        return pltpu.bitcast(b16 << 16, F32)

    def bisect16(i, t16):
        cand16 = t16 | (jnp.int32(1) << (15 - i))
        cnt = count16(decode16(cand16).astype(BF16))
        return jnp.where(cnt >= topk, cand16, t16)

    search = extent > topk
    t16 = lax.fori_loop(0, jnp.where(search, 16, 0), bisect16, jnp.zeros((1, qb), jnp.int32))
    few = t16 <= NEG_INF_PATTERN16
    head = decode16(jnp.maximum(t16, NEG_INF_PATTERN16))
    base = order_key(pltpu.bitcast(head, jnp.int32)) - (1 << 15) - 1

    def bisect17(i, d):
        cand_d = d | (jnp.int32(1) << (16 - i))
        cand = pltpu.bitcast(order_key(base + cand_d), F32)
        cnt = count(lambda s: s >= cand)
        return jnp.where(cnt >= topk, cand_d, d)

    d = lax.fori_loop(0, jnp.where(search, 17, 0), bisect17, jnp.zeros((1, qb), jnp.int32))
    thr = jnp.where(few, neg_inf, pltpu.bitcast(order_key(base + d), F32))
    need = topk - count(lambda s: s > thr)
    need = jnp.where(few, 0.0, need)

    qqs = [qb_ref[g] for g in range(groups)]
    per_q = B_HEADS // B_KV_HEADS
    per_block = LANES // B_HD
    q_pairs = [_grouped_rhs(qqs, (n * per_q, n * per_q + 1), B_HD, n % per_block, per_block)
               for n in range(B_KV_HEADS)]
    tri = tri_ref[...]

    def logit_tile(j, carry):
        offs, mx = carry
        k0 = tile_start(j)
        s = score_scr[pl.ds(k0, KEY_TILE), :]
        tie = jnp.where(s == thr, 1.0, 0.0)
        rank = _dot(tri, tie.astype(BF16)) + offs
        picked = jnp.where(s > thr, 1.0, jnp.where(rank < need, tie, 0.0))
        bias = jnp.where(picked > 0.0, 0.0, NEG_BIG)
        bias2 = jnp.concatenate([bias, bias], axis=1)
        new_mx = []
        for n in range(B_KV_HEADS):
            kblk = (n // per_block) * LANES
            k_t = lane_cat([k_ref[g, pl.ds(k0, KEY_TILE), kblk:kblk + LANES]
                            for g in range(groups)])
            lg = _dot_nt(k_t, q_pairs[n]) + bias2
            logit_scr[n, pl.ds(k0, KEY_TILE), :] = lg
            new_mx.append(jnp.maximum(mx[n], _fold8(lg, jnp.maximum)))
        offs = offs + jnp.sum(_fold8(tie, jnp.add), axis=0, keepdims=True)
        return offs, tuple(new_mx)

    mx0 = tuple(jnp.full((SUBLANES, 2 * qb), NEG_BIG, F32) for _ in range(B_KV_HEADS))
    _, mx = tile_loop(logit_tile, (jnp.zeros((1, qb), F32), mx0))
    mx = [jnp.max(m, axis=0, keepdims=True) for m in mx]

    acc_scr[...] = jnp.zeros_like(acc_scr)

    def pv_tile(j, den):
        k0 = tile_start(j)
        new_den = []
        for n in range(B_KV_HEADS):
            p = jnp.exp2(logit_scr[n, pl.ds(k0, KEY_TILE), :] - mx[n])
            new_den.append(den[n] + _fold8(p, jnp.add))
            vts = [vt_ref[g, n * B_HD:(n + 1) * B_HD, pl.ds(k0, KEY_TILE)] for g in range(groups)]
            vt = jnp.concatenate(vts, axis=0) if groups > 1 else vts[0]
            acc_scr[n] += _dot(vt, p.astype(BF16))
        return tuple(new_den)

    den0 = tuple(jnp.zeros((SUBLANES, 2 * qb), F32) for _ in range(B_KV_HEADS))
    den = tile_loop(pv_tile, den0)
    lane_group = (lax.broadcasted_iota(jnp.int32, (1, 2 * qb), 1) % qb) // gq
    rows = []
    for n in range(B_KV_HEADS):
        acc = acc_scr[n]
        o2 = acc[0:B_HD]
        for g in range(1, groups):
            o2 = jnp.where(lane_group == g, acc[g * B_HD:(g + 1) * B_HD], o2)
        o2 = o2 / jnp.sum(den[n], axis=0, keepdims=True)
        rows += [o2[:, :qb], o2[:, qb:]]
    o_ref[...] = jnp.concatenate(rows, axis=0).T.astype(BF16).reshape(groups, gq, B_WIDTH)


def _dsa_call(qi, wi, qbs, k_bf, vt, ki_bf, tk, pos0):
    b, t, _ = qi.shape
    if t % DSA_QUERIES == 0:
        groups, gq = 1, DSA_QUERIES
    else:
        assert LANES % t == 0 and b % (LANES // t) == 0
        groups, gq = LANES // t, t
    qb = groups * gq
    tkp = ki_bf.shape[1]
    topk = min(TOPK_MAX, tk // 4)
    assert topk <= KEY_TILE
    wit = jnp.swapaxes(wi, 1, 2)
    tri = jnp.asarray(np.tril(np.ones((KEY_TILE, KEY_TILE), np.float32), -1), BF16)

    def q_spec(w):
        return pl.BlockSpec((groups, gq, w), lambda i, j: (i, j, 0))

    def kv_spec(rows, cols):
        return pl.BlockSpec((groups, rows, cols), lambda i, j: (i, 0, 0))

    return pl.pallas_call(
        functools.partial(_dsa_kernel, groups=groups, gq=gq, qreal=gq, tk=tk, topk=topk,
                          pos0=pos0),
        grid=(b // groups, t // gq),
        in_specs=[q_spec(IDX_WIDTH),
                  pl.BlockSpec((groups, IDX_HEADS, gq), lambda i, j: (i, 0, j)),
                  q_spec(B_WIDTH),
                  kv_spec(tkp, IDX_DIM), kv_spec(tkp, KV_WIDTH), kv_spec(KV_WIDTH, tkp),
                  pl.BlockSpec((KEY_TILE, KEY_TILE), lambda i, j: (0, 0))],
        out_specs=q_spec(B_WIDTH),
        out_shape=jax.ShapeDtypeStruct((b, t, B_WIDTH), BF16),
        scratch_shapes=[pltpu.VMEM((tkp, qb), F32),
                        pltpu.VMEM((tkp, qb), BF16),
                        pltpu.VMEM((B_KV_HEADS, tkp, 2 * qb), F32),
                        pltpu.VMEM((B_KV_HEADS, groups * B_HD, 2 * qb), F32)],
        compiler_params=pltpu.CompilerParams(
            dimension_semantics=("parallel", "parallel"),
            vmem_limit_bytes=VMEM_LIMIT_BYTES),
        name="dsa",
    )(qi, wit, qbs, ki_bf, k_bf, vt, tri)


def _out_kernel(x_ref, oa_ref, ob_ref, mod_ref, n2_ref, nf_ref, wo_ref, w1_ref, w2_ref,
                y_ref, *, bb, tt):
    rows = bb * tt
    x = x_ref[...]
    mod = mod_ref[...]
    g1 = mod[:, :, 2 * D_MODEL:3 * D_MODEL]
    sh2 = mod[:, :, 3 * D_MODEL:4 * D_MODEL]
    sc2 = mod[:, :, 4 * D_MODEL:5 * D_MODEL]
    g2 = mod[:, :, 5 * D_MODEL:6 * D_MODEL]
    oa = oa_ref[...].reshape(rows, A_WIDTH)
    ob = ob_ref[...].reshape(rows, B_WIDTH)
    mix = _dot(oa, wo_ref[0:A_WIDTH, :]) + _dot(ob, wo_ref[A_WIDTH:A_WIDTH + B_WIDTH, :])
    x = x + g1 * mix.reshape(bb, tt, D_MODEL)
    h2 = (_rms(x) * n2_ref[...]) * (1.0 + sc2) + sh2
    u = _dot(h2.reshape(rows, D_MODEL).astype(BF16), w1_ref[...])
    r = jnp.square(jnp.maximum(u, 0.0)).astype(BF16)
    x = x + g2 * _dot(r, w2_ref[...]).reshape(bb, tt, D_MODEL)
    y_ref[...] = _rms(x) * nf_ref[...]


def _out_call(x, oa, ob, mod, norm2, norm_f, wo_bf, w1_bf, w2_bf, bb, tt):
    b, t, d = x.shape

    def act_spec(w):
        return pl.BlockSpec((bb, tt, w), lambda i, j: (i, j, 0))

    def const_spec(shape):
        zeros = (0,) * len(shape)
        return pl.BlockSpec(shape, lambda i, j: zeros, pipeline_mode=pl.Buffered(1))

    return pl.pallas_call(
        functools.partial(_out_kernel, bb=bb, tt=tt),
        grid=(b // bb, t // tt),
        in_specs=[act_spec(d), act_spec(A_WIDTH), act_spec(B_WIDTH),
                  pl.BlockSpec((bb, 1, 6 * d), lambda i, j: (i, 0, 0)),
                  const_spec((1, 1, d)), const_spec((1, 1, d)),
                  const_spec(wo_bf.shape), const_spec(w1_bf.shape), const_spec(w2_bf.shape)],
        out_specs=act_spec(d),
        out_shape=jax.ShapeDtypeStruct((b, t, d), F32),
        compiler_params=pltpu.CompilerParams(
            dimension_semantics=("parallel", "parallel"),
            vmem_limit_bytes=VMEM_LIMIT_BYTES),
        name="out",
    )(x, oa, ob, mod.reshape(b, 1, 6 * d), norm2.reshape(1, 1, d), norm_f.reshape(1, 1, d),
      wo_bf, w1_bf, w2_bf)


def _layer(x, mod, pos0, s0, k_past, v_past, ki_past, weights, bb, tt, cc):
    norm1, w_main, w_tail, lb, g_norm, wo_bf, norm2, w1_bf, w2_bf, norm_f = weights
    b, t, _ = x.shape
    pos = pos0 + jnp.arange(t)
    outs = _inproj_call(x, mod, norm1, w_main, w_tail, pos, bb, tt, dsa_layouts=k_past is None)
    hg, qbs, k_new, v_new, qi, ki_new, wi = outs[:7]
    oa, s_new = _hgrn_call(hg, lb, g_norm, s0, cc)
    if k_past is None:
        assert t % KEY_TILE == 0
        tk = t
        k_bf, vt, ki_bf = outs[7:]
    else:
        n_past = k_past.shape[1]
        tk = n_past + t
        tkp = -(-tk // KEY_TILE) * KEY_TILE

        def cat(past, new, axis):
            pads = [(0, 0)] * 3
            pads[axis] = (0, tkp - tk)
            return jnp.pad(jnp.concatenate([past.astype(BF16), new.astype(BF16)], axis=axis), pads)

        k_bf = cat(k_past.reshape(b, n_past, KV_WIDTH), k_new, 1)
        vt = cat(jnp.transpose(v_past, (0, 2, 3, 1)).reshape(b, KV_WIDTH, n_past),
                 jnp.swapaxes(v_new, 1, 2), 2)
        ki_bf = cat(ki_past, ki_new, 1)
    ob = _dsa_call(qi, wi, qbs, k_bf, vt, ki_bf, tk, pos0)
    y = _out_call(x, oa, ob, mod, norm2, norm_f, wo_bf, w1_bf, w2_bf, bb, tt)
    return (y, k_new.reshape(b, t, B_KV_HEADS, B_HD), v_new.reshape(b, t, B_KV_HEADS, B_HD),
            ki_new, s_new)


def _tiling(b, t):
    if t >= ROW_TILE:
        assert t % ROW_TILE == 0
        bb, tt = 1, ROW_TILE
    else:
        bb, tt = min(b, ROW_TILE // t), t
        assert b % bb == 0
    return dict(bb=bb, tt=tt, cc=min(HGRN_CHUNK, t))


def kernel(x_prompt, x_sample, cache_k, cache_v, cache_k_idx, state_hgrn, c_prompt, c_sample,
           w_mod, b_mod, norm1, w_in, lb_logits, g_norm_a, w_out, norm2, w_ff1, w_ff2, norm_f):
    depth = w_in.shape[0]
    assert depth == 1, "kernel is written for the single-layer configuration"
    lb_all = jnp.cumsum(jax.nn.softmax(lb_logits.astype(F32), axis=0), axis=0)
    bp, tp, _ = x_prompt.shape
    bs, ts, _ = x_sample.shape
    past = cache_k.shape[2]
    l = 0
    mod = _mod_call(jnp.concatenate([c_prompt, c_sample], axis=0), w_mod[l], b_mod[l])
    w_main = w_in[l][:, :OFF_KI].astype(BF16)
    w_tail = jnp.pad(w_in[l][:, OFF_KI:].astype(BF16), ((0, 0), (0, LANES - (IN_WIDTH - OFF_KI))))
    weights = (norm1[l], w_main, w_tail, lb_all[l], g_norm_a[l], w_out[l].astype(BF16), norm2[l],
               w_ff1[l].astype(BF16), w_ff2[l].astype(BF16), norm_f)
    s0 = jnp.zeros((bp, A_HEADS, A_DK, A_DV), F32)
    yp, kp, vp, kip, sp = _layer(x_prompt, mod[:bp], 0, s0, None, None, None, weights,
                                 **_tiling(bp, tp))
    ys, ks, vs, kis, ss = _layer(x_sample, mod[bp:], past, state_hgrn[l], cache_k[l],
                                 cache_v[l], cache_k_idx[l], weights, **_tiling(bs, ts))
    return (yp, ys, kp[None], vp[None], kip[None], sp[None],
            ks[None], vs[None], kis[None], ss[None])
```

```python
import functools
import math

import numpy as np
import jax
import jax.numpy as jnp
from jax import lax
from jax.experimental import pallas as pl
from jax.experimental.pallas import tpu as pltpu

D_MODEL = 1024
CHUNK = 64
A_HEADS = 4
A_DK = 128
A_DV = 128
A_WIDTH = A_HEADS * A_DV
B_HEADS = 8
B_KV_HEADS = 4
B_HD = 64
B_WIDTH = B_HEADS * B_HD
KV_WIDTH = B_KV_HEADS * B_HD
IDX_HEADS = 8
IDX_DIM = 64
IDX_WIDTH = IDX_HEADS * IDX_DIM
TOPK_MAX = 256
ROT_FRAC = 4
ROPE_THETA = 500000.0
D_FF = 4 * D_MODEL
EPS = 1e-6
IN_WIDTH = 4 * A_WIDTH + B_WIDTH + 2 * KV_WIDTH + IDX_WIDTH + IDX_DIM + IDX_HEADS

LANES = 128
SUBLANES = 8
BF16_ROWS = 16
KEY_TILE = 256
DSA_QUERIES = 256
ROW_TILE = 512
HGRN_CHUNK = 128
HGRN_CHUNKS_PER_STEP = 8
VMEM_LIMIT_BYTES = 56 * 1024 * 1024

F32 = jnp.float32
BF16 = jnp.bfloat16
INT_MIN = np.int32(-2 ** 31)
NEG_INF_PATTERN16 = 0x007F
NEG_BIG = -1e30
LOG2_E = 1.4426950408889634

OFF_HG = 0
OFF_QB = 4 * A_WIDTH
OFF_KB = OFF_QB + B_WIDTH
OFF_VB = OFF_KB + KV_WIDTH
OFF_QI = OFF_VB + KV_WIDTH
OFF_KI = OFF_QI + IDX_WIDTH
OFF_WI = OFF_KI + IDX_DIM


def _dot(a, b):
    return jnp.dot(a, b, preferred_element_type=F32)


def _dot_nt(a, b):
    return lax.dot_general(a, b, (((1,), (1,)), ((), ())), preferred_element_type=F32)


def _silu(x):
    return x * jax.nn.sigmoid(x)


def _rms(x):
    return x * lax.rsqrt(jnp.mean(jnp.square(x), axis=-1, keepdims=True) + EPS)


def _block_diag(x, width):
    zero = jnp.zeros((x.shape[0], width), x.dtype)
    return jnp.concatenate([jnp.concatenate([x[:, :width], zero], axis=1),
                            jnp.concatenate([zero, x[:, width:]], axis=1)], axis=0)


def _mod_kernel(c_ref, w_ref, b_ref, o_ref):
    a = _silu(c_ref[...])
    w = w_ref[...]
    a_hi = a.astype(BF16)
    a_lo = (a - a_hi.astype(F32)).astype(BF16)
    w_hi = w.astype(BF16)
    w_lo = (w - w_hi.astype(F32)).astype(BF16)
    o_ref[...] = _dot(a_hi, w_hi) + _dot(a_lo, w_hi) + _dot(a_hi, w_lo) + b_ref[...]


def _mod_call(c, w_mod, b_mod):
    rows, d = c.shape
    n = w_mod.shape[1]
    tn = 1024
    return pl.pallas_call(
        _mod_kernel,
        grid=(n // tn,),
        in_specs=[pl.BlockSpec((rows, d), lambda j: (0, 0)),
                  pl.BlockSpec((d, tn), lambda j: (0, j)),
                  pl.BlockSpec((1, tn), lambda j: (0, j))],
        out_specs=pl.BlockSpec((rows, tn), lambda j: (0, j)),
        out_shape=jax.ShapeDtypeStruct((rows, n), F32),
        compiler_params=pltpu.CompilerParams(vmem_limit_bytes=VMEM_LIMIT_BYTES),
        name="mod",
    )(c, w_mod, b_mod.reshape(1, n))


def _rope(x, cos, sin_lo, sin_hi):
    half = B_HD // ROT_FRAC // 2
    return (x * cos + pltpu.roll(x, half, 1) * sin_hi
            + pltpu.roll(x, LANES - half, 1) * sin_lo)


def _inproj_kernel(x_ref, mod_ref, n1_ref, w_ref, wt_ref, cos_ref, slo_ref, shi_ref,
                   hg_ref, qb_ref, k_ref, v_ref, qi_ref, ki_ref, wi_ref, *dsa_refs, bb, tt):
    rows = bb * tt
    x = x_ref[...]
    mod = mod_ref[...]
    sh1 = mod[:, :, 0:D_MODEL]
    sc1 = mod[:, :, D_MODEL:2 * D_MODEL]
    h = (_rms(x) * n1_ref[...]) * (1.0 + sc1) + sh1
    h = h.reshape(rows, D_MODEL).astype(BF16)
    z_att = _dot(h, w_ref[:, OFF_QB:])
    z_tail = _dot(h, wt_ref[...])
    cos, slo, shi = cos_ref[...], slo_ref[...], shi_ref[...]

    def cols(off, width):
        return z_att[:, off - OFF_QB:off - OFF_QB + width]

    def rope_cols(off, width):
        return [_rope(cols(off + j, LANES), cos, slo, shi) for j in range(0, width, LANES)]

    scale = B_HD ** -0.5 * LOG2_E
    qb = jnp.concatenate(rope_cols(OFF_QB, B_WIDTH), axis=1) * scale
    qb_ref[...] = qb.astype(BF16).reshape(bb, tt, B_WIDTH)
    kb = jnp.concatenate(rope_cols(OFF_KB, KV_WIDTH), axis=1)
    k_ref[...] = kb.reshape(bb, tt, KV_WIDTH)
    vb = cols(OFF_VB, KV_WIDTH)
    v_ref[...] = vb.reshape(bb, tt, KV_WIDTH)
    qi = jnp.concatenate(rope_cols(OFF_QI, IDX_WIDTH), axis=1)
    qi_ref[...] = qi.astype(BF16).reshape(bb, tt, IDX_WIDTH)
    last = _rope(z_tail, cos, slo, shi)
    ki_ref[...] = last[:, 0:IDX_DIM].reshape(bb, tt, IDX_DIM)
    wi = z_tail[:, OFF_WI - OFF_KI:OFF_WI - OFF_KI + IDX_HEADS] * (IDX_WIDTH ** -0.5)
    wi_ref[...] = wi.reshape(bb, tt, IDX_HEADS)
    if dsa_refs:
        kbf_ref, vt_ref, kibf_ref = dsa_refs
        kbf_ref[0] = kb.astype(BF16)
        vt_ref[0] = vb.T.astype(BF16)
        kibf_ref[0] = last[:, 0:IDX_DIM].astype(BF16)
    hg_ref[...] = _dot(h, w_ref[:, OFF_HG:OFF_QB]).reshape(bb, tt, 4 * A_WIDTH)


def _rope_tables(pos, reps):
    rot = B_HD // ROT_FRAC
    half = rot // 2
    inv = jnp.power(ROPE_THETA, -jnp.arange(half, dtype=F32) * (2.0 / rot))
    ang = pos.astype(F32)[:, None] * inv[None, :]
    cos, sin = jnp.cos(ang), jnp.sin(ang)
    t = pos.shape[0]
    ones = jnp.ones((t, B_HD - rot), F32)
    zeros = jnp.zeros((t, B_HD - rot), F32)
    zh = jnp.zeros((t, half), F32)
    cos_h = jnp.concatenate([cos, cos, ones], axis=1)
    slo_h = jnp.concatenate([-sin, zh, zeros], axis=1)
    shi_h = jnp.concatenate([zh, sin, zeros], axis=1)
    per = LANES // B_HD
    return tuple(jnp.tile(a, (reps, per)) for a in (cos_h, slo_h, shi_h))


def _inproj_call(x, mod, norm1, w_main, w_tail, pos, bb, tt, dsa_layouts):
    b, t, d = x.shape
    cos, slo, shi = _rope_tables(pos, bb)
    rows = bb * tt
    if bb == 1:
        tab_spec = pl.BlockSpec((tt, LANES), lambda i, j: (j, 0))
    else:
        tab_spec = pl.BlockSpec((rows, LANES), lambda i, j: (0, 0))

    def act_spec(w):
        return pl.BlockSpec((bb, tt, w), lambda i, j: (i, j, 0))

    def out(w, dt):
        return jax.ShapeDtypeStruct((b, t, w), dt)

    out_specs = [act_spec(4 * A_WIDTH), act_spec(B_WIDTH), act_spec(KV_WIDTH),
                 act_spec(KV_WIDTH), act_spec(IDX_WIDTH), act_spec(IDX_DIM),
                 act_spec(IDX_HEADS)]
    out_shape = [out(4 * A_WIDTH, F32), out(B_WIDTH, BF16), out(KV_WIDTH, F32),
                 out(KV_WIDTH, F32), out(IDX_WIDTH, BF16), out(IDX_DIM, F32),
                 out(IDX_HEADS, F32)]
    if dsa_layouts:
        assert bb == 1
        out_specs += [act_spec(KV_WIDTH), pl.BlockSpec((1, KV_WIDTH, tt), lambda i, j: (i, 0, j)),
                      act_spec(IDX_DIM)]
        out_shape += [out(KV_WIDTH, BF16), jax.ShapeDtypeStruct((b, KV_WIDTH, t), BF16),
                      out(IDX_DIM, BF16)]

    return pl.pallas_call(
        functools.partial(_inproj_kernel, bb=bb, tt=tt),
        grid=(b // bb, t // tt),
        in_specs=[act_spec(d),
                  pl.BlockSpec((bb, 1, 6 * d), lambda i, j: (i, 0, 0)),
                  pl.BlockSpec((1, 1, d), lambda i, j: (0, 0, 0)),
                  pl.BlockSpec(w_main.shape, lambda i, j: (0, 0)),
                  pl.BlockSpec(w_tail.shape, lambda i, j: (0, 0)),
                  tab_spec, tab_spec, tab_spec],
        out_specs=out_specs,
        out_shape=out_shape,
        compiler_params=pltpu.CompilerParams(
            dimension_semantics=("parallel", "parallel"),
            vmem_limit_bytes=VMEM_LIMIT_BYTES),
        name="inproj",
    )(x, mod.reshape(b, 1, 6 * d), norm1.reshape(1, 1, d), w_main, w_tail, cos, slo, shi)


def _hgrn_tables(cc):
    nlev = int(np.log2(cc))
    t = np.arange(cc)[:, None]
    u = np.arange(cc)[None, :]
    mats = [(u <= t).astype(np.float32)]
    for l in range(nlev):
        m = cc >> (l + 1)
        if m < SUBLANES:
            ref = (t // (2 * m)) * (2 * m) + m - 1
            qside = ((t // m) % 2) == 1
            seg = np.where(qside, (u > ref) & (u <= t), (u > t) & (u <= ref))
            mats.append(-seg.astype(np.float32))
    w = np.concatenate(mats, axis=0)
    w = np.concatenate([w] * 3, axis=1)
    lvl = np.full((cc, cc), -1, np.int32)
    for l in range(nlev):
        m = cc >> (l + 1)
        same_parent = (t // (2 * m)) == (u // (2 * m))
        lvl[same_parent & ((t // m) % 2 == 1) & ((u // m) % 2 == 0)] = l
    lvl[np.arange(cc), np.arange(cc)] = nlev
    return jnp.asarray(w, BF16), jnp.asarray(lvl), nlev


def _level_exponent(cum_ref, cols, cum, m):
    cc, n = cum.shape
    parts = []
    for p in range(0, cc, 2 * m):
        ref = jnp.broadcast_to(cum_ref[p + m - 1:p + m, cols], (m, n))
        parts += [cum[p:p + m] - ref, ref - cum[p + m:p + 2 * m]]
    return jnp.concatenate(parts, axis=0)


def _hgrn_kernel(q_ref, f_ref, i_ref, g_ref, lb_ref, gn_ref, s0_ref, w_ref, lvl_ref,
                 o_ref, s_out_ref, st_scr, cum_scr, *, cc, nc, nlev):
    ci = pl.program_id(1)

    @pl.when(ci == 0)
    def _():
        for h in range(A_HEADS):
            st_scr[h] = s0_ref[0, h].T

    row = lax.broadcasted_iota(jnp.int32, (cc, 2 * A_DK), 0)
    qsides = [((row // (cc >> (l + 1))) % 2) == 1 for l in range(nlev)]
    n_small = sum(1 for l in range(nlev) if (cc >> (l + 1)) < SUBLANES)
    qsides16 = [jnp.where(m, 1.0, 0.0).astype(BF16) > 0 for m in qsides]
    lvl = jnp.concatenate([lvl_ref[...]] * 2, axis=1)
    lvl_masks = [lvl == l for l in range(nlev + 1)]
    w = w_ref[...]
    lb = lb_ref[...]
    f_all = lb + (1.0 - lb) * jax.nn.sigmoid(f_ref[0])
    nl = -jnp.log2(f_all)
    nl_hi = nl.astype(BF16)
    rest = nl - nl_hi.astype(F32)
    nl_mid = rest.astype(BF16)
    nl_lo = (rest - nl_mid.astype(F32)).astype(BF16)
    for c in range(nc):
        rs = slice(c * cc, (c + 1) * cc)
        cum_scr[...] = _dot(w[0:cc], jnp.concatenate([nl_hi[rs], nl_mid[rs], nl_lo[rs]], axis=0))
        sums = _dot(w[cc:, 0:2 * cc], jnp.concatenate([nl_hi[rs], nl_mid[rs]], axis=0))
        for hp in range(A_HEADS // 2):
            h0, h1 = 2 * hp, 2 * hp + 1
            sl = slice(h0 * A_DK, (h1 + 1) * A_DK)
            q = _silu(q_ref[0, rs, sl])
            kk = 1.0 - f_all[rs, sl]
            v = i_ref[0, rs, sl]
            cum = cum_scr[:, sl]
            q16, k16, v16 = q.astype(BF16), kk.astype(BF16), v.astype(BF16)
            attn = jnp.zeros((cc, 2 * cc), F32)
            for l in range(nlev):
                m = cc >> (l + 1)
                if m >= SUBLANES:
                    ex = _level_exponent(cum_scr, sl, cum, m)
                else:
                    small = l - (nlev - n_small)
                    ex = sums[small * cc:(small + 1) * cc, sl]
                e16 = jnp.exp2(ex).astype(BF16)
                if m % BF16_ROWS == 0:
                    qk = jnp.concatenate([(q16 if (r // m) % 2 else k16)[r:r + m]
                                          for r in range(0, cc, m)], axis=0)
                else:
                    qk = jnp.where(qsides16[l], q16, k16)
                xl = qk * e16
                attn = jnp.where(lvl_masks[l], _dot_nt(xl, _block_diag(xl, A_DK)), attn)
            attn = jnp.where(lvl_masks[nlev], _dot_nt(q16, _block_diag(k16, A_DK)), attn)
            last = jnp.broadcast_to(cum_scr[cc - 1:cc, sl], (cc, 2 * A_DK))
            qg = (q * jnp.exp2(-cum)).astype(BF16)
            kg = (kk * jnp.exp2(cum - last)).astype(BF16)
            st = jnp.concatenate([st_scr[h0], st_scr[h1]], axis=1)
            o = (_dot(attn.astype(BF16), _block_diag(v16, A_DV))
                 + _dot_nt(qg, _block_diag(st.astype(BF16), A_DK)))
            dec = jnp.exp2(-cum[cc - 1:cc, :])
            upd = _dot(v.T.astype(BF16), kg)
            st_scr[h0] = st[:, :A_DK] * dec[:, :A_DK] + upd[:A_DV, :A_DK]
            st_scr[h1] = st[:, A_DK:] * dec[:, A_DK:] + upd[A_DV:, A_DK:]
            for i, h in enumerate((h0, h1)):
                hs = slice(h * A_DV, (h + 1) * A_DV)
                y = _rms(o[:, i * A_DV:(i + 1) * A_DV]) * gn_ref[:, hs]
                o_ref[0, rs, hs] = (y * _silu(g_ref[0, rs, hs])).astype(BF16)

    @pl.when(ci == pl.num_programs(1) - 1)
    def _():
        for h in range(A_HEADS):
            s_out_ref[0, h] = st_scr[h].T


def _hgrn_call(hg, lb, g_norm, s0, cc):
    b, t, _ = hg.shape
    w, lvl, nlev = _hgrn_tables(cc)
    nc = HGRN_CHUNKS_PER_STEP if t % (HGRN_CHUNKS_PER_STEP * cc) == 0 else 1
    rows = nc * cc

    def part(p):
        return pl.BlockSpec((1, rows, A_WIDTH), lambda i, j, p=p: (i, j, p))

    return pl.pallas_call(
        functools.partial(_hgrn_kernel, cc=cc, nc=nc, nlev=nlev),
        grid=(b, t // rows),
        in_specs=[part(0), part(1), part(2), part(3),
                  pl.BlockSpec((1, A_WIDTH), lambda i, j: (0, 0)),
                  pl.BlockSpec((1, A_WIDTH), lambda i, j: (0, 0)),
                  pl.BlockSpec((1, A_HEADS, A_DK, A_DV), lambda i, j: (i, 0, 0, 0)),
                  pl.BlockSpec(w.shape, lambda i, j: (0, 0)),
                  pl.BlockSpec(lvl.shape, lambda i, j: (0, 0))],
        out_specs=[pl.BlockSpec((1, rows, A_WIDTH), lambda i, j: (i, j, 0)),
                   pl.BlockSpec((1, A_HEADS, A_DK, A_DV), lambda i, j: (i, 0, 0, 0))],
        out_shape=[jax.ShapeDtypeStruct((b, t, A_WIDTH), BF16),
                   jax.ShapeDtypeStruct((b, A_HEADS, A_DK, A_DV), F32)],
        scratch_shapes=[pltpu.VMEM((A_HEADS, A_DV, A_DK), F32),
                        pltpu.VMEM((cc, A_WIDTH), F32)],
        compiler_params=pltpu.CompilerParams(
            dimension_semantics=("parallel", "arbitrary"),
            vmem_limit_bytes=VMEM_LIMIT_BYTES),
        name="hgrn",
    )(hg, hg, hg, hg, lb.reshape(1, A_WIDTH), g_norm.reshape(1, A_WIDTH), s0, w, lvl)


def _fold8(x, op):
    parts = [x[r:r + SUBLANES] for r in range(0, x.shape[0], SUBLANES)]
    while len(parts) > 1:
        parts = [op(parts[i], parts[i + 1]) for i in range(0, len(parts) - 1, 2)] + (
            [parts[-1]] if len(parts) % 2 else [])
    return parts[0]


def _grouped_rhs(xs, heads, width, slot, slots):
    groups = len(xs)
    zero = jnp.zeros((xs[0].shape[0], width), xs[0].dtype)
    rows = []
    for h in heads:
        for g, x in enumerate(xs):
            parts = [zero] * (groups * slots)
            parts[g * slots + slot] = x[:, h * width:(h + 1) * width]
            rows.append(jnp.concatenate(parts, axis=1) if len(parts) > 1 else parts[0])
    return jnp.concatenate(rows, axis=0)


def _dsa_kernel(qi_ref, wit_ref, qb_ref, ki_ref, k_ref, vt_ref, tri_ref, o_ref,
                score_scr, score16_scr, logit_scr, acc_scr, *, groups, gq, qreal, tk, topk, pos0):
    qb = groups * gq
    blk = pl.program_id(1)
    last_pos = pos0 + (blk + 1) * qreal - 1
    extent = jnp.minimum((last_pos // CHUNK + 1) * CHUNK, tk)
    ntile = (extent + KEY_TILE - 1) // KEY_TILE
    lane = lax.broadcasted_iota(jnp.int32, (1, qb), 1)
    qpos = pos0 + blk * qreal + lane % gq
    key_end = jnp.minimum((qpos // CHUNK + 1) * CHUNK, tk)
    neg_inf = jnp.float32(-jnp.inf)

    def lane_cat(parts):
        return jnp.concatenate(parts, axis=1) if len(parts) > 1 else parts[0]

    qis = [qi_ref[g] for g in range(groups)]
    wit = lane_cat([wit_ref[g] for g in range(groups)])
    qi_pairs = [_grouped_rhs(qis, (2 * p, 2 * p + 1), IDX_DIM, 0, 1)
                for p in range(IDX_HEADS // 2)]

    def tile_start(j):
        return pl.multiple_of(j * KEY_TILE, KEY_TILE)

    def tile_loop(body, init):
        def run(first, count, c):
            for u in range(count):
                c = body(first + u, c)
            return c
        carry = lax.fori_loop(0, ntile // 4, lambda i, c: run(4 * i, 4, c), init)
        done = (ntile // 4) * 4
        carry = lax.cond((ntile & 2) != 0, lambda c: run(done, 2, c), lambda c: c, carry)
        return lax.cond((ntile & 1) != 0, lambda c: body(ntile - 1, c), lambda c: c, carry)

    def score_tile(j, carry):
        k0 = tile_start(j)
        ki_t = lane_cat([ki_ref[g, pl.ds(k0, KEY_TILE), :] for g in range(groups)])
        acc = jnp.zeros((KEY_TILE, qb), F32)
        for p in range(IDX_HEADS // 2):
            s2 = jnp.maximum(_dot_nt(ki_t, qi_pairs[p]), 0.0)
            acc = acc + wit[2 * p:2 * p + 1, :] * s2[:, :qb]
            acc = acc + wit[2 * p + 1:2 * p + 2, :] * s2[:, qb:]
        kidx = k0 + lax.broadcasted_iota(jnp.int32, (KEY_TILE, qb), 0)
        masked = jnp.where(kidx < key_end, acc, neg_inf)
        score_scr[pl.ds(k0, KEY_TILE), :] = masked
        score16_scr[pl.ds(k0, KEY_TILE), :] = masked.astype(BF16)
        return carry

    tile_loop(score_tile, 0)

    def count(pred_fn):
        def body(j, acc):
            s = score_scr[pl.ds(tile_start(j), KEY_TILE), :]
            return acc + _fold8(jnp.where(pred_fn(s), 1.0, 0.0), jnp.add)
        acc = tile_loop(body, jnp.zeros((SUBLANES, qb), F32))
        return jnp.sum(acc, axis=0, keepdims=True)

    def count16(cand):
        one, zero = jnp.ones((), BF16), jnp.zeros((), BF16)

        def body(j, acc):
            r = score16_scr[pl.ds(tile_start(j), KEY_TILE), :]
            hit = jnp.where(r >= cand, one, zero)
            parts = [hit[i:i + BF16_ROWS] for i in range(0, KEY_TILE, BF16_ROWS)]
            while len(parts) > 1:
                parts = [parts[i] + parts[i + 1] for i in range(0, len(parts), 2)]
            return acc + parts[0]
        acc = tile_loop(body, jnp.zeros((BF16_ROWS, qb), BF16))
        return jnp.sum(acc.astype(F32), axis=0, keepdims=True)

    def order_key(x):
        return x ^ ((x >> 31) & np.int32(0x7FFFFFFF))

    def decode16(t16):
        b16 = jnp.where(t16 >= 0x8000, t16 & 0x7FFF, ~t16 & 0xFFFF)
        return pltpu.bitcast(b16 << 16, F32)

    def bisect16(i, t16):
        cand16 = t16 | (jnp.int32(1) << (15 - i))
        cnt = count16(decode16(cand16).astype(BF16))
        return jnp.where(cnt >= topk, cand16, t16)

    search = extent > topk
    t16 = lax.fori_loop(0, jnp.where(search, 16, 0), bisect16, jnp.zeros((1, qb), jnp.int32))
    few = t16 <= NEG_INF_PATTERN16
    head = decode16(jnp.maximum(t16, NEG_INF_PATTERN16))
    base = order_key(pltpu.bitcast(head, jnp.int32)) - (1 << 15) - 1

    def bisect17(i, d):
        cand_d = d | (jnp.int32(1) << (16 - i))
        cand = pltpu.bitcast(order_key(base + cand_d), F32)
        cnt = count(lambda s: s >= cand)
        return jnp.where(cnt >= topk, cand_d, d)

    d = lax.fori_loop(0, jnp.where(search, 17, 0), bisect17, jnp.zeros((1, qb), jnp.int32))
    thr = jnp.where(few, neg_inf, pltpu.bitcast(order_key(base + d), F32))
    need = topk - count(lambda s: s > thr)
    need = jnp.where(few, 0.0, need)

    qqs = [qb_ref[g] for g in range(groups)]
    per_q = B_HEADS // B_KV_HEADS
    per_block = LANES // B_HD
    q_pairs = [_grouped_rhs(qqs, (n * per_q, n * per_q + 1), B_HD, n % per_block, per_block)
               for n in range(B_KV_HEADS)]
    tri = tri_ref[...]

    def logit_tile(j, carry):
        offs, mx = carry
        k0 = tile_start(j)
        s = score_scr[pl.ds(k0, KEY_TILE), :]
        tie = jnp.where(s == thr, 1.0, 0.0)
        rank = _dot(tri, tie.astype(BF16)) + offs
        picked = jnp.where(s > thr, 1.0, jnp.where(rank < need, tie, 0.0))
        bias = jnp.where(picked > 0.0, 0.0, NEG_BIG)
        bias2 = jnp.concatenate([bias, bias], axis=1)
        new_mx = []
        for n in range(B_KV_HEADS):
            kblk = (n // per_block) * LANES
            k_t = lane_cat([k_ref[g, pl.ds(k0, KEY_TILE), kblk:kblk + LANES]
                            for g in range(groups)])
            lg = _dot_nt(k_t, q_pairs[n]) + bias2
            logit_scr[n, pl.ds(k0, KEY_TILE), :] = lg
            new_mx.append(jnp.maximum(mx[n], _fold8(lg, jnp.maximum)))
        offs = offs + jnp.sum(_fold8(tie, jnp.add), axis=0, keepdims=True)
        return offs, tuple(new_mx)

    mx0 = tuple(jnp.full((SUBLANES, 2 * qb), NEG_BIG, F32) for _ in range(B_KV_HEADS))
    _, mx = tile_loop(logit_tile, (jnp.zeros((1, qb), F32), mx0))
    mx = [jnp.max(m, axis=0, keepdims=True) for m in mx]

    acc_scr[...] = jnp.zeros_like(acc_scr)

    def pv_tile(j, den):
        k0 = tile_start(j)
        new_den = []
        for n in range(B_KV_HEADS):
            p = jnp.exp2(logit_scr[n, pl.ds(k0, KEY_TILE), :] - mx[n])
            new_den.append(den[n] + _fold8(p, jnp.add))
            vts = [vt_ref[g, n * B_HD:(n + 1) * B_HD, pl.ds(k0, KEY_TILE)] for g in range(groups)]
            vt = jnp.concatenate(vts, axis=0) if groups > 1 else vts[0]
            acc_scr[n] += _dot(vt, p.astype(BF16))
        return tuple(new_den)

    den0 = tuple(jnp.zeros((SUBLANES, 2 * qb), F32) for _ in range(B_KV_HEADS))
    den = tile_loop(pv_tile, den0)
    lane_group = (lax.broadcasted_iota(jnp.int32, (1, 2 * qb), 1) % qb) // gq
    rows = []
    for n in range(B_KV_HEADS):
        acc = acc_scr[n]
        o2 = acc[0:B_HD]
        for g in range(1, groups):
            o2 = jnp.where(lane_group == g, acc[g * B_HD:(g + 1) * B_HD], o2)
        o2 = o2 / jnp.sum(den[n], axis=0, keepdims=True)
        rows += [o2[:, :qb], o2[:, qb:]]
    o_ref[...] = jnp.concatenate(rows, axis=0).T.astype(BF16).reshape(groups, gq, B_WIDTH)


def _dsa_call(qi, wi, qbs, k_bf, vt, ki_bf, tk, pos0):
    b, t, _ = qi.shape
    if t % DSA_QUERIES == 0:
        groups, gq = 1, DSA_QUERIES
    else:
        assert DSA_QUERIES % t == 0
        groups, gq = math.gcd(DSA_QUERIES // t, b), t
        assert (groups * gq) % LANES == 0
    qb = groups * gq
    tkp = ki_bf.shape[1]
    topk = min(TOPK_MAX, tk // 4)
    assert topk <= KEY_TILE
    wit = jnp.swapaxes(wi, 1, 2)
    tri = jnp.asarray(np.tril(np.ones((KEY_TILE, KEY_TILE), np.float32), -1), BF16)

    def q_spec(w):
        return pl.BlockSpec((groups, gq, w), lambda i, j: (i, j, 0))

    def kv_spec(rows, cols):
        return pl.BlockSpec((groups, rows, cols), lambda i, j: (i, 0, 0))

    return pl.pallas_call(
        functools.partial(_dsa_kernel, groups=groups, gq=gq, qreal=gq, tk=tk, topk=topk,
                          pos0=pos0),
        grid=(b // groups, t // gq),
        in_specs=[q_spec(IDX_WIDTH),
                  pl.BlockSpec((groups, IDX_HEADS, gq), lambda i, j: (i, 0, j)),
                  q_spec(B_WIDTH),
                  kv_spec(tkp, IDX_DIM), kv_spec(tkp, KV_WIDTH), kv_spec(KV_WIDTH, tkp),
                  pl.BlockSpec((KEY_TILE, KEY_TILE), lambda i, j: (0, 0))],
        out_specs=q_spec(B_WIDTH),
        out_shape=jax.ShapeDtypeStruct((b, t, B_WIDTH), BF16),
        scratch_shapes=[pltpu.VMEM((tkp, qb), F32),
                        pltpu.VMEM((tkp, qb), BF16),
                        pltpu.VMEM((B_KV_HEADS, tkp, 2 * qb), F32),
                        pltpu.VMEM((B_KV_HEADS, groups * B_HD, 2 * qb), F32)],
        compiler_params=pltpu.CompilerParams(
            dimension_semantics=("parallel", "parallel"),
            vmem_limit_bytes=VMEM_LIMIT_BYTES),
        name="dsa",
    )(qi, wit, qbs, ki_bf, k_bf, vt, tri)


def _out_kernel(x_ref, oa_ref, ob_ref, mod_ref, n2_ref, nf_ref, wo_ref, w1_ref, w2_ref,
                y_ref, *, bb, tt):
    rows = bb * tt
    x = x_ref[...]
    mod = mod_ref[...]
    g1 = mod[:, :, 2 * D_MODEL:3 * D_MODEL]
    sh2 = mod[:, :, 3 * D_MODEL:4 * D_MODEL]
    sc2 = mod[:, :, 4 * D_MODEL:5 * D_MODEL]
    g2 = mod[:, :, 5 * D_MODEL:6 * D_MODEL]
    oa = oa_ref[...].reshape(rows, A_WIDTH)
    ob = ob_ref[...].reshape(rows, B_WIDTH)
    mix = _dot(oa, wo_ref[0:A_WIDTH, :]) + _dot(ob, wo_ref[A_WIDTH:A_WIDTH + B_WIDTH, :])
    x = x + g1 * mix.reshape(bb, tt, D_MODEL)
    h2 = (_rms(x) * n2_ref[...]) * (1.0 + sc2) + sh2
    u = _dot(h2.reshape(rows, D_MODEL).astype(BF16), w1_ref[...])
    r = jnp.square(jnp.maximum(u, 0.0)).astype(BF16)
    x = x + g2 * _dot(r, w2_ref[...]).reshape(bb, tt, D_MODEL)
    y_ref[...] = _rms(x) * nf_ref[...]


def _out_call(x, oa, ob, mod, norm2, norm_f, wo_bf, w1_bf, w2_bf, bb, tt):
    b, t, d = x.shape

    def act_spec(w):
        return pl.BlockSpec((bb, tt, w), lambda i, j: (i, j, 0))

    def const_spec(shape):
        zeros = (0,) * len(shape)
        return pl.BlockSpec(shape, lambda i, j: zeros, pipeline_mode=pl.Buffered(1))

    return pl.pallas_call(
        functools.partial(_out_kernel, bb=bb, tt=tt),
        grid=(b // bb, t // tt),
        in_specs=[act_spec(d), act_spec(A_WIDTH), act_spec(B_WIDTH),
                  pl.BlockSpec((bb, 1, 6 * d), lambda i, j: (i, 0, 0)),
                  const_spec((1, 1, d)), const_spec((1, 1, d)),
                  const_spec(wo_bf.shape), const_spec(w1_bf.shape), const_spec(w2_bf.shape)],
        out_specs=act_spec(d),
        out_shape=jax.ShapeDtypeStruct((b, t, d), F32),
        compiler_params=pltpu.CompilerParams(
            dimension_semantics=("parallel", "parallel"),
            vmem_limit_bytes=VMEM_LIMIT_BYTES),
        name="out",
    )(x, oa, ob, mod.reshape(b, 1, 6 * d), norm2.reshape(1, 1, d), norm_f.reshape(1, 1, d),
      wo_bf, w1_bf, w2_bf)


def _layer(x, mod, pos0, s0, k_past, v_past, ki_past, weights, bb, tt, cc):
    norm1, w_main, w_tail, lb, g_norm, wo_bf, norm2, w1_bf, w2_bf, norm_f = weights
    b, t, _ = x.shape
    pos = pos0 + jnp.arange(t)
    outs = _inproj_call(x, mod, norm1, w_main, w_tail, pos, bb, tt, dsa_layouts=k_past is None)
    hg, qbs, k_new, v_new, qi, ki_new, wi = outs[:7]
    oa, s_new = _hgrn_call(hg, lb, g_norm, s0, cc)
    if k_past is None:
        assert t % KEY_TILE == 0
        tk = t
        k_bf, vt, ki_bf = outs[7:]
    else:
        n_past = k_past.shape[1]
        tk = n_past + t
        tkp = -(-tk // KEY_TILE) * KEY_TILE

        def cat(past, new, axis):
            pads = [(0, 0)] * 3
            pads[axis] = (0, tkp - tk)
            return jnp.pad(jnp.concatenate([past.astype(BF16), new.astype(BF16)], axis=axis), pads)

        k_bf = cat(k_past.reshape(b, n_past, KV_WIDTH), k_new, 1)
        vt = cat(jnp.transpose(v_past, (0, 2, 3, 1)).reshape(b, KV_WIDTH, n_past),
                 jnp.swapaxes(v_new, 1, 2), 2)
        ki_bf = cat(ki_past, ki_new, 1)
    ob = _dsa_call(qi, wi, qbs, k_bf, vt, ki_bf, tk, pos0)
    y = _out_call(x, oa, ob, mod, norm2, norm_f, wo_bf, w1_bf, w2_bf, bb, tt)
    return (y, k_new.reshape(b, t, B_KV_HEADS, B_HD), v_new.reshape(b, t, B_KV_HEADS, B_HD),
            ki_new, s_new)


def _tiling(b, t):
    if t >= ROW_TILE:
        assert t % ROW_TILE == 0
        bb, tt = 1, ROW_TILE
    else:
        bb, tt = min(b, ROW_TILE // t), t
        assert b % bb == 0
    return dict(bb=bb, tt=tt, cc=min(HGRN_CHUNK, t))


def kernel(x_prompt, x_sample, cache_k, cache_v, cache_k_idx, state_hgrn, c_prompt, c_sample,
           w_mod, b_mod, norm1, w_in, lb_logits, g_norm_a, w_out, norm2, w_ff1, w_ff2, norm_f):
    depth = w_in.shape[0]
    assert depth == 1, "kernel is written for the single-layer configuration"
    lb_all = jnp.cumsum(jax.nn.softmax(lb_logits.astype(F32), axis=0), axis=0)
    bp, tp, _ = x_prompt.shape
    bs, ts, _ = x_sample.shape
    past = cache_k.shape[2]
    l = 0
    mod = _mod_call(jnp.concatenate([c_prompt, c_sample], axis=0), w_mod[l], b_mod[l])
    w_main = w_in[l][:, :OFF_KI].astype(BF16)
    w_tail = jnp.pad(w_in[l][:, OFF_KI:].astype(BF16), ((0, 0), (0, LANES - (IN_WIDTH - OFF_KI))))
    weights = (norm1[l], w_main, w_tail, lb_all[l], g_norm_a[l], w_out[l].astype(BF16), norm2[l],
               w_ff1[l].astype(BF16), w_ff2[l].astype(BF16), norm_f)
    s0 = jnp.zeros((bp, A_HEADS, A_DK, A_DV), F32)
    yp, kp, vp, kip, sp = _layer(x_prompt, mod[:bp], 0, s0, None, None, None, weights,
                                 **_tiling(bp, tp))
    ys, ks, vs, kis, ss = _layer(x_sample, mod[bp:], past, state_hgrn[l], cache_k[l],
                                 cache_v[l], cache_k_idx[l], weights, **_tiling(bs, ts))
    return (yp, ys, kp[None], vp[None], kip[None], sp[None],
            ks[None], vs[None], kis[None], ss[None])
```

```python
import functools

import numpy as np
import jax
import jax.numpy as jnp
from jax import lax
from jax.experimental import pallas as pl
from jax.experimental.pallas import tpu as pltpu

D_MODEL = 1024
CHUNK = 64
A_HEADS = 4
A_DK = 128
A_DV = 128
A_WIDTH = A_HEADS * A_DV
B_HEADS = 8
B_KV_HEADS = 4
B_HD = 64
B_WIDTH = B_HEADS * B_HD
KV_WIDTH = B_KV_HEADS * B_HD
IDX_HEADS = 8
IDX_DIM = 64
IDX_WIDTH = IDX_HEADS * IDX_DIM
TOPK_MAX = 256
ROT_FRAC = 4
ROPE_THETA = 500000.0
D_FF = 4 * D_MODEL
EPS = 1e-6
IN_WIDTH = 4 * A_WIDTH + B_WIDTH + 2 * KV_WIDTH + IDX_WIDTH + IDX_DIM + IDX_HEADS

LANES = 128
SUBLANES = 8
BF16_ROWS = 16
KEY_TILE = 256
DSA_QUERIES = 256
ROW_TILE = 512
HGRN_CHUNK = 128
HGRN_CHUNKS_PER_STEP = 8
VMEM_LIMIT_BYTES = 56 * 1024 * 1024

F32 = jnp.float32
BF16 = jnp.bfloat16
INT_MIN = np.int32(-2 ** 31)
NEG_INF_PATTERN16 = 0x007F
NEG_BIG = -1e30
LOG2_E = 1.4426950408889634

OFF_HG = 0
OFF_QB = 4 * A_WIDTH
OFF_KB = OFF_QB + B_WIDTH
OFF_VB = OFF_KB + KV_WIDTH
OFF_QI = OFF_VB + KV_WIDTH
OFF_KI = OFF_QI + IDX_WIDTH
OFF_WI = OFF_KI + IDX_DIM


def _dot(a, b):
    return jnp.dot(a, b, preferred_element_type=F32)


def _dot_nt(a, b):
    return lax.dot_general(a, b, (((1,), (1,)), ((), ())), preferred_element_type=F32)


def _silu(x):
    return x * jax.nn.sigmoid(x)


def _rms(x):
    return x * lax.rsqrt(jnp.mean(jnp.square(x), axis=-1, keepdims=True) + EPS)


def _block_diag(x, width):
    zero = jnp.zeros((x.shape[0], width), x.dtype)
    return jnp.concatenate([jnp.concatenate([x[:, :width], zero], axis=1),
                            jnp.concatenate([zero, x[:, width:]], axis=1)], axis=0)


def _mod_kernel(c_ref, w_ref, b_ref, o_ref):
    a = _silu(c_ref[...])
    w = w_ref[...]
    a_hi = a.astype(BF16)
    a_lo = (a - a_hi.astype(F32)).astype(BF16)
    w_hi = w.astype(BF16)
    w_lo = (w - w_hi.astype(F32)).astype(BF16)
    o_ref[...] = _dot(a_hi, w_hi) + _dot(a_lo, w_hi) + _dot(a_hi, w_lo) + b_ref[...]


def _mod_call(c, w_mod, b_mod):
    rows, d = c.shape
    n = w_mod.shape[1]
    tn = 1024
    return pl.pallas_call(
        _mod_kernel,
        grid=(n // tn,),
        in_specs=[pl.BlockSpec((rows, d), lambda j: (0, 0)),
                  pl.BlockSpec((d, tn), lambda j: (0, j)),
                  pl.BlockSpec((1, tn), lambda j: (0, j))],
        out_specs=pl.BlockSpec((rows, tn), lambda j: (0, j)),
        out_shape=jax.ShapeDtypeStruct((rows, n), F32),
        compiler_params=pltpu.CompilerParams(vmem_limit_bytes=VMEM_LIMIT_BYTES),
        name="mod",
    )(c, w_mod, b_mod.reshape(1, n))


def _rope(x, cos, sin_lo, sin_hi):
    half = B_HD // ROT_FRAC // 2
    return (x * cos + pltpu.roll(x, half, 1) * sin_hi
            + pltpu.roll(x, LANES - half, 1) * sin_lo)


def _inproj_kernel(x_ref, mod_ref, n1_ref, w_ref, wt_ref, cos_ref, slo_ref, shi_ref,
                   hg_ref, qb_ref, k_ref, v_ref, qi_ref, ki_ref, wi_ref, *dsa_refs, bb, tt):
    rows = bb * tt
    x = x_ref[...]
    mod = mod_ref[...]
    sh1 = mod[:, :, 0:D_MODEL]
    sc1 = mod[:, :, D_MODEL:2 * D_MODEL]
    h = (_rms(x) * n1_ref[...]) * (1.0 + sc1) + sh1
    h = h.reshape(rows, D_MODEL).astype(BF16)
    z_att = _dot(h, w_ref[:, OFF_QB:])
    z_tail = _dot(h, wt_ref[...])
    cos, slo, shi = cos_ref[...], slo_ref[...], shi_ref[...]

    def cols(off, width):
        return z_att[:, off - OFF_QB:off - OFF_QB + width]

    def rope_cols(off, width):
        return [_rope(cols(off + j, LANES), cos, slo, shi) for j in range(0, width, LANES)]

    scale = B_HD ** -0.5 * LOG2_E
    qb = jnp.concatenate(rope_cols(OFF_QB, B_WIDTH), axis=1) * scale
    qb_ref[...] = qb.astype(BF16).reshape(bb, tt, B_WIDTH)
    kb = jnp.concatenate(rope_cols(OFF_KB, KV_WIDTH), axis=1)
    k_ref[...] = kb.reshape(bb, tt, KV_WIDTH)
    vb = cols(OFF_VB, KV_WIDTH)
    v_ref[...] = vb.reshape(bb, tt, KV_WIDTH)
    qi = jnp.concatenate(rope_cols(OFF_QI, IDX_WIDTH), axis=1)
    qi_ref[...] = qi.astype(BF16).reshape(bb, tt, IDX_WIDTH)
    last = _rope(z_tail, cos, slo, shi)
    ki_ref[...] = last[:, 0:IDX_DIM].reshape(bb, tt, IDX_DIM)
    wi = z_tail[:, OFF_WI - OFF_KI:OFF_WI - OFF_KI + IDX_HEADS] * (IDX_WIDTH ** -0.5)
    wi_ref[...] = wi.reshape(bb, tt, IDX_HEADS)
    if dsa_refs:
        kbf_ref, vt_ref, kibf_ref = dsa_refs
        kbf_ref[0] = kb.astype(BF16)
        vt_ref[0] = vb.T.astype(BF16)
        kibf_ref[0] = last[:, 0:IDX_DIM].astype(BF16)
    hg_ref[...] = _dot(h, w_ref[:, OFF_HG:OFF_QB]).reshape(bb, tt, 4 * A_WIDTH)


def _rope_tables(pos, reps):
    rot = B_HD // ROT_FRAC
    half = rot // 2
    inv = jnp.power(ROPE_THETA, -jnp.arange(half, dtype=F32) * (2.0 / rot))
    ang = pos.astype(F32)[:, None] * inv[None, :]
    cos, sin = jnp.cos(ang), jnp.sin(ang)
    t = pos.shape[0]
    ones = jnp.ones((t, B_HD - rot), F32)
    zeros = jnp.zeros((t, B_HD - rot), F32)
    zh = jnp.zeros((t, half), F32)
    cos_h = jnp.concatenate([cos, cos, ones], axis=1)
    slo_h = jnp.concatenate([-sin, zh, zeros], axis=1)
    shi_h = jnp.concatenate([zh, sin, zeros], axis=1)
    per = LANES // B_HD
    return tuple(jnp.tile(a, (reps, per)) for a in (cos_h, slo_h, shi_h))


def _inproj_call(x, mod, norm1, w_main, w_tail, pos, bb, tt, dsa_layouts):
    b, t, d = x.shape
    cos, slo, shi = _rope_tables(pos, bb)
    rows = bb * tt
    if bb == 1:
        tab_spec = pl.BlockSpec((tt, LANES), lambda i, j: (j, 0))
    else:
        tab_spec = pl.BlockSpec((rows, LANES), lambda i, j: (0, 0))

    def act_spec(w):
        return pl.BlockSpec((bb, tt, w), lambda i, j: (i, j, 0))

    def out(w, dt):
        return jax.ShapeDtypeStruct((b, t, w), dt)

    out_specs = [act_spec(4 * A_WIDTH), act_spec(B_WIDTH), act_spec(KV_WIDTH),
                 act_spec(KV_WIDTH), act_spec(IDX_WIDTH), act_spec(IDX_DIM),
                 act_spec(IDX_HEADS)]
    out_shape = [out(4 * A_WIDTH, F32), out(B_WIDTH, BF16), out(KV_WIDTH, F32),
                 out(KV_WIDTH, F32), out(IDX_WIDTH, BF16), out(IDX_DIM, F32),
                 out(IDX_HEADS, F32)]
    if dsa_layouts:
        assert bb == 1
        out_specs += [act_spec(KV_WIDTH), pl.BlockSpec((1, KV_WIDTH, tt), lambda i, j: (i, 0, j)),
                      act_spec(IDX_DIM)]
        out_shape += [out(KV_WIDTH, BF16), jax.ShapeDtypeStruct((b, KV_WIDTH, t), BF16),
                      out(IDX_DIM, BF16)]

    return pl.pallas_call(
        functools.partial(_inproj_kernel, bb=bb, tt=tt),
        grid=(b // bb, t // tt),
        in_specs=[act_spec(d),
                  pl.BlockSpec((bb, 1, 6 * d), lambda i, j: (i, 0, 0)),
                  pl.BlockSpec((1, 1, d), lambda i, j: (0, 0, 0)),
                  pl.BlockSpec(w_main.shape, lambda i, j: (0, 0)),
                  pl.BlockSpec(w_tail.shape, lambda i, j: (0, 0)),
                  tab_spec, tab_spec, tab_spec],
        out_specs=out_specs,
        out_shape=out_shape,
        compiler_params=pltpu.CompilerParams(
            dimension_semantics=("parallel", "parallel"),
            vmem_limit_bytes=VMEM_LIMIT_BYTES),
        name="inproj",
    )(x, mod.reshape(b, 1, 6 * d), norm1.reshape(1, 1, d), w_main, w_tail, cos, slo, shi)


def _hgrn_tables(cc):
    nlev = int(np.log2(cc))
    t = np.arange(cc)[:, None]
    u = np.arange(cc)[None, :]
    mats = [(u <= t).astype(np.float32)]
    for l in range(nlev):
        m = cc >> (l + 1)
        if m < SUBLANES:
            ref = (t // (2 * m)) * (2 * m) + m - 1
            qside = ((t // m) % 2) == 1
            seg = np.where(qside, (u > ref) & (u <= t), (u > t) & (u <= ref))
            mats.append(-seg.astype(np.float32))
    w = np.concatenate(mats, axis=0)
    w = np.concatenate([w] * 3, axis=1)
    lvl = np.full((cc, cc), -1, np.int32)
    for l in range(nlev):
        m = cc >> (l + 1)
        same_parent = (t // (2 * m)) == (u // (2 * m))
        lvl[same_parent & ((t // m) % 2 == 1) & ((u // m) % 2 == 0)] = l
    lvl[np.arange(cc), np.arange(cc)] = nlev
    return jnp.asarray(w, BF16), jnp.asarray(lvl), nlev


def _level_exponent(cum_ref, cols, cum, m):
    cc, n = cum.shape
    parts = []
    for p in range(0, cc, 2 * m):
        ref = jnp.broadcast_to(cum_ref[p + m - 1:p + m, cols], (m, n))
        parts += [cum[p:p + m] - ref, ref - cum[p + m:p + 2 * m]]
    return jnp.concatenate(parts, axis=0)


def _hgrn_kernel(q_ref, f_ref, i_ref, g_ref, lb_ref, gn_ref, s0_ref, w_ref, lvl_ref,
                 o_ref, s_out_ref, st_scr, cum_scr, *, cc, nc, nlev):
    ci = pl.program_id(1)

    @pl.when(ci == 0)
    def _():
        for h in range(A_HEADS):
            st_scr[h] = s0_ref[0, h].T

    row = lax.broadcasted_iota(jnp.int32, (cc, 2 * A_DK), 0)
    qsides = [((row // (cc >> (l + 1))) % 2) == 1 for l in range(nlev)]
    n_small = sum(1 for l in range(nlev) if (cc >> (l + 1)) < SUBLANES)
    qsides16 = [jnp.where(m, 1.0, 0.0).astype(BF16) > 0 for m in qsides]
    lvl = jnp.concatenate([lvl_ref[...]] * 2, axis=1)
    lvl_masks = [lvl == l for l in range(nlev + 1)]
    w = w_ref[...]
    lb = lb_ref[...]
    f_all = lb + (1.0 - lb) * jax.nn.sigmoid(f_ref[0])
    nl = -jnp.log2(f_all)
    nl_hi = nl.astype(BF16)
    rest = nl - nl_hi.astype(F32)
    nl_mid = rest.astype(BF16)
    nl_lo = (rest - nl_mid.astype(F32)).astype(BF16)
    for c in range(nc):
        rs = slice(c * cc, (c + 1) * cc)
        cum_scr[...] = _dot(w[0:cc], jnp.concatenate([nl_hi[rs], nl_mid[rs], nl_lo[rs]], axis=0))
        sums = _dot(w[cc:, 0:2 * cc], jnp.concatenate([nl_hi[rs], nl_mid[rs]], axis=0))
        for hp in range(A_HEADS // 2):
            h0, h1 = 2 * hp, 2 * hp + 1
            sl = slice(h0 * A_DK, (h1 + 1) * A_DK)
            q = _silu(q_ref[0, rs, sl])
            kk = 1.0 - f_all[rs, sl]
            v = i_ref[0, rs, sl]
            cum = cum_scr[:, sl]
            q16, k16, v16 = q.astype(BF16), kk.astype(BF16), v.astype(BF16)
            attn = jnp.zeros((cc, 2 * cc), F32)
            for l in range(nlev):
                m = cc >> (l + 1)
                if m >= SUBLANES:
                    ex = _level_exponent(cum_scr, sl, cum, m)
                else:
                    small = l - (nlev - n_small)
                    ex = sums[small * cc:(small + 1) * cc, sl]
                e16 = jnp.exp2(ex).astype(BF16)
                if m % BF16_ROWS == 0:
                    qk = jnp.concatenate([(q16 if (r // m) % 2 else k16)[r:r + m]
                                          for r in range(0, cc, m)], axis=0)
                else:
                    qk = jnp.where(qsides16[l], q16, k16)
                xl = qk * e16
                attn = jnp.where(lvl_masks[l], _dot_nt(xl, _block_diag(xl, A_DK)), attn)
            attn = jnp.where(lvl_masks[nlev], _dot_nt(q16, _block_diag(k16, A_DK)), attn)
            last = jnp.broadcast_to(cum_scr[cc - 1:cc, sl], (cc, 2 * A_DK))
            qg = (q * jnp.exp2(-cum)).astype(BF16)
            kg = (kk * jnp.exp2(cum - last)).astype(BF16)
            st = jnp.concatenate([st_scr[h0], st_scr[h1]], axis=1)
            o = (_dot(attn.astype(BF16), _block_diag(v16, A_DV))
                 + _dot_nt(qg, _block_diag(st.astype(BF16), A_DK)))
            dec = jnp.exp2(-cum[cc - 1:cc, :])
            upd = _dot(v.T.astype(BF16), kg)
            st_scr[h0] = st[:, :A_DK] * dec[:, :A_DK] + upd[:A_DV, :A_DK]
            st_scr[h1] = st[:, A_DK:] * dec[:, A_DK:] + upd[A_DV:, A_DK:]
            for i, h in enumerate((h0, h1)):
                hs = slice(h * A_DV, (h + 1) * A_DV)
                y = _rms(o[:, i * A_DV:(i + 1) * A_DV]) * gn_ref[:, hs]
                o_ref[0, rs, hs] = (y * _silu(g_ref[0, rs, hs])).astype(BF16)

    @pl.when(ci == pl.num_programs(1) - 1)
    def _():
        for h in range(A_HEADS):
            s_out_ref[0, h] = st_scr[h].T


def _hgrn_call(hg, lb, g_norm, s0, cc):
    b, t, _ = hg.shape
    w, lvl, nlev = _hgrn_tables(cc)
    nc = HGRN_CHUNKS_PER_STEP if t % (HGRN_CHUNKS_PER_STEP * cc) == 0 else 1
    rows = nc * cc

    def part(p):
        return pl.BlockSpec((1, rows, A_WIDTH), lambda i, j, p=p: (i, j, p))

    return pl.pallas_call(
        functools.partial(_hgrn_kernel, cc=cc, nc=nc, nlev=nlev),
        grid=(b, t // rows),
        in_specs=[part(0), part(1), part(2), part(3),
                  pl.BlockSpec((1, A_WIDTH), lambda i, j: (0, 0)),
                  pl.BlockSpec((1, A_WIDTH), lambda i, j: (0, 0)),
                  pl.BlockSpec((1, A_HEADS, A_DK, A_DV), lambda i, j: (i, 0, 0, 0)),
                  pl.BlockSpec(w.shape, lambda i, j: (0, 0)),
                  pl.BlockSpec(lvl.shape, lambda i, j: (0, 0))],
        out_specs=[pl.BlockSpec((1, rows, A_WIDTH), lambda i, j: (i, j, 0)),
                   pl.BlockSpec((1, A_HEADS, A_DK, A_DV), lambda i, j: (i, 0, 0, 0))],
        out_shape=[jax.ShapeDtypeStruct((b, t, A_WIDTH), BF16),
                   jax.ShapeDtypeStruct((b, A_HEADS, A_DK, A_DV), F32)],
        scratch_shapes=[pltpu.VMEM((A_HEADS, A_DV, A_DK), F32),
                        pltpu.VMEM((cc, A_WIDTH), F32)],
        compiler_params=pltpu.CompilerParams(
            dimension_semantics=("parallel", "arbitrary"),
            vmem_limit_bytes=VMEM_LIMIT_BYTES),
        name="hgrn",
    )(hg, hg, hg, hg, lb.reshape(1, A_WIDTH), g_norm.reshape(1, A_WIDTH), s0, w, lvl)


def _fold8(x, op):
    parts = [x[r:r + SUBLANES] for r in range(0, x.shape[0], SUBLANES)]
    while len(parts) > 1:
        parts = [op(parts[i], parts[i + 1]) for i in range(0, len(parts) - 1, 2)] + (
            [parts[-1]] if len(parts) % 2 else [])
    return parts[0]


def _grouped_rhs(xs, heads, width, slot, slots):
    groups = len(xs)
    zero = jnp.zeros((xs[0].shape[0], width), xs[0].dtype)
    rows = []
    for h in heads:
        for g, x in enumerate(xs):
            parts = [zero] * (groups * slots)
            parts[g * slots + slot] = x[:, h * width:(h + 1) * width]
            rows.append(jnp.concatenate(parts, axis=1) if len(parts) > 1 else parts[0])
    return jnp.concatenate(rows, axis=0)


def _dsa_kernel(qi_ref, wit_ref, qb_ref, ki_ref, k_ref, vt_ref, tri_ref, o_ref,
                score_scr, score16_scr, logit_scr, acc_scr, *, groups, gq, qreal, tk, topk, pos0):
    qb = groups * gq
    blk = pl.program_id(1)
    last_pos = pos0 + (blk + 1) * qreal - 1
    extent = jnp.minimum((last_pos // CHUNK + 1) * CHUNK, tk)
    ntile = (extent + KEY_TILE - 1) // KEY_TILE
    lane = lax.broadcasted_iota(jnp.int32, (1, qb), 1)
    qpos = pos0 + blk * qreal + lane % gq
    key_end = jnp.minimum((qpos // CHUNK + 1) * CHUNK, tk)
    neg_inf = jnp.float32(-jnp.inf)

    def lane_cat(parts):
        return jnp.concatenate(parts, axis=1) if len(parts) > 1 else parts[0]

    qis = [qi_ref[g] for g in range(groups)]
    wit = lane_cat([wit_ref[g] for g in range(groups)])
    qi_pairs = [_grouped_rhs(qis, (2 * p, 2 * p + 1), IDX_DIM, 0, 1)
                for p in range(IDX_HEADS // 2)]

    def tile_start(j):
        return pl.multiple_of(j * KEY_TILE, KEY_TILE)

    def tile_loop(body, init):
        def run(first, count, c):
            for u in range(count):
                c = body(first + u, c)
            return c
        carry = lax.fori_loop(0, ntile // 4, lambda i, c: run(4 * i, 4, c), init)
        done = (ntile // 4) * 4
        carry = lax.cond((ntile & 2) != 0, lambda c: run(done, 2, c), lambda c: c, carry)
        return lax.cond((ntile & 1) != 0, lambda c: body(ntile - 1, c), lambda c: c, carry)

    def score_tile(j, carry):
        k0 = tile_start(j)
        ki_t = lane_cat([ki_ref[g, pl.ds(k0, KEY_TILE), :] for g in range(groups)])
        acc = jnp.zeros((KEY_TILE, qb), F32)
        for p in range(IDX_HEADS // 2):
            s2 = jnp.maximum(_dot_nt(ki_t, qi_pairs[p]), 0.0)
            acc = acc + wit[2 * p:2 * p + 1, :] * s2[:, :qb]
            acc = acc + wit[2 * p + 1:2 * p + 2, :] * s2[:, qb:]
        kidx = k0 + lax.broadcasted_iota(jnp.int32, (KEY_TILE, qb), 0)
        masked = jnp.where(kidx < key_end, acc, neg_inf)
        score_scr[pl.ds(k0, KEY_TILE), :] = masked
        score16_scr[pl.ds(k0, KEY_TILE), :] = masked.astype(BF16)
        return carry

    tile_loop(score_tile, 0)

    def count(pred_fn):
        def body(j, acc):
            s = score_scr[pl.ds(tile_start(j), KEY_TILE), :]
            return acc + _fold8(jnp.where(pred_fn(s), 1.0, 0.0), jnp.add)
        acc = tile_loop(body, jnp.zeros((SUBLANES, qb), F32))
        return jnp.sum(acc, axis=0, keepdims=True)

    def count16(cand):
        one, zero = jnp.ones((), BF16), jnp.zeros((), BF16)

        def body(j, acc):
            r = score16_scr[pl.ds(tile_start(j), KEY_TILE), :]
            hit = jnp.where(r >= cand, one, zero)
            parts = [hit[i:i + BF16_ROWS] for i in range(0, KEY_TILE, BF16_ROWS)]
            while len(parts) > 1:
                parts = [parts[i] + parts[i + 1] for i in range(0, len(parts), 2)]
            return acc + parts[0]
        acc = tile_loop(body, jnp.zeros((BF16_ROWS, qb), BF16))
        return jnp.sum(acc.astype(F32), axis=0, keepdims=True)

    def order_key(x):
        return x ^ ((x >> 31) & np.int32(0x7FFFFFFF))

    def decode16(t16):
        b16 = jnp.where(t16 >= 0x8000, t16 & 0x7FFF, ~t16 & 0xFFFF)
        return pltpu.bitcast(b16 << 16, F32)

    def bisect16(i, t16):
        cand16 = t16 | (jnp.int32(1) << (15 - i))
        cnt = count16(decode16(cand16).astype(BF16))
        return jnp.where(cnt >= topk, cand16, t16)

    search = extent > topk
    t16 = lax.fori_loop(0, jnp.where(search, 16, 0), bisect16, jnp.zeros((1, qb), jnp.int32))
    few = t16 <= NEG_INF_PATTERN16
    head = decode16(jnp.maximum(t16, NEG_INF_PATTERN16))
    base = order_key(pltpu.bitcast(head, jnp.int32)) - (1 << 15) - 1

    def bisect17(i, d):
        cand_d = d | (jnp.int32(1) << (16 - i))
        cand = pltpu.bitcast(order_key(base + cand_d), F32)
        cnt = count(lambda s: s >= cand)
        return jnp.where(cnt >= topk, cand_d, d)

    d = lax.fori_loop(0, jnp.where(search, 17, 0), bisect17, jnp.zeros((1, qb), jnp.int32))
    thr = jnp.where(few, neg_inf, pltpu.bitcast(order_key(base + d), F32))
    need = topk - count(lambda s: s > thr)
    need = jnp.where(few, 0.0, need)

    qqs = [qb_ref[g] for g in range(groups)]
    per_q = B_HEADS // B_KV_HEADS
    per_block = LANES // B_HD
    q_pairs = [_grouped_rhs(qqs, (n * per_q, n * per_q + 1), B_HD, n % per_block, per_block)
               for n in range(B_KV_HEADS)]
    tri = tri_ref[...]

    def logit_tile(j, carry):
        offs, mx = carry
        k0 = tile_start(j)
        s = score_scr[pl.ds(k0, KEY_TILE), :]
        tie = jnp.where(s == thr, 1.0, 0.0)
        rank = _dot(tri, tie.astype(BF16)) + offs
        picked = jnp.where(s > thr, 1.0, jnp.where(rank < need, tie, 0.0))
        bias = jnp.where(picked > 0.0, 0.0, NEG_BIG)
        bias2 = jnp.concatenate([bias, bias], axis=1)
        new_mx = []
        for n in range(B_KV_HEADS):
            kblk = (n // per_block) * LANES
            k_t = lane_cat([k_ref[g, pl.ds(k0, KEY_TILE), kblk:kblk + LANES]
                            for g in range(groups)])
            lg = _dot_nt(k_t, q_pairs[n]) + bias2
            logit_scr[n, pl.ds(k0, KEY_TILE), :] = lg
            new_mx.append(jnp.maximum(mx[n], _fold8(lg, jnp.maximum)))
        offs = offs + jnp.sum(_fold8(tie, jnp.add), axis=0, keepdims=True)
        return offs, tuple(new_mx)

    mx0 = tuple(jnp.full((SUBLANES, 2 * qb), NEG_BIG, F32) for _ in range(B_KV_HEADS))
    _, mx = tile_loop(logit_tile, (jnp.zeros((1, qb), F32), mx0))
    mx = [jnp.max(m, axis=0, keepdims=True) for m in mx]

    acc_scr[...] = jnp.zeros_like(acc_scr)

    def pv_tile(j, den):
        k0 = tile_start(j)
        new_den = []
        for n in range(B_KV_HEADS):
            p = jnp.exp2(logit_scr[n, pl.ds(k0, KEY_TILE), :] - mx[n])
            new_den.append(den[n] + _fold8(p, jnp.add))
            vts = [vt_ref[g, n * B_HD:(n + 1) * B_HD, pl.ds(k0, KEY_TILE)] for g in range(groups)]
            vt = jnp.concatenate(vts, axis=0) if groups > 1 else vts[0]
            acc_scr[n] += _dot(vt, p.astype(BF16))
        return tuple(new_den)

    den0 = tuple(jnp.zeros((SUBLANES, 2 * qb), F32) for _ in range(B_KV_HEADS))
    den = tile_loop(pv_tile, den0)
    lane_group = (lax.broadcasted_iota(jnp.int32, (1, 2 * qb), 1) % qb) // gq
    rows = []
    for n in range(B_KV_HEADS):
        acc = acc_scr[n]
        o2 = acc[0:B_HD]
        for g in range(1, groups):
            o2 = jnp.where(lane_group == g, acc[g * B_HD:(g + 1) * B_HD], o2)
        o2 = o2 / jnp.sum(den[n], axis=0, keepdims=True)
        rows += [o2[:, :qb], o2[:, qb:]]
    o_ref[...] = jnp.concatenate(rows, axis=0).T.astype(BF16).reshape(groups, gq, B_WIDTH)


def _dsa_call(qi, wi, qbs, k_bf, vt, ki_bf, tk, pos0):
    b, t, _ = qi.shape
    if t % DSA_QUERIES == 0:
        groups, gq = 1, DSA_QUERIES
    else:
        assert LANES % t == 0 and b % (LANES // t) == 0
        groups, gq = LANES // t, t
    qb = groups * gq
    tkp = ki_bf.shape[1]
    topk = min(TOPK_MAX, tk // 4)
    assert topk <= KEY_TILE
    wit = jnp.swapaxes(wi, 1, 2)
    tri = jnp.asarray(np.tril(np.ones((KEY_TILE, KEY_TILE), np.float32), -1), BF16)

    def q_spec(w):
        return pl.BlockSpec((groups, gq, w), lambda i, j: (i, j, 0))

    def kv_spec(rows, cols):
        return pl.BlockSpec((groups, rows, cols), lambda i, j: (i, 0, 0))

    return pl.pallas_call(
        functools.partial(_dsa_kernel, groups=groups, gq=gq, qreal=gq, tk=tk, topk=topk,
                          pos0=pos0),
        grid=(b // groups, t // gq),
        in_specs=[q_spec(IDX_WIDTH),
                  pl.BlockSpec((groups, IDX_HEADS, gq), lambda i, j: (i, 0, j)),
                  q_spec(B_WIDTH),
                  kv_spec(tkp, IDX_DIM), kv_spec(tkp, KV_WIDTH), kv_spec(KV_WIDTH, tkp),
                  pl.BlockSpec((KEY_TILE, KEY_TILE), lambda i, j: (0, 0))],
        out_specs=q_spec(B_WIDTH),
        out_shape=jax.ShapeDtypeStruct((b, t, B_WIDTH), BF16),
        scratch_shapes=[pltpu.VMEM((tkp, qb), F32),
                        pltpu.VMEM((tkp, qb), BF16),
                        pltpu.VMEM((B_KV_HEADS, tkp, 2 * qb), F32),
                        pltpu.VMEM((B_KV_HEADS, groups * B_HD, 2 * qb), F32)],
        compiler_params=pltpu.CompilerParams(
            dimension_semantics=("parallel", "parallel"),
            vmem_limit_bytes=VMEM_LIMIT_BYTES),
        name="dsa",
    )(qi, wit, qbs, ki_bf, k_bf, vt, tri)


def _out_body(x_ref, oa_ref, ob_ref, mod_ref, n2_ref, nf_ref, wo_ref, w1_ref, w2_ref,
              y_ref, *, bb, tt):
    rows = bb * tt
    x = x_ref[...]
    mod = mod_ref[...]
    g1 = mod[:, :, 2 * D_MODEL:3 * D_MODEL]
    sh2 = mod[:, :, 3 * D_MODEL:4 * D_MODEL]
    sc2 = mod[:, :, 4 * D_MODEL:5 * D_MODEL]
    g2 = mod[:, :, 5 * D_MODEL:6 * D_MODEL]
    oa = oa_ref[...].reshape(rows, A_WIDTH)
    ob = ob_ref[...].reshape(rows, B_WIDTH)
    mix = _dot(oa, wo_ref[0:A_WIDTH, :]) + _dot(ob, wo_ref[A_WIDTH:A_WIDTH + B_WIDTH, :])
    x = x + g1 * mix.reshape(bb, tt, D_MODEL)
    h2 = (_rms(x) * n2_ref[...]) * (1.0 + sc2) + sh2
    u = _dot(h2.reshape(rows, D_MODEL).astype(BF16), w1_ref[...])
    r = jnp.square(jnp.maximum(u, 0.0)).astype(BF16)
    x = x + g2 * _dot(r, w2_ref[...]).reshape(bb, tt, D_MODEL)
    y_ref[...] = _rms(x) * nf_ref[...]


N_ACT = 4
N_CONST = 5


def _out_kernel(*refs, tiles):
    n = len(tiles)
    consts = refs[N_ACT * n:N_ACT * n + N_CONST]
    outs = refs[N_ACT * n + N_CONST:]
    step = pl.program_id(0)
    first = 0
    for g, (bb, tt, steps) in enumerate(tiles):
        @pl.when((step >= first) & (step < first + steps))
        def _(g=g, bb=bb, tt=tt):
            _out_body(*refs[N_ACT * g:N_ACT * (g + 1)], *consts, outs[g], bb=bb, tt=tt)
        first += steps


def _out_call(groups, norm2, norm_f, wo_bf, w1_bf, w2_bf):
    d = D_MODEL
    in_specs, args, out_specs, out_shape, tiles = [], [], [], [], []
    first = 0
    for x, oa, ob, mod, bb, tt in groups:
        b, t, _ = x.shape
        nt = t // tt
        steps = (b // bb) * nt

        def block(i, first=first, steps=steps, nt=nt):
            j = jnp.clip(i - first, 0, steps - 1)
            return j // nt, j % nt

        def act_spec(w, bb=bb, tt=tt, block=block):
            return pl.BlockSpec((bb, tt, w), lambda i: (*block(i), 0))

        in_specs += [act_spec(d), act_spec(A_WIDTH), act_spec(B_WIDTH),
                     pl.BlockSpec((bb, 1, 6 * d), lambda i, block=block: (block(i)[0], 0, 0))]
        args += [x, oa, ob, mod.reshape(b, 1, 6 * d)]
        out_specs.append(act_spec(d))
        out_shape.append(jax.ShapeDtypeStruct((b, t, d), F32))
        tiles.append((bb, tt, steps))
        first += steps

    def const_spec(shape):
        zeros = (0,) * len(shape)
        return pl.BlockSpec(shape, lambda i: zeros, pipeline_mode=pl.Buffered(1))

    in_specs += [const_spec((1, 1, d)), const_spec((1, 1, d)),
                 const_spec(wo_bf.shape), const_spec(w1_bf.shape), const_spec(w2_bf.shape)]
    args += [norm2.reshape(1, 1, d), norm_f.reshape(1, 1, d), wo_bf, w1_bf, w2_bf]
    return pl.pallas_call(
        functools.partial(_out_kernel, tiles=tuple(tiles)),
        grid=(first,),
        in_specs=in_specs,
        out_specs=out_specs,
        out_shape=out_shape,
        compiler_params=pltpu.CompilerParams(
            dimension_semantics=("arbitrary",),
            vmem_limit_bytes=VMEM_LIMIT_BYTES),
        name="out",
    )(*args)


def _layer(x, mod, pos0, s0, k_past, v_past, ki_past, weights, bb, tt, cc):
    norm1, w_main, w_tail, lb, g_norm = weights
    b, t, _ = x.shape
    pos = pos0 + jnp.arange(t)
    outs = _inproj_call(x, mod, norm1, w_main, w_tail, pos, bb, tt, dsa_layouts=k_past is None)
    hg, qbs, k_new, v_new, qi, ki_new, wi = outs[:7]
    oa, s_new = _hgrn_call(hg, lb, g_norm, s0, cc)
    if k_past is None:
        assert t % KEY_TILE == 0
        tk = t
        k_bf, vt, ki_bf = outs[7:]
    else:
        n_past = k_past.shape[1]
        tk = n_past + t
        tkp = -(-tk // KEY_TILE) * KEY_TILE

        def cat(past, new, axis):
            pads = [(0, 0)] * 3
            pads[axis] = (0, tkp - tk)
            return jnp.pad(jnp.concatenate([past.astype(BF16), new.astype(BF16)], axis=axis), pads)

        k_bf = cat(k_past.reshape(b, n_past, KV_WIDTH), k_new, 1)
        vt = cat(jnp.transpose(v_past, (0, 2, 3, 1)).reshape(b, KV_WIDTH, n_past),
                 jnp.swapaxes(v_new, 1, 2), 2)
        ki_bf = cat(ki_past, ki_new, 1)
    ob = _dsa_call(qi, wi, qbs, k_bf, vt, ki_bf, tk, pos0)
    return ((x, oa, ob, mod, bb, tt), k_new.reshape(b, t, B_KV_HEADS, B_HD),
            v_new.reshape(b, t, B_KV_HEADS, B_HD), ki_new, s_new)


def _tiling(b, t):
    if t >= ROW_TILE:
        assert t % ROW_TILE == 0
        bb, tt = 1, ROW_TILE
    else:
        bb, tt = min(b, ROW_TILE // t), t
        assert b % bb == 0
    return dict(bb=bb, tt=tt, cc=min(HGRN_CHUNK, t))


def kernel(x_prompt, x_sample, cache_k, cache_v, cache_k_idx, state_hgrn, c_prompt, c_sample,
           w_mod, b_mod, norm1, w_in, lb_logits, g_norm_a, w_out, norm2, w_ff1, w_ff2, norm_f):
    depth = w_in.shape[0]
    assert depth == 1, "kernel is written for the single-layer configuration"
    lb_all = jnp.cumsum(jax.nn.softmax(lb_logits.astype(F32), axis=0), axis=0)
    bp, tp, _ = x_prompt.shape
    bs, ts, _ = x_sample.shape
    past = cache_k.shape[2]
    l = 0
    mod = _mod_call(jnp.concatenate([c_prompt, c_sample], axis=0), w_mod[l], b_mod[l])
    w_main = w_in[l][:, :OFF_KI].astype(BF16)
    w_tail = jnp.pad(w_in[l][:, OFF_KI:].astype(BF16), ((0, 0), (0, LANES - (IN_WIDTH - OFF_KI))))
    weights = (norm1[l], w_main, w_tail, lb_all[l], g_norm_a[l])
    s0 = jnp.zeros((bp, A_HEADS, A_DK, A_DV), F32)
    tail_p, kp, vp, kip, sp = _layer(x_prompt, mod[:bp], 0, s0, None, None, None, weights,
                                     **_tiling(bp, tp))
    tail_s, ks, vs, kis, ss = _layer(x_sample, mod[bp:], past, state_hgrn[l], cache_k[l],
                                     cache_v[l], cache_k_idx[l], weights, **_tiling(bs, ts))
    yp, ys = _out_call([tail_p, tail_s], norm2[l], norm_f, w_out[l].astype(BF16),
                       w_ff1[l].astype(BF16), w_ff2[l].astype(BF16))
    return (yp, ys, kp[None], vp[None], kip[None], sp[None],
            ks[None], vs[None], kis[None], ss[None])
```

```python
import functools

import numpy as np
import jax
import jax.numpy as jnp
from jax import lax
from jax.experimental import pallas as pl
from jax.experimental.pallas import tpu as pltpu

D_MODEL = 1024
CHUNK = 64
A_HEADS = 4
A_DK = 128
A_DV = 128
A_WIDTH = A_HEADS * A_DV
B_HEADS = 8
B_KV_HEADS = 4
B_HD = 64
B_WIDTH = B_HEADS * B_HD
KV_WIDTH = B_KV_HEADS * B_HD
IDX_HEADS = 8
IDX_DIM = 64
IDX_WIDTH = IDX_HEADS * IDX_DIM
TOPK_MAX = 256
ROT_FRAC = 4
ROPE_THETA = 500000.0
EPS = 1e-6
IN_WIDTH = 4 * A_WIDTH + B_WIDTH + 2 * KV_WIDTH + IDX_WIDTH + IDX_DIM + IDX_HEADS

LANES = 128
SUBLANES = 8
BF16_ROWS = 16
KEY_TILE = 256
DSA_QUERIES = 256
MOD_COL_TILE = 2048
ROW_TILE = 512
HGRN_CHUNK = 128
HGRN_CHUNKS_PER_STEP = 8
VMEM_LIMIT_BYTES = 56 * 1024 * 1024

F32 = jnp.float32
BF16 = jnp.bfloat16
NEG_INF_PATTERN16 = 0x007F
NEG_BIG = -1e30
LOG2_E = 1.4426950408889634

OFF_HG = 0
OFF_QB = 4 * A_WIDTH
OFF_KB = OFF_QB + B_WIDTH
OFF_VB = OFF_KB + KV_WIDTH
OFF_QI = OFF_VB + KV_WIDTH
OFF_KI = OFF_QI + IDX_WIDTH
OFF_WI = OFF_KI + IDX_DIM


def _dot(a, b):
    return jnp.dot(a, b, preferred_element_type=F32)


def _dot_nt(a, b):
    return lax.dot_general(a, b, (((1,), (1,)), ((), ())), preferred_element_type=F32)


def _silu(x):
    return x * jax.nn.sigmoid(x)


def _rms(x):
    return x * lax.rsqrt(jnp.mean(jnp.square(x), axis=-1, keepdims=True) + EPS)


def _block_diag(x, width):
    zero = jnp.zeros((x.shape[0], width), x.dtype)
    return jnp.concatenate([jnp.concatenate([x[:, :width], zero], axis=1),
                            jnp.concatenate([zero, x[:, width:]], axis=1)], axis=0)


def _mod_kernel(c_ref, w_ref, b_ref, o_ref):
    a = _silu(c_ref[...])
    w = w_ref[...]
    a_hi = a.astype(BF16)
    a_lo = (a - a_hi.astype(F32)).astype(BF16)
    w_hi = w.astype(BF16)
    w_lo = (w - w_hi.astype(F32)).astype(BF16)
    o_ref[...] = _dot(a_hi, w_hi) + _dot(a_lo, w_hi) + _dot(a_hi, w_lo) + b_ref[...]


def _mod_call(c, w_mod, b_mod):
    rows, d = c.shape
    n = w_mod.shape[1]
    tn = MOD_COL_TILE
    return pl.pallas_call(
        _mod_kernel,
        grid=(n // tn,),
        in_specs=[pl.BlockSpec((rows, d), lambda j: (0, 0)),
                  pl.BlockSpec((d, tn), lambda j: (0, j)),
                  pl.BlockSpec((1, tn), lambda j: (0, j))],
        out_specs=pl.BlockSpec((rows, tn), lambda j: (0, j)),
        out_shape=jax.ShapeDtypeStruct((rows, n), F32),
        compiler_params=pltpu.CompilerParams(vmem_limit_bytes=VMEM_LIMIT_BYTES),
        name="mod",
    )(c, w_mod, b_mod.reshape(1, n))


def _rope(x, cos, sin_lo, sin_hi):
    half = B_HD // ROT_FRAC // 2
    return (x * cos + pltpu.roll(x, half, 1) * sin_hi
            + pltpu.roll(x, LANES - half, 1) * sin_lo)


def _inproj_kernel(x_ref, mod_ref, n1_ref, w_ref, wt_ref, cos_ref, slo_ref, shi_ref,
                   hg_ref, qb_ref, k_ref, v_ref, qi_ref, ki_ref, wi_ref, *dsa_refs, bb, tt):
    rows = bb * tt
    x = x_ref[...]
    mod = mod_ref[...]
    sh1 = mod[:, :, 0:D_MODEL]
    sc1 = mod[:, :, D_MODEL:2 * D_MODEL]
    h = (_rms(x) * n1_ref[...]) * (1.0 + sc1) + sh1
    h = h.reshape(rows, D_MODEL).astype(BF16)
    z_att = _dot(h, w_ref[:, OFF_QB:])
    z_tail = _dot(h, wt_ref[...])
    cos, slo, shi = cos_ref[...], slo_ref[...], shi_ref[...]

    def cols(off, width):
        return z_att[:, off - OFF_QB:off - OFF_QB + width]

    def rope_cols(off, width):
        return [_rope(cols(off + j, LANES), cos, slo, shi) for j in range(0, width, LANES)]

    scale = B_HD ** -0.5 * LOG2_E
    qb = jnp.concatenate(rope_cols(OFF_QB, B_WIDTH), axis=1) * scale
    qb_ref[...] = qb.astype(BF16).reshape(bb, tt, B_WIDTH)
    kb = jnp.concatenate(rope_cols(OFF_KB, KV_WIDTH), axis=1)
    k_ref[...] = kb.reshape(bb, tt, KV_WIDTH)
    vb = cols(OFF_VB, KV_WIDTH)
    v_ref[...] = vb.reshape(bb, tt, KV_WIDTH)
    qi = jnp.concatenate(rope_cols(OFF_QI, IDX_WIDTH), axis=1)
    qi_ref[...] = qi.astype(BF16).reshape(bb, tt, IDX_WIDTH)
    last = _rope(z_tail, cos, slo, shi)
    ki_ref[...] = last[:, 0:IDX_DIM].reshape(bb, tt, IDX_DIM)
    wi = z_tail[:, OFF_WI - OFF_KI:OFF_WI - OFF_KI + IDX_HEADS] * (IDX_WIDTH ** -0.5)
    wi_ref[...] = wi.reshape(bb, tt, IDX_HEADS)
    if dsa_refs:
        kbf_ref, vt_ref, kibf_ref = dsa_refs
        kbf_ref[0] = kb.astype(BF16)
        vt_ref[0] = vb.T.astype(BF16)
        kibf_ref[0] = last[:, 0:IDX_DIM].astype(BF16)
    hg_ref[...] = _dot(h, w_ref[:, OFF_HG:OFF_QB]).reshape(bb, tt, 4 * A_WIDTH)


def _rope_tables(pos, reps):
    rot = B_HD // ROT_FRAC
    half = rot // 2
    inv = jnp.power(ROPE_THETA, -jnp.arange(half, dtype=F32) * (2.0 / rot))
    ang = pos.astype(F32)[:, None] * inv[None, :]
    cos, sin = jnp.cos(ang), jnp.sin(ang)
    t = pos.shape[0]
    ones = jnp.ones((t, B_HD - rot), F32)
    zeros = jnp.zeros((t, B_HD - rot), F32)
    zh = jnp.zeros((t, half), F32)
    cos_h = jnp.concatenate([cos, cos, ones], axis=1)
    slo_h = jnp.concatenate([-sin, zh, zeros], axis=1)
    shi_h = jnp.concatenate([zh, sin, zeros], axis=1)
    per = LANES // B_HD
    return tuple(jnp.tile(a, (reps, per)) for a in (cos_h, slo_h, shi_h))


def _inproj_call(x, mod, norm1, w_main, w_tail, pos, bb, tt, dsa_layouts):
    b, t, d = x.shape
    cos, slo, shi = _rope_tables(pos, bb)
    rows = bb * tt
    if bb == 1:
        tab_spec = pl.BlockSpec((tt, LANES), lambda i, j: (j, 0))
    else:
        tab_spec = pl.BlockSpec((rows, LANES), lambda i, j: (0, 0))

    def act_spec(w):
        return pl.BlockSpec((bb, tt, w), lambda i, j: (i, j, 0))

    def out(w, dt):
        return jax.ShapeDtypeStruct((b, t, w), dt)

    out_specs = [act_spec(4 * A_WIDTH), act_spec(B_WIDTH), act_spec(KV_WIDTH),
                 act_spec(KV_WIDTH), act_spec(IDX_WIDTH), act_spec(IDX_DIM),
                 act_spec(IDX_HEADS)]
    out_shape = [out(4 * A_WIDTH, F32), out(B_WIDTH, BF16), out(KV_WIDTH, F32),
                 out(KV_WIDTH, F32), out(IDX_WIDTH, BF16), out(IDX_DIM, F32),
                 out(IDX_HEADS, F32)]
    if dsa_layouts:
        assert bb == 1
        out_specs += [act_spec(KV_WIDTH), pl.BlockSpec((1, KV_WIDTH, tt), lambda i, j: (i, 0, j)),
                      act_spec(IDX_DIM)]
        out_shape += [out(KV_WIDTH, BF16), jax.ShapeDtypeStruct((b, KV_WIDTH, t), BF16),
                      out(IDX_DIM, BF16)]

    return pl.pallas_call(
        functools.partial(_inproj_kernel, bb=bb, tt=tt),
        grid=(b // bb, t // tt),
        in_specs=[act_spec(d),
                  pl.BlockSpec((bb, 1, 6 * d), lambda i, j: (i, 0, 0)),
                  pl.BlockSpec((1, 1, d), lambda i, j: (0, 0, 0)),
                  pl.BlockSpec(w_main.shape, lambda i, j: (0, 0)),
                  pl.BlockSpec(w_tail.shape, lambda i, j: (0, 0)),
                  tab_spec, tab_spec, tab_spec],
        out_specs=out_specs,
        out_shape=out_shape,
        compiler_params=pltpu.CompilerParams(
            dimension_semantics=("parallel", "parallel"),
            vmem_limit_bytes=VMEM_LIMIT_BYTES),
        name="inproj",
    )(x, mod.reshape(b, 1, 6 * d), norm1.reshape(1, 1, d), w_main, w_tail, cos, slo, shi)


def _hgrn_tables(cc):
    nlev = int(np.log2(cc))
    t = np.arange(cc)[:, None]
    u = np.arange(cc)[None, :]
    mats = [(u <= t).astype(np.float32)]
    for l in range(nlev):
        m = cc >> (l + 1)
        if m < SUBLANES:
            ref = (t // (2 * m)) * (2 * m) + m - 1
            qside = ((t // m) % 2) == 1
            seg = np.where(qside, (u > ref) & (u <= t), (u > t) & (u <= ref))
            mats.append(-seg.astype(np.float32))
    w = np.concatenate(mats, axis=0)
    w = np.concatenate([w] * 3, axis=1)
    lvl = np.full((cc, cc), -1, np.int32)
    for l in range(nlev):
        m = cc >> (l + 1)
        same_parent = (t // (2 * m)) == (u // (2 * m))
        lvl[same_parent & ((t // m) % 2 == 1) & ((u // m) % 2 == 0)] = l
    lvl[np.arange(cc), np.arange(cc)] = nlev
    return jnp.asarray(w, BF16), jnp.asarray(lvl), nlev


def _level_exponent(cum_ref, cols, cum, m):
    cc, n = cum.shape
    parts = []
    for p in range(0, cc, 2 * m):
        ref = jnp.broadcast_to(cum_ref[p + m - 1:p + m, cols], (m, n))
        parts += [cum[p:p + m] - ref, ref - cum[p + m:p + 2 * m]]
    return jnp.concatenate(parts, axis=0)


def _hgrn_kernel(q_ref, f_ref, i_ref, g_ref, lb_ref, gn_ref, s0_ref, w_ref, lvl_ref,
                 o_ref, s_out_ref, st_scr, cum_scr, *, cc, nc, nlev):
    ci = pl.program_id(1)

    @pl.when(ci == 0)
    def _():
        for h in range(A_HEADS):
            st_scr[h] = s0_ref[0, h].T

    row = lax.broadcasted_iota(jnp.int32, (cc, 2 * A_DK), 0)
    qsides = [((row // (cc >> (l + 1))) % 2) == 1 for l in range(nlev)]
    n_small = sum(1 for l in range(nlev) if (cc >> (l + 1)) < SUBLANES)
    qsides16 = [jnp.where(m, 1.0, 0.0).astype(BF16) > 0 for m in qsides]
    lvl = jnp.concatenate([lvl_ref[...]] * 2, axis=1)
    lvl_masks = [lvl == l for l in range(nlev + 1)]
    w = w_ref[...]
    lb = lb_ref[...]
    f_all = lb + (1.0 - lb) * jax.nn.sigmoid(f_ref[0])
    nl = -jnp.log2(f_all)
    nl_hi = nl.astype(BF16)
    rest = nl - nl_hi.astype(F32)
    nl_mid = rest.astype(BF16)
    nl_lo = (rest - nl_mid.astype(F32)).astype(BF16)
    for c in range(nc):
        rs = slice(c * cc, (c + 1) * cc)
        cum_scr[...] = _dot(w[0:cc], jnp.concatenate([nl_hi[rs], nl_mid[rs], nl_lo[rs]], axis=0))
        sums = _dot(w[cc:, 0:2 * cc], jnp.concatenate([nl_hi[rs], nl_mid[rs]], axis=0))
        for hp in range(A_HEADS // 2):
            h0, h1 = 2 * hp, 2 * hp + 1
            sl = slice(h0 * A_DK, (h1 + 1) * A_DK)
            q = _silu(q_ref[0, rs, sl])
            kk = 1.0 - f_all[rs, sl]
            v = i_ref[0, rs, sl]
            cum = cum_scr[:, sl]
            q16, k16, v16 = q.astype(BF16), kk.astype(BF16), v.astype(BF16)
            attn = jnp.zeros((cc, 2 * cc), F32)
            for l in range(nlev):
                m = cc >> (l + 1)
                if m >= SUBLANES:
                    ex = _level_exponent(cum_scr, sl, cum, m)
                else:
                    small = l - (nlev - n_small)
                    ex = sums[small * cc:(small + 1) * cc, sl]
                e16 = jnp.exp2(ex).astype(BF16)
                if m % BF16_ROWS == 0:
                    qk = jnp.concatenate([(q16 if (r // m) % 2 else k16)[r:r + m]
                                          for r in range(0, cc, m)], axis=0)
                else:
                    qk = jnp.where(qsides16[l], q16, k16)
                xl = qk * e16
                attn = jnp.where(lvl_masks[l], _dot_nt(xl, _block_diag(xl, A_DK)), attn)
            attn = jnp.where(lvl_masks[nlev], _dot_nt(q16, _block_diag(k16, A_DK)), attn)
            last = jnp.broadcast_to(cum_scr[cc - 1:cc, sl], (cc, 2 * A_DK))
            qg = (q * jnp.exp2(-cum)).astype(BF16)
            kg = (kk * jnp.exp2(cum - last)).astype(BF16)
            st = jnp.concatenate([st_scr[h0], st_scr[h1]], axis=1)
            o = (_dot(attn.astype(BF16), _block_diag(v16, A_DV))
                 + _dot_nt(qg, _block_diag(st.astype(BF16), A_DK)))
            dec = jnp.exp2(-cum[cc - 1:cc, :])
            upd = _dot(v.T.astype(BF16), kg)
            st_scr[h0] = st[:, :A_DK] * dec[:, :A_DK] + upd[:A_DV, :A_DK]
            st_scr[h1] = st[:, A_DK:] * dec[:, A_DK:] + upd[A_DV:, A_DK:]
            for i, h in enumerate((h0, h1)):
                hs = slice(h * A_DV, (h + 1) * A_DV)
                y = _rms(o[:, i * A_DV:(i + 1) * A_DV]) * gn_ref[:, hs]
                o_ref[0, rs, hs] = (y * _silu(g_ref[0, rs, hs])).astype(BF16)

    @pl.when(ci == pl.num_programs(1) - 1)
    def _():
        for h in range(A_HEADS):
            s_out_ref[0, h] = st_scr[h].T


def _hgrn_call(hg, lb, g_norm, s0, cc):
    b, t, _ = hg.shape
    w, lvl, nlev = _hgrn_tables(cc)
    nc = HGRN_CHUNKS_PER_STEP if t % (HGRN_CHUNKS_PER_STEP * cc) == 0 else 1
    rows = nc * cc

    def part(p):
        return pl.BlockSpec((1, rows, A_WIDTH), lambda i, j, p=p: (i, j, p))

    return pl.pallas_call(
        functools.partial(_hgrn_kernel, cc=cc, nc=nc, nlev=nlev),
        grid=(b, t // rows),
        in_specs=[part(0), part(1), part(2), part(3),
                  pl.BlockSpec((1, A_WIDTH), lambda i, j: (0, 0)),
                  pl.BlockSpec((1, A_WIDTH), lambda i, j: (0, 0)),
                  pl.BlockSpec((1, A_HEADS, A_DK, A_DV), lambda i, j: (i, 0, 0, 0)),
                  pl.BlockSpec(w.shape, lambda i, j: (0, 0)),
                  pl.BlockSpec(lvl.shape, lambda i, j: (0, 0))],
        out_specs=[pl.BlockSpec((1, rows, A_WIDTH), lambda i, j: (i, j, 0)),
                   pl.BlockSpec((1, A_HEADS, A_DK, A_DV), lambda i, j: (i, 0, 0, 0))],
        out_shape=[jax.ShapeDtypeStruct((b, t, A_WIDTH), BF16),
                   jax.ShapeDtypeStruct((b, A_HEADS, A_DK, A_DV), F32)],
        scratch_shapes=[pltpu.VMEM((A_HEADS, A_DV, A_DK), F32),
                        pltpu.VMEM((cc, A_WIDTH), F32)],
        compiler_params=pltpu.CompilerParams(
            dimension_semantics=("parallel", "arbitrary"),
            vmem_limit_bytes=VMEM_LIMIT_BYTES),
        name="hgrn",
    )(hg, hg, hg, hg, lb.reshape(1, A_WIDTH), g_norm.reshape(1, A_WIDTH), s0, w, lvl)


def _fold8(x, op):
    parts = [x[r:r + SUBLANES] for r in range(0, x.shape[0], SUBLANES)]
    while len(parts) > 1:
        parts = [op(parts[i], parts[i + 1]) for i in range(0, len(parts) - 1, 2)] + (
            [parts[-1]] if len(parts) % 2 else [])
    return parts[0]


def _grouped_rhs(xs, heads, width, slot, slots):
    groups = len(xs)
    zero = jnp.zeros((xs[0].shape[0], width), xs[0].dtype)
    rows = []
    for h in heads:
        for g, x in enumerate(xs):
            parts = [zero] * (groups * slots)
            parts[g * slots + slot] = x[:, h * width:(h + 1) * width]
            rows.append(jnp.concatenate(parts, axis=1) if len(parts) > 1 else parts[0])
    return jnp.concatenate(rows, axis=0)


def _dsa_kernel(qi_ref, wit_ref, qb_ref, ki_ref, k_ref, vt_ref, tri_ref, o_ref,
                score_scr, score16_scr, logit_scr, acc_scr, *, groups, gq, qreal, tk, topk, pos0):
    qb = groups * gq
    blk = pl.program_id(1)
    last_pos = pos0 + (blk + 1) * qreal - 1
    extent = jnp.minimum((last_pos // CHUNK + 1) * CHUNK, tk)
    ntile = (extent + KEY_TILE - 1) // KEY_TILE
    lane = lax.broadcasted_iota(jnp.int32, (1, qb), 1)
    qpos = pos0 + blk * qreal + lane % gq
    key_end = jnp.minimum((qpos // CHUNK + 1) * CHUNK, tk)
    neg_inf = jnp.float32(-jnp.inf)

    def lane_cat(parts):
        return jnp.concatenate(parts, axis=1) if len(parts) > 1 else parts[0]

    qis = [qi_ref[g] for g in range(groups)]
    wit = lane_cat([wit_ref[g] for g in range(groups)])
    qi_pairs = [_grouped_rhs(qis, (2 * p, 2 * p + 1), IDX_DIM, 0, 1)
                for p in range(IDX_HEADS // 2)]

    def tile_start(j):
        return pl.multiple_of(j * KEY_TILE, KEY_TILE)

    def tile_loop(body, init):
        def run(first, count, c):
            for u in range(count):
                c = body(first + u, c)
            return c
        carry = lax.fori_loop(0, ntile // 4, lambda i, c: run(4 * i, 4, c), init)
        done = (ntile // 4) * 4
        carry = lax.cond((ntile & 2) != 0, lambda c: run(done, 2, c), lambda c: c, carry)
        return lax.cond((ntile & 1) != 0, lambda c: body(ntile - 1, c), lambda c: c, carry)

    def score_tile(j, carry):
        k0 = tile_start(j)
        ki_t = lane_cat([ki_ref[g, pl.ds(k0, KEY_TILE), :] for g in range(groups)])
        acc = jnp.zeros((KEY_TILE, qb), F32)
        for p in range(IDX_HEADS // 2):
            s2 = jnp.maximum(_dot_nt(ki_t, qi_pairs[p]), 0.0)
            acc = acc + wit[2 * p:2 * p + 1, :] * s2[:, :qb]
            acc = acc + wit[2 * p + 1:2 * p + 2, :] * s2[:, qb:]
        kidx = k0 + lax.broadcasted_iota(jnp.int32, (KEY_TILE, qb), 0)
        masked = jnp.where(kidx < key_end, acc, neg_inf)
        score_scr[pl.ds(k0, KEY_TILE), :] = masked
        score16_scr[pl.ds(k0, KEY_TILE), :] = masked.astype(BF16)
        return carry

    tile_loop(score_tile, 0)

    def count(pred_fn):
        def body(j, acc):
            s = score_scr[pl.ds(tile_start(j), KEY_TILE), :]
            return acc + _fold8(jnp.where(pred_fn(s), 1.0, 0.0), jnp.add)
        acc = tile_loop(body, jnp.zeros((SUBLANES, qb), F32))
        return jnp.sum(acc, axis=0, keepdims=True)

    def count16(cand):
        one, zero = jnp.ones((), BF16), jnp.zeros((), BF16)

        def body(j, acc):
            r = score16_scr[pl.ds(tile_start(j), KEY_TILE), :]
            hit = jnp.where(r >= cand, one, zero)
            parts = [hit[i:i + BF16_ROWS] for i in range(0, KEY_TILE, BF16_ROWS)]
            while len(parts) > 1:
                parts = [parts[i] + parts[i + 1] for i in range(0, len(parts), 2)]
            return acc + parts[0]
        acc = tile_loop(body, jnp.zeros((BF16_ROWS, qb), BF16))
        return jnp.sum(acc.astype(F32), axis=0, keepdims=True)

    def order_key(x):
        return x ^ ((x >> 31) & np.int32(0x7FFFFFFF))

    def decode16(t16):
        b16 = jnp.where(t16 >= 0x8000, t16 & 0x7FFF, ~t16 & 0xFFFF)
        return pltpu.bitcast(b16 << 16, F32)

    def bisect16(i, t16):
        cand16 = t16 | (jnp.int32(1) << (15 - i))
        cnt = count16(decode16(cand16).astype(BF16))
        return jnp.where(cnt >= topk, cand16, t16)

    search = extent > topk
    t16 = lax.fori_loop(0, jnp.where(search, 16, 0), bisect16, jnp.zeros((1, qb), jnp.int32))
    few = t16 <= NEG_INF_PATTERN16
    head = decode16(jnp.maximum(t16, NEG_INF_PATTERN16))
    base = order_key(pltpu.bitcast(head, jnp.int32)) - (1 << 15) - 1

    def bisect17(i, d):
        cand_d = d | (jnp.int32(1) << (16 - i))
        cand = pltpu.bitcast(order_key(base + cand_d), F32)
        cnt = count(lambda s: s >= cand)
        return jnp.where(cnt >= topk, cand_d, d)

    d = lax.fori_loop(0, jnp.where(search, 17, 0), bisect17, jnp.zeros((1, qb), jnp.int32))
    thr = jnp.where(few, neg_inf, pltpu.bitcast(order_key(base + d), F32))
    need = topk - count(lambda s: s > thr)
    need = jnp.where(few, 0.0, need)

    qqs = [qb_ref[g] for g in range(groups)]
    per_q = B_HEADS // B_KV_HEADS
    per_block = LANES // B_HD
    q_pairs = [_grouped_rhs(qqs, (n * per_q, n * per_q + 1), B_HD, n % per_block, per_block)
               for n in range(B_KV_HEADS)]
    tri = tri_ref[...]

    def logit_tile(j, carry):
        offs, mx = carry
        k0 = tile_start(j)
        s = score_scr[pl.ds(k0, KEY_TILE), :]
        tie = jnp.where(s == thr, 1.0, 0.0)
        rank = _dot(tri, tie.astype(BF16)) + offs
        picked = jnp.where(s > thr, 1.0, jnp.where(rank < need, tie, 0.0))
        bias = jnp.where(picked > 0.0, 0.0, NEG_BIG)
        bias2 = jnp.concatenate([bias, bias], axis=1)
        new_mx = []
        for n in range(B_KV_HEADS):
            kblk = (n // per_block) * LANES
            k_t = lane_cat([k_ref[g, pl.ds(k0, KEY_TILE), kblk:kblk + LANES]
                            for g in range(groups)])
            lg = _dot_nt(k_t, q_pairs[n]) + bias2
            logit_scr[n, pl.ds(k0, KEY_TILE), :] = lg
            new_mx.append(jnp.maximum(mx[n], _fold8(lg, jnp.maximum)))
        offs = offs + jnp.sum(_fold8(tie, jnp.add), axis=0, keepdims=True)
        return offs, tuple(new_mx)

    mx0 = tuple(jnp.full((SUBLANES, 2 * qb), NEG_BIG, F32) for _ in range(B_KV_HEADS))
    _, mx = tile_loop(logit_tile, (jnp.zeros((1, qb), F32), mx0))
    mx = [jnp.max(m, axis=0, keepdims=True) for m in mx]

    acc_scr[...] = jnp.zeros_like(acc_scr)

    def pv_tile(j, den):
        k0 = tile_start(j)
        new_den = []
        for n in range(B_KV_HEADS):
            p = jnp.exp2(logit_scr[n, pl.ds(k0, KEY_TILE), :] - mx[n])
            new_den.append(den[n] + _fold8(p, jnp.add))
            vts = [vt_ref[g, n * B_HD:(n + 1) * B_HD, pl.ds(k0, KEY_TILE)] for g in range(groups)]
            vt = jnp.concatenate(vts, axis=0) if groups > 1 else vts[0]
            acc_scr[n] += _dot(vt, p.astype(BF16))
        return tuple(new_den)

    den0 = tuple(jnp.zeros((SUBLANES, 2 * qb), F32) for _ in range(B_KV_HEADS))
    den = tile_loop(pv_tile, den0)
    lane_group = (lax.broadcasted_iota(jnp.int32, (1, 2 * qb), 1) % qb) // gq
    rows = []
    for n in range(B_KV_HEADS):
        acc = acc_scr[n]
        o2 = acc[0:B_HD]
        for g in range(1, groups):
            o2 = jnp.where(lane_group == g, acc[g * B_HD:(g + 1) * B_HD], o2)
        o2 = o2 / jnp.sum(den[n], axis=0, keepdims=True)
        rows += [o2[:, :qb], o2[:, qb:]]
    o_ref[...] = jnp.concatenate(rows, axis=0).T.astype(BF16).reshape(groups, gq, B_WIDTH)


def _dsa_call(qi, wi, qbs, k_bf, vt, ki_bf, tk, pos0):
    b, t, _ = qi.shape
    if t % DSA_QUERIES == 0:
        groups, gq = 1, DSA_QUERIES
    else:
        assert LANES % t == 0 and b % (LANES // t) == 0
        groups, gq = LANES // t, t
    qb = groups * gq
    tkp = ki_bf.shape[1]
    topk = min(TOPK_MAX, tk // 4)
    assert topk <= KEY_TILE
    wit = jnp.swapaxes(wi, 1, 2)
    tri = jnp.asarray(np.tril(np.ones((KEY_TILE, KEY_TILE), np.float32), -1), BF16)

    def q_spec(w):
        return pl.BlockSpec((groups, gq, w), lambda i, j: (i, j, 0))

    def kv_spec(rows, cols):
        return pl.BlockSpec((groups, rows, cols), lambda i, j: (i, 0, 0))

    return pl.pallas_call(
        functools.partial(_dsa_kernel, groups=groups, gq=gq, qreal=gq, tk=tk, topk=topk,
                          pos0=pos0),
        grid=(b // groups, t // gq),
        in_specs=[q_spec(IDX_WIDTH),
                  pl.BlockSpec((groups, IDX_HEADS, gq), lambda i, j: (i, 0, j)),
                  q_spec(B_WIDTH),
                  kv_spec(tkp, IDX_DIM), kv_spec(tkp, KV_WIDTH), kv_spec(KV_WIDTH, tkp),
                  pl.BlockSpec((KEY_TILE, KEY_TILE), lambda i, j: (0, 0))],
        out_specs=q_spec(B_WIDTH),
        out_shape=jax.ShapeDtypeStruct((b, t, B_WIDTH), BF16),
        scratch_shapes=[pltpu.VMEM((tkp, qb), F32),
                        pltpu.VMEM((tkp, qb), BF16),
                        pltpu.VMEM((B_KV_HEADS, tkp, 2 * qb), F32),
                        pltpu.VMEM((B_KV_HEADS, groups * B_HD, 2 * qb), F32)],
        compiler_params=pltpu.CompilerParams(
            dimension_semantics=("parallel", "parallel"),
            vmem_limit_bytes=VMEM_LIMIT_BYTES),
        name="dsa",
    )(qi, wit, qbs, ki_bf, k_bf, vt, tri)


def _out_kernel(x_ref, oa_ref, ob_ref, mod_ref, n2_ref, nf_ref, wo_ref, w1_ref, w2_ref,
                y_ref, *, bb, tt):
    rows = bb * tt
    x = x_ref[...]
    mod = mod_ref[...]
    g1 = mod[:, :, 2 * D_MODEL:3 * D_MODEL]
    sh2 = mod[:, :, 3 * D_MODEL:4 * D_MODEL]
    sc2 = mod[:, :, 4 * D_MODEL:5 * D_MODEL]
    g2 = mod[:, :, 5 * D_MODEL:6 * D_MODEL]
    oa = oa_ref[...].reshape(rows, A_WIDTH)
    ob = ob_ref[...].reshape(rows, B_WIDTH)
    mix = _dot(oa, wo_ref[0:A_WIDTH, :]) + _dot(ob, wo_ref[A_WIDTH:A_WIDTH + B_WIDTH, :])
    x = x + g1 * mix.reshape(bb, tt, D_MODEL)
    h2 = (_rms(x) * n2_ref[...]) * (1.0 + sc2) + sh2
    u = _dot(h2.reshape(rows, D_MODEL).astype(BF16), w1_ref[...])
    r = jnp.square(jnp.maximum(u, 0.0)).astype(BF16)
    x = x + g2 * _dot(r, w2_ref[...]).reshape(bb, tt, D_MODEL)
    y_ref[...] = _rms(x) * nf_ref[...]


def _out_call(x, oa, ob, mod, norm2, norm_f, wo_bf, w1_bf, w2_bf, bb, tt):
    b, t, d = x.shape

    def act_spec(w):
        return pl.BlockSpec((bb, tt, w), lambda i, j: (i, j, 0))

    def const_spec(shape):
        zeros = (0,) * len(shape)
        return pl.BlockSpec(shape, lambda i, j: zeros, pipeline_mode=pl.Buffered(1))

    return pl.pallas_call(
        functools.partial(_out_kernel, bb=bb, tt=tt),
        grid=(b // bb, t // tt),
        in_specs=[act_spec(d), act_spec(A_WIDTH), act_spec(B_WIDTH),
                  pl.BlockSpec((bb, 1, 6 * d), lambda i, j: (i, 0, 0)),
                  const_spec((1, 1, d)), const_spec((1, 1, d)),
                  const_spec(wo_bf.shape), const_spec(w1_bf.shape), const_spec(w2_bf.shape)],
        out_specs=act_spec(d),
        out_shape=jax.ShapeDtypeStruct((b, t, d), F32),
        compiler_params=pltpu.CompilerParams(
            dimension_semantics=("parallel", "parallel"),
            vmem_limit_bytes=VMEM_LIMIT_BYTES),
        name="out",
    )(x, oa, ob, mod.reshape(b, 1, 6 * d), norm2.reshape(1, 1, d), norm_f.reshape(1, 1, d),
      wo_bf, w1_bf, w2_bf)


def _layer(x, mod, pos0, s0, k_past, v_past, ki_past, weights, bb, tt, cc):
    norm1, w_main, w_tail, lb, g_norm, wo_bf, norm2, w1_bf, w2_bf, norm_f = weights
    b, t, _ = x.shape
    pos = pos0 + jnp.arange(t)
    outs = _inproj_call(x, mod, norm1, w_main, w_tail, pos, bb, tt, dsa_layouts=k_past is None)
    hg, qbs, k_new, v_new, qi, ki_new, wi = outs[:7]
    oa, s_new = _hgrn_call(hg, lb, g_norm, s0, cc)
    if k_past is None:
        assert t % KEY_TILE == 0
        tk = t
        k_bf, vt, ki_bf = outs[7:]
    else:
        n_past = k_past.shape[1]
        tk = n_past + t
        tkp = -(-tk // KEY_TILE) * KEY_TILE

        def cat(past, new, axis):
            pads = [(0, 0)] * 3
            pads[axis] = (0, tkp - tk)
            return jnp.pad(jnp.concatenate([past.astype(BF16), new.astype(BF16)], axis=axis), pads)

        k_bf = cat(k_past.reshape(b, n_past, KV_WIDTH), k_new, 1)
        vt = cat(jnp.transpose(v_past, (0, 2, 3, 1)).reshape(b, KV_WIDTH, n_past),
                 jnp.swapaxes(v_new, 1, 2), 2)
        ki_bf = cat(ki_past, ki_new, 1)
    ob = _dsa_call(qi, wi, qbs, k_bf, vt, ki_bf, tk, pos0)
    y = _out_call(x, oa, ob, mod, norm2, norm_f, wo_bf, w1_bf, w2_bf, bb, tt)
    return (y, k_new.reshape(b, t, B_KV_HEADS, B_HD), v_new.reshape(b, t, B_KV_HEADS, B_HD),
            ki_new, s_new)


def _tiling(b, t):
    if t >= ROW_TILE:
        assert t % ROW_TILE == 0
        bb, tt = 1, ROW_TILE
    else:
        bb, tt = min(b, ROW_TILE // t), t
        assert b % bb == 0
    return dict(bb=bb, tt=tt, cc=min(HGRN_CHUNK, t))


def kernel(x_prompt, x_sample, cache_k, cache_v, cache_k_idx, state_hgrn, c_prompt, c_sample,
           w_mod, b_mod, norm1, w_in, lb_logits, g_norm_a, w_out, norm2, w_ff1, w_ff2, norm_f):
    depth = w_in.shape[0]
    assert depth == 1, "kernel is written for the single-layer configuration"
    lb_all = jnp.cumsum(jax.nn.softmax(lb_logits.astype(F32), axis=0), axis=0)
    bp, tp, _ = x_prompt.shape
    bs, ts, _ = x_sample.shape
    past = cache_k.shape[2]
    l = 0
    mod = _mod_call(jnp.concatenate([c_prompt, c_sample], axis=0), w_mod[l], b_mod[l])
    w_main = w_in[l][:, :OFF_KI].astype(BF16)
    w_tail = jnp.pad(w_in[l][:, OFF_KI:].astype(BF16), ((0, 0), (0, LANES - (IN_WIDTH - OFF_KI))))
    weights = (norm1[l], w_main, w_tail, lb_all[l], g_norm_a[l], w_out[l].astype(BF16), norm2[l],
               w_ff1[l].astype(BF16), w_ff2[l].astype(BF16), norm_f)
    s0 = jnp.zeros((bp, A_HEADS, A_DK, A_DV), F32)
    yp, kp, vp, kip, sp = _layer(x_prompt, mod[:bp], 0, s0, None, None, None, weights,
                                 **_tiling(bp, tp))
    ys, ks, vs, kis, ss = _layer(x_sample, mod[bp:], past, state_hgrn[l], cache_k[l],
                                 cache_v[l], cache_k_idx[l], weights, **_tiling(bs, ts))
    return (yp, ys, kp[None], vp[None], kip[None], sp[None],
            ks[None], vs[None], kis[None], ss[None])
```
